```python
import math
import jax, jax.numpy as jnp
from jax import lax
import numpy as np

D_MODEL = 1024
BATCH = 2
SEQ = 8192
DEPTH = 2

CHUNK = 64
Q_BLOCK = 128
N_META = 16
DIFF_HEADS = 4
DIFF_QK_DIM = 64
DIFF_V_DIM = 2 * DIFF_QK_DIM
FOX_HEADS = 4
FOX_HEAD_DIM = 64
FOX_FORGET_BIAS = 2.0
GDN_HEADS = 4
GDN_HEAD_DIM = 64
CONV_K = 4
N_EXPERTS = 32
TOP_K = 4
D_EXPERT = D_MODEL
MOE_BLOCK = 128
SWIGLU_LIMIT = 7.0
SWIGLU_ALPHA = 1.702
DEEPNORM_ALPHA = (2 * DEPTH) ** 0.25
DEEPNORM_BETA = (8 * DEPTH) ** -0.25
MASK_VALUE = -1e30

DIFF_QK_WIDTH = DIFF_HEADS * 2 * DIFF_QK_DIM
DIFF_V_WIDTH = DIFF_HEADS * DIFF_V_DIM
FOX_WIDTH = FOX_HEADS * FOX_HEAD_DIM
GDN_WIDTH = GDN_HEADS * GDN_HEAD_DIM
GDN_CONV_CH = 3 * GDN_WIDTH
MIX_WIDTH = DIFF_V_WIDTH + FOX_WIDTH + GDN_WIDTH
IN_PROJ_SIZES = (DIFF_QK_WIDTH, DIFF_QK_WIDTH, DIFF_V_WIDTH, FOX_WIDTH, FOX_WIDTH, FOX_WIDTH, FOX_HEADS,
                 GDN_CONV_CH, GDN_HEADS, GDN_HEADS, GDN_WIDTH)
N_PROJ = 3340

kernel_name = 'hybrid_diff_fox_gdn_moe_deepnorm'


def _split_points():
    return [int(s) for s in np.cumsum(IN_PROJ_SIZES)[:-1]]


def _in_proj_column_scale():
    scale = np.ones((N_PROJ,), np.float32)
    offs = np.concatenate([[0], np.cumsum(IN_PROJ_SIZES)])
    for seg in (2, 5):
        scale[offs[seg]:offs[seg + 1]] = DEEPNORM_BETA
    scale[offs[7] + 2 * GDN_WIDTH:offs[8]] = DEEPNORM_BETA
    return scale


def _diff_lambda_init(layer):
    return 0.8 - 0.6 * math.exp(-0.3 * layer)


def _layer_norm(x, g, b, eps=1e-5):
    xf = x.astype(jnp.float32)
    mu = jnp.mean(xf, -1, keepdims=True)
    var = jnp.mean(jnp.square(xf - mu), -1, keepdims=True)
    return ((xf - mu) * lax.rsqrt(var + eps) * g + b).astype(x.dtype)


def _rms_norm(x, g, eps):
    xf = x.astype(jnp.float32)
    return (xf * lax.rsqrt(jnp.mean(jnp.square(xf), -1, keepdims=True) + eps) * g).astype(x.dtype)


def _l2_normalize(x, eps=1e-6):
    return x * lax.rsqrt(jnp.sum(jnp.square(x), -1, keepdims=True) + eps)


def _sweep_query_blocks(block_fn, seq_len):
    out = lax.map(block_fn, jnp.arange(seq_len // Q_BLOCK))
    nb, b, h, qb, dv = out.shape
    return out.transpose(1, 2, 0, 3, 4).reshape(b, h, nb * qb, dv)


def _differential_attention(q, k, v, lam_vecs, subln_g, lambda_init, key_valid):
    b, l, _ = q.shape
    q = q.reshape(b, l, DIFF_HEADS, 2, DIFF_QK_DIM).transpose(3, 0, 2, 1, 4)
    k = k.reshape(b, l, DIFF_HEADS, 2, DIFF_QK_DIM).transpose(3, 0, 2, 1, 4)
    v = v.reshape(b, l, DIFF_HEADS, DIFF_V_DIM).transpose(0, 2, 1, 3)
    lv = lam_vecs.astype(jnp.float32)
    lam = jnp.exp(jnp.sum(lv[0] * lv[1])) - jnp.exp(jnp.sum(lv[2] * lv[3])) + lambda_init
    key_chunk = jnp.arange(l) // CHUNK
    scale = DIFF_QK_DIM ** -0.5

    def block(i):
        qi = lax.dynamic_slice_in_dim(q, i * Q_BLOCK, Q_BLOCK, axis=3)
        q_chunk = (i * Q_BLOCK + jnp.arange(Q_BLOCK)) // CHUNK
        visible = (key_chunk[None, :] <= q_chunk[:, None]) & key_valid[None, :]
        s = jnp.einsum('mbhqd,mbhkd->mbhqk', qi, k).astype(jnp.float32) * scale
        p = jax.nn.softmax(jnp.where(visible, s, MASK_VALUE), axis=-1)
        p = p[0] - lam * p[1]
        return jnp.einsum('bhqk,bhkd->bhqd', p.astype(v.dtype), v)

    o = _sweep_query_blocks(block, l)
    o = _rms_norm(o, subln_g, 1e-5) * (1.0 - lambda_init)
    return o.transpose(0, 2, 1, 3).reshape(b, l, DIFF_V_WIDTH)


def _forgetting_attention(q, k, v, f_logit, f_bias, key_valid):
    b, l, _ = q.shape

    def heads(t):
        return t.reshape(b, l, FOX_HEADS, FOX_HEAD_DIM).transpose(0, 2, 1, 3)

    q, k, v = heads(q), heads(k), heads(v)
    log_f = jax.nn.log_sigmoid((f_logit + f_bias).astype(jnp.float32))
    cum = jnp.cumsum(log_f, axis=1).transpose(0, 2, 1)
    key_pos = jnp.arange(l)
    scale = FOX_HEAD_DIM ** -0.5

    def block(i):
        start = i * Q_BLOCK
        qi = lax.dynamic_slice_in_dim(q, start, Q_BLOCK, axis=2)
        ci = lax.dynamic_slice_in_dim(cum, start, Q_BLOCK, axis=2)
        q_pos = start + jnp.arange(Q_BLOCK)
        visible = (key_pos[None, :] <= q_pos[:, None]) & key_valid[None, :]
        s = (jnp.einsum('bhqd,bhkd->bhqk', qi, k).astype(jnp.float32) * scale
             + ci[..., :, None] - cum[..., None, :])
        p = jax.nn.softmax(jnp.where(visible, s, MASK_VALUE), axis=-1)
        return jnp.einsum('bhqk,bhkd->bhqd', p.astype(v.dtype), v)

    o = _sweep_query_blocks(block, l)
    return o.transpose(0, 2, 1, 3).reshape(b, l, FOX_WIDTH)


def _causal_depthwise_conv(x, w):
    return lax.conv_general_dilated(x, w[:, None, :], window_strides=(1,), padding=[(CONV_K - 1, 0)],
                                    dimension_numbers=('NWC', 'WIO', 'NWC'),
                                    feature_group_count=x.shape[-1])


def _unit_lower_solve(a, rhs):
    return lax.linalg.triangular_solve(a, rhs, left_side=True, lower=True, unit_diagonal=True)


def _chunk_gated_delta_rule(q, k, v, beta, g):
    b, l, h, dk = q.shape
    dv = v.shape[-1]
    n = l // CHUNK

    def chunks(t):
        return t.reshape(b, n, CHUNK, h, -1).transpose(0, 3, 1, 2, 4)

    q, k, v = chunks(q), chunks(k), chunks(v)
    beta = beta.reshape(b, n, CHUNK, h).transpose(0, 3, 1, 2)
    gc = jnp.cumsum(g.reshape(b, n, CHUNK, h).transpose(0, 3, 1, 2), axis=-1)
    causal = jnp.tril(jnp.ones((CHUNK, CHUNK), bool))
    strict = jnp.tril(jnp.ones((CHUNK, CHUNK), bool), -1)
    decay = jnp.exp(jnp.where(causal, gc[..., :, None] - gc[..., None, :], -jnp.inf))
    a = jnp.where(strict, jnp.einsum('bhncd,bhnsd->bhncs', k, k) * beta[..., :, None] * decay, 0.0)
    a = a + jnp.eye(CHUNK, dtype=a.dtype)
    u_base = _unit_lower_solve(a, v * beta[..., None])
    w_dec = _unit_lower_solve(a, k * (beta * jnp.exp(gc))[..., None])
    qk = jnp.einsum('bhncd,bhnsd->bhncs', q, k) * decay
    q_dec = q * jnp.exp(gc)[..., None]
    k_dec = k * jnp.exp(gc[..., -1:] - gc)[..., None]
    chunk_decay = jnp.exp(gc[..., -1])

    def step(state, xs):
        u_n, w_n, qk_n, qd_n, kd_n, cd_n = xs
        v_new = u_n - jnp.einsum('bhcd,bhdv->bhcv', w_n, state)
        o_n = jnp.einsum('bhcd,bhdv->bhcv', qd_n, state) + jnp.einsum('bhcs,bhsv->bhcv', qk_n, v_new)
        state = state * cd_n[..., None, None] + jnp.einsum('bhcd,bhcv->bhdv', kd_n, v_new)
        return state, o_n

    xs = tuple(jnp.moveaxis(t, 2, 0) for t in (u_base, w_dec, qk, q_dec, k_dec, chunk_decay))
    state0 = jnp.zeros((b, h, dk, dv), jnp.float32)
    _, o = lax.scan(step, state0, xs)
    return o.transpose(1, 0, 3, 2, 4).reshape(b, l, h, dv)


def _gated_deltanet(qkv, b_logit, a_logit, z, conv_w, a_log, dt_bias, norm_g):
    b, l, _ = qkv.shape
    qkv = jax.nn.silu(_causal_depthwise_conv(qkv, conv_w))
    q, k, v = jnp.split(qkv.astype(jnp.float32), 3, axis=-1)

    def heads(t):
        return t.reshape(b, l, GDN_HEADS, GDN_HEAD_DIM)

    q = _l2_normalize(heads(q)) * GDN_HEAD_DIM ** -0.5
    k = _l2_normalize(heads(k))
    beta = jax.nn.sigmoid(b_logit.astype(jnp.float32))
    g = -jnp.exp(a_log.astype(jnp.float32)) * jax.nn.softplus(a_logit.astype(jnp.float32) + dt_bias)
    o = _chunk_gated_delta_rule(q, k, heads(v), beta, g).astype(z.dtype)
    o = _rms_norm(o, norm_g, 1e-6) * jax.nn.silu(heads(z))
    return o.reshape(b, l, GDN_WIDTH)


def _hybrid_mixer(h, w_in, diff_lambda, diff_subln_g, fox_forget_b, gdn_conv_w, gdn_a_log, gdn_dt_bias,
                  gdn_norm_g, w_out, lambda_init, key_valid):
    proj = h @ w_in
    dq, dk, dv, fq, fk, fv, ff, gqkv, gb, ga, gz = jnp.split(proj, _split_points(), axis=-1)
    y_diff = _differential_attention(dq, dk, dv, diff_lambda, diff_subln_g, lambda_init, key_valid)
    y_fox = _forgetting_attention(fq, fk, fv, ff, fox_forget_b, key_valid)
    y_gdn = _gated_deltanet(gqkv, gb, ga, gz, gdn_conv_w, gdn_a_log, gdn_dt_bias, gdn_norm_g)
    return jnp.concatenate([y_diff, y_fox, y_gdn], axis=-1) @ w_out


def _moe_ffn(h, router_w, router_b, w1, b1, w2, b2):
    b, l, d = h.shape
    xt = h.reshape(-1, d)
    t = xt.shape[0]
    logits = (xt @ router_w + router_b).astype(jnp.float32)
    top_val, top_idx = lax.top_k(logits, TOP_K)
    gate = jax.nn.softmax(top_val, axis=-1)
    e_flat = top_idx.reshape(-1)
    tok_flat = jnp.arange(t * TOP_K) // TOP_K
    order = jnp.argsort(e_flat)
    e_sorted = e_flat[order]
    counts = jnp.bincount(e_flat, length=N_EXPERTS)
    padded = (counts + MOE_BLOCK - 1) // MOE_BLOCK * MOE_BLOCK
    pad_end = jnp.cumsum(padded)
    pad_start = pad_end - padded
    start = jnp.cumsum(counts) - counts
    dest = pad_start[e_sorted] + jnp.arange(t * TOP_K) - start[e_sorted]
    n_slots = -(-(t * TOP_K + N_EXPERTS * (MOE_BLOCK - 1)) // MOE_BLOCK) * MOE_BLOCK
    n_blocks = n_slots // MOE_BLOCK
    slot_tok = jnp.full((n_slots,), t, jnp.int32).at[dest].set(tok_flat[order])
    slot_gate = jnp.zeros((n_slots,), jnp.float32).at[dest].set(gate.reshape(-1)[order])
    block_expert = jnp.minimum(jnp.searchsorted(pad_end, jnp.arange(n_blocks) * MOE_BLOCK, side='right'),
                               N_EXPERTS - 1)
    x_pad = jnp.concatenate([xt, jnp.zeros((1, d), xt.dtype)], axis=0)
    xs = x_pad[slot_tok].reshape(n_blocks, MOE_BLOCK, d)

    def expert_block(args):
        xb, e = args
        hid = xb @ w1[e] + b1[e]
        glu, lin = jnp.split(hid, 2, axis=-1)
        glu = jnp.minimum(glu, SWIGLU_LIMIT)
        lin = jnp.clip(lin, -SWIGLU_LIMIT, SWIGLU_LIMIT)
        act = glu * jax.nn.sigmoid(SWIGLU_ALPHA * glu) * (lin + 1.0)
        return act @ w2[e] + b2[e]

    ys = lax.map(expert_block, (xs, block_expert)).reshape(n_slots, d)
    out = jnp.zeros((t + 1, d), ys.dtype).at[slot_tok].add(ys * slot_gate[:, None].astype(ys.dtype))
    return out[:t].reshape(b, l, d)


def setup_inputs(seed: int = 0) -> dict:
    key = jax.random.key(seed)
    ks = jax.random.split(key, 23)
    f32 = jnp.float32

    def normal(k, shape, scale):
        return jax.random.normal(k, shape, f32) * scale

    x = normal(ks[0], (BATCH, SEQ, D_MODEL), 1.0)
    meta_tokens = normal(ks[1], (N_META, D_MODEL), 1.0)
    ln_in_g = 1.0 + normal(ks[2], (D_MODEL,), 0.02)
    ln_in_b = normal(ks[3], (D_MODEL,), 0.02)
    w_in = normal(ks[4], (DEPTH, D_MODEL, N_PROJ), D_MODEL ** -0.5) * jnp.asarray(_in_proj_column_scale())
    diff_lambda = normal(ks[5], (DEPTH, 4, DIFF_QK_DIM), 0.1)
    diff_subln_g = 1.0 + normal(ks[6], (DEPTH, DIFF_V_DIM), 0.02)
    fox_forget_b = FOX_FORGET_BIAS + normal(ks[7], (DEPTH, FOX_HEADS), 0.1)
    gdn_conv_w = normal(ks[8], (DEPTH, CONV_K, GDN_CONV_CH), CONV_K ** -0.5)
    gdn_a_log = jnp.log(jax.random.uniform(ks[9], (DEPTH, GDN_HEADS), f32, 1.0, 16.0))
    dt = jnp.exp(jax.random.uniform(ks[10], (DEPTH, GDN_HEADS), f32, math.log(1e-3), math.log(0.1)))
    gdn_dt_bias = dt + jnp.log(-jnp.expm1(-dt))
    gdn_norm_g = 1.0 + normal(ks[11], (DEPTH, GDN_HEAD_DIM), 0.02)
    w_out = normal(ks[12], (DEPTH, MIX_WIDTH, D_MODEL), MIX_WIDTH ** -0.5 * DEEPNORM_BETA)
    ln1_g = 1.0 + normal(ks[13], (DEPTH, D_MODEL), 0.02)
    ln1_b = normal(ks[14], (DEPTH, D_MODEL), 0.02)
    router_w = normal(ks[15], (DEPTH, D_MODEL, N_EXPERTS), D_MODEL ** -0.5)
    router_b = normal(ks[16], (DEPTH, N_EXPERTS), 0.01)
    expert_w1 = normal(ks[17], (DEPTH, N_EXPERTS, D_MODEL, 2 * D_EXPERT), D_MODEL ** -0.5 * DEEPNORM_BETA)
    expert_b1 = normal(ks[18], (DEPTH, N_EXPERTS, 2 * D_EXPERT), 0.02)
    expert_w2 = normal(ks[19], (DEPTH, N_EXPERTS, D_EXPERT, D_MODEL), D_EXPERT ** -0.5 * DEEPNORM_BETA)
    expert_b2 = normal(ks[20], (DEPTH, N_EXPERTS, D_MODEL), 0.02)
    ln2_g = 1.0 + normal(ks[21], (DEPTH, D_MODEL), 0.02)
    ln2_b = normal(ks[22], (DEPTH, D_MODEL), 0.02)
    return {'x': x, 'meta_tokens': meta_tokens, 'ln_in_g': ln_in_g, 'ln_in_b': ln_in_b, 'w_in': w_in,
            'diff_lambda': diff_lambda, 'diff_subln_g': diff_subln_g, 'fox_forget_b': fox_forget_b,
            'gdn_conv_w': gdn_conv_w, 'gdn_a_log': gdn_a_log, 'gdn_dt_bias': gdn_dt_bias,
            'gdn_norm_g': gdn_norm_g, 'w_out': w_out, 'ln1_g': ln1_g, 'ln1_b': ln1_b,
            'router_w': router_w, 'router_b': router_b, 'expert_w1': expert_w1, 'expert_b1': expert_b1,
            'expert_w2': expert_w2, 'expert_b2': expert_b2, 'ln2_g': ln2_g, 'ln2_b': ln2_b}


def reference(x, meta_tokens, ln_in_g, ln_in_b, w_in, diff_lambda, diff_subln_g, fox_forget_b, gdn_conv_w,
              gdn_a_log, gdn_dt_bias, gdn_norm_g, w_out, ln1_g, ln1_b, router_w, router_b, expert_w1,
              expert_b1, expert_w2, expert_b2, ln2_g, ln2_b):
    b, s, d = x.shape
    n_lead = (-(s + N_META)) % Q_BLOCK
    l = n_lead + N_META + s
    stream = jnp.concatenate([jnp.zeros((b, n_lead, d), x.dtype),
                              jnp.broadcast_to(meta_tokens.astype(x.dtype)[None], (b, N_META, d)),
                              x], axis=1)
    stream = _layer_norm(stream, ln_in_g, ln_in_b)
    key_valid = jnp.arange(l) >= n_lead
    in_mask = key_valid[None, :, None].astype(x.dtype)
    for layer in range(DEPTH):
        mix = _hybrid_mixer(stream * in_mask, w_in[layer], diff_lambda[layer], diff_subln_g[layer],
                            fox_forget_b[layer], gdn_conv_w[layer], gdn_a_log[layer], gdn_dt_bias[layer],
                            gdn_norm_g[layer], w_out[layer], _diff_lambda_init(layer), key_valid)
        stream = _layer_norm(DEEPNORM_ALPHA * stream + mix, ln1_g[layer], ln1_b[layer])
        ffn = _moe_ffn(stream, router_w[layer], router_b[layer], expert_w1[layer], expert_b1[layer],
                       expert_w2[layer], expert_b2[layer])
        stream = _layer_norm(DEEPNORM_ALPHA * stream + ffn, ln2_g[layer], ln2_b[layer])
    return stream[:, n_lead + N_META:]
```

```python
import functools
import math

import numpy as np
import jax
import jax.numpy as jnp
from jax import lax
from jax.experimental import pallas as pl
from jax.experimental.pallas import tpu as pltpu

F32 = jnp.float32
BF16 = jnp.bfloat16
I32 = jnp.int32

CHUNK = 64
CHUNK_SHIFT = 6
Q_BLOCK = 128
N_META = 16
DIFF_HEADS = 4
DIFF_QK_DIM = 64
DIFF_V_DIM = 128
FOX_HEADS = 4
FOX_HEAD_DIM = 64
GDN_HEADS = 4
GDN_HEAD_DIM = 64
CONV_K = 4
N_EXPERTS = 32
TOP_K = 4
SWIGLU_LIMIT = 7.0
SWIGLU_ALPHA = 1.702
MASK_VALUE = -1e30

DIFF_QK_WIDTH = DIFF_HEADS * 2 * DIFF_QK_DIM
DIFF_V_WIDTH = DIFF_HEADS * DIFF_V_DIM
FOX_WIDTH = FOX_HEADS * FOX_HEAD_DIM
GDN_WIDTH = GDN_HEADS * GDN_HEAD_DIM
GDN_CONV_CH = 3 * GDN_WIDTH
IN_PROJ_SIZES = (DIFF_QK_WIDTH, DIFF_QK_WIDTH, DIFF_V_WIDTH, FOX_WIDTH, FOX_WIDTH, FOX_WIDTH, FOX_HEADS,
                 GDN_CONV_CH, GDN_HEADS, GDN_HEADS, GDN_WIDTH)

LANES = 128
SUBLANES = 8
VMEM_LIMIT = 56 * 1024 * 1024

GATE_FF = 0
GATE_BETA = 4
GATE_DECAY = 8

ATTN_WIDTH = 2 * DIFF_QK_WIDTH + DIFF_V_WIDTH + 3 * FOX_WIDTH
GDN_SLAB = GDN_CONV_CH + GDN_WIDTH


def _cparams(*sem):
    return pltpu.CompilerParams(dimension_semantics=tuple(sem), vmem_limit_bytes=VMEM_LIMIT)


def _dot(a, b):
    return jnp.dot(a.astype(BF16), b.astype(BF16), preferred_element_type=F32)


def _dot_nt(a, b):
    return lax.dot_general(a.astype(BF16), b.astype(BF16), (((1,), (1,)), ((), ())),
                           preferred_element_type=F32)


def _split2(x):
    hi = x.astype(BF16)
    lo = (x - hi.astype(F32)).astype(BF16)
    return hi, lo


def _split3(x):
    hi = x.astype(BF16)
    r = x - hi.astype(F32)
    mid = r.astype(BF16)
    lo = (r - mid.astype(F32)).astype(BF16)
    return hi, mid, lo


def _dot3(a, b):
    ah, al = _split2(a)
    bh, bl = _split2(b)
    return (jnp.dot(ah, bh, preferred_element_type=F32) + jnp.dot(ah, bl, preferred_element_type=F32)
            + jnp.dot(al, bh, preferred_element_type=F32))


def _dot_exact_lhs(a01, b):
    a = a01.astype(BF16)
    hi, mid, lo = _split3(b)
    return (jnp.dot(a, hi, preferred_element_type=F32) + jnp.dot(a, mid, preferred_element_type=F32)
            + jnp.dot(a, lo, preferred_element_type=F32))


_BDIMS = (((2,), (1,)), ((0,), (0,)))
_BDIMS_NT = (((2,), (2,)), ((0,), (0,)))


def _bdot(a, b):
    return lax.dot_general(a.astype(BF16), b.astype(BF16), _BDIMS, preferred_element_type=F32)


def _bdot_nt(a, b):
    return lax.dot_general(a.astype(BF16), b.astype(BF16), _BDIMS_NT, preferred_element_type=F32)


def _bdot3(a, b):
    ah, al = _split2(a)
    bh, bl = _split2(b)
    return (lax.dot_general(ah, bh, _BDIMS, preferred_element_type=F32)
            + lax.dot_general(ah, bl, _BDIMS, preferred_element_type=F32)
            + lax.dot_general(al, bh, _BDIMS, preferred_element_type=F32))


def _bdot_exact_lhs(a01, b):
    a = a01.astype(BF16)
    hi, mid, lo = _split3(b)
    return (lax.dot_general(a, hi, _BDIMS, preferred_element_type=F32)
            + lax.dot_general(a, mid, _BDIMS, preferred_element_type=F32)
            + lax.dot_general(a, lo, _BDIMS, preferred_element_type=F32))


def _layer_norm(x, g, b, eps=1e-5):
    mu = jnp.mean(x, axis=-1, keepdims=True)
    xc = x - mu
    var = jnp.mean(xc * xc, axis=-1, keepdims=True)
    return xc * lax.rsqrt(var + eps) * g + b


def _sigmoid(x):
    return 1.0 / (1.0 + jnp.exp(-x))


def _softplus(x):
    return jnp.maximum(x, 0.0) + jnp.log1p(jnp.exp(-jnp.abs(x)))


def _log_sigmoid(x):
    return -_softplus(-x)


def _ln_in_kernel(x_ref, meta_ref, g_ref, b_ref, o_ref, *, n_lead):
    i = pl.program_id(1)
    g = g_ref[...]
    b = b_ref[...]

    @pl.when(i == 0)
    def _():
        o_ref[0, :n_lead, :] = jnp.broadcast_to(b, (n_lead, b.shape[-1]))
        o_ref[0, n_lead:, :] = _layer_norm(meta_ref[...], g, b)

    @pl.when(i > 0)
    def _():
        o_ref[0] = _layer_norm(x_ref[0], g, b)


def _ln_in(x, meta, g, b, n_lead):
    bsz, s, d = x.shape
    l = n_lead + N_META + s
    blk = n_lead + N_META
    return pl.pallas_call(
        functools.partial(_ln_in_kernel, n_lead=n_lead),
        out_shape=jax.ShapeDtypeStruct((bsz, l, d), F32),
        grid=(bsz, l // blk),
        in_specs=[pl.BlockSpec((1, blk, d), lambda bb, i: (bb, jnp.maximum(i - 1, 0), 0)),
                  pl.BlockSpec((N_META, d), lambda bb, i: (0, 0)),
                  pl.BlockSpec((1, d), lambda bb, i: (0, 0)),
                  pl.BlockSpec((1, d), lambda bb, i: (0, 0))],
        out_specs=pl.BlockSpec((1, blk, d), lambda bb, i: (bb, i, 0)),
        compiler_params=_cparams("arbitrary", "arbitrary"),
        name="ln_in",
    )(x, meta, g.reshape(1, d), b.reshape(1, d))


PROJ_COL_CHUNK = 768


def _proj_kernel(x_ref, wa_ref, wg_ref, ws_ref, oa_ref, og_ref, os_ref, *, tm, n_lead):
    i = pl.program_id(1)
    pos = i * tm + lax.broadcasted_iota(I32, (tm, 1), 0)
    x = jnp.where(pos >= n_lead, x_ref[...], 0.0)
    xb = x.astype(BF16)
    for c in range(0, ATTN_WIDTH, PROJ_COL_CHUNK):
        oa_ref[:, c:c + PROJ_COL_CHUNK] = jnp.dot(
            xb, wa_ref[:, c:c + PROJ_COL_CHUNK], preferred_element_type=F32).astype(BF16)
    og_ref[...] = jnp.dot(xb, wg_ref[...], preferred_element_type=F32)
    os_ref[...] = _dot3(x, ws_ref[...])


def _in_proj(stream, wa, wg, ws, seq_len, n_lead, tm):
    t_all, d = stream.shape
    nb = seq_len // tm
    row = lambda bb, i: (bb * nb + i, 0)
    const = lambda bb, i: (0, 0)
    return pl.pallas_call(
        functools.partial(_proj_kernel, tm=tm, n_lead=n_lead),
        out_shape=(jax.ShapeDtypeStruct((t_all, ATTN_WIDTH), BF16),
                   jax.ShapeDtypeStruct((t_all, GDN_SLAB), F32),
                   jax.ShapeDtypeStruct((t_all, LANES), F32)),
        grid=(t_all // seq_len, nb),
        in_specs=[pl.BlockSpec((tm, d), row),
                  pl.BlockSpec((d, ATTN_WIDTH), const),
                  pl.BlockSpec((d, GDN_SLAB), const),
                  pl.BlockSpec((d, LANES), const)],
        out_specs=(pl.BlockSpec((tm, ATTN_WIDTH), row),
                   pl.BlockSpec((tm, GDN_SLAB), row),
                   pl.BlockSpec((tm, LANES), row)),
        compiler_params=_cparams("arbitrary", "arbitrary"),
        name="in_proj",
    )(stream, wa, wg, ws)


def _online_softmax_step(s, v, m_ref, l_ref, a_ref):
    m_old = m_ref[...]
    m_new = jnp.maximum(m_old, jnp.max(s, axis=-1, keepdims=True))
    alpha = jnp.exp(m_old - m_new)
    p = jnp.exp(s - m_new)
    l_ref[...] = alpha * l_ref[...] + jnp.sum(p, axis=-1, keepdims=True)
    a_ref[...] = alpha * a_ref[...] + jnp.dot(p.astype(BF16), v, preferred_element_type=F32)
    m_ref[...] = m_new


def _init_softmax_state(m_ref, l_ref, a_ref):
    m_ref[...] = jnp.full(m_ref.shape, MASK_VALUE, F32)
    l_ref[...] = jnp.zeros(l_ref.shape, F32)
    a_ref[...] = jnp.zeros(a_ref.shape, F32)


def _diff_attn_kernel(lam_ref, q_ref, k_ref, v_ref, g_ref, o_ref,
                      m0, l0, a0, m1, l1, a1, *, tb, n_lead, out_scale):
    i = pl.program_id(2)
    lane = lax.broadcasted_iota(I32, (1, LANES), 1)
    q = q_ref[...]
    zero = jnp.zeros_like(q)
    q0 = jnp.where(lane < DIFF_QK_DIM, q, zero)
    q1 = jnp.where(lane >= DIFF_QK_DIM, q, zero)
    _init_softmax_state(m0, l0, a0)
    _init_softmax_state(m1, l1, a1)

    def step(j, masked):
        start = pl.multiple_of(j * tb, tb)
        k = k_ref[pl.ds(start, tb), :]
        v = v_ref[pl.ds(start, tb), :]
        s0 = _dot_nt(q0, k)
        s1 = _dot_nt(q1, k)
        if masked:
            qpos = i * tb + lax.broadcasted_iota(I32, (tb, 1), 0)
            kpos = j * tb + lax.broadcasted_iota(I32, (1, tb), 1)
            vis = jnp.logical_and(kpos >> CHUNK_SHIFT <= qpos >> CHUNK_SHIFT, kpos >= n_lead)
            s0 = jnp.where(vis, s0, MASK_VALUE)
            s1 = jnp.where(vis, s1, MASK_VALUE)
        _online_softmax_step(s0, v, m0, l0, a0)
        _online_softmax_step(s1, v, m1, l1, a1)

    step(0, True)

    def body(j, c):
        step(j, False)
        return c

    lax.fori_loop(1, i, body, 0)

    @pl.when(i > 0)
    def _():
        step(i, True)

    lam = lam_ref[0]
    o = a0[...] / l0[...] - lam * (a1[...] / l1[...])
    o = o * lax.rsqrt(jnp.mean(o * o, axis=-1, keepdims=True) + 1e-5) * g_ref[...] * out_scale
    o_ref[...] = o.astype(o_ref.dtype)


def _diff_attention(attn, lam, subln_g, seq_len, n_lead, lambda_init, tb):
    t_all = attn.shape[0]
    bsz = t_all // seq_len
    nq = seq_len // tb
    kcol = DIFF_QK_WIDTH // LANES
    vcol = 2 * DIFF_QK_WIDTH // LANES
    scratch = []
    for _ in range(2):
        scratch += [pltpu.VMEM((tb, 1), F32), pltpu.VMEM((tb, 1), F32), pltpu.VMEM((tb, DIFF_V_DIM), F32)]
    return pl.pallas_call(
        functools.partial(_diff_attn_kernel, tb=tb, n_lead=n_lead, out_scale=1.0 - lambda_init),
        out_shape=jax.ShapeDtypeStruct((t_all, DIFF_V_WIDTH), BF16),
        grid_spec=pltpu.PrefetchScalarGridSpec(
            num_scalar_prefetch=1,
            grid=(bsz, DIFF_HEADS, nq),
            in_specs=[pl.BlockSpec((tb, LANES), lambda b, h, i, lam: (b * nq + i, h)),
                      pl.BlockSpec((seq_len, LANES), lambda b, h, i, lam: (b, kcol + h)),
                      pl.BlockSpec((seq_len, LANES), lambda b, h, i, lam: (b, vcol + h)),
                      pl.BlockSpec((1, DIFF_V_DIM), lambda b, h, i, lam: (0, 0))],
            out_specs=pl.BlockSpec((tb, DIFF_V_DIM), lambda b, h, i, lam: (b * nq + i, h)),
            scratch_shapes=scratch),
        compiler_params=_cparams("arbitrary", "arbitrary", "arbitrary"),
        name="diff_attention",
    )(lam, attn, attn, attn, subln_g.reshape(1, DIFF_V_DIM))


def _fox_attn_kernel(q_ref, k_ref, v_ref, cc_ref, cr_ref, o_ref, *scratch, tb, n_lead):
    i = pl.program_id(1)
    lane = lax.broadcasted_iota(I32, (1, LANES), 1)
    pairs = FOX_HEADS // 2
    qs = []
    for p in range(pairs):
        q = q_ref[:, p * LANES:(p + 1) * LANES]
        zero = jnp.zeros_like(q)
        qs.append((jnp.where(lane < FOX_HEAD_DIM, q, zero), jnp.where(lane >= FOX_HEAD_DIM, q, zero)))
    for hh in range(FOX_HEADS):
        _init_softmax_state(*scratch[3 * hh:3 * hh + 3])
    cc = cc_ref[...]

    def step(j, masked):
        start = pl.multiple_of(j * tb, tb)
        if masked:
            qpos = i * tb + lax.broadcasted_iota(I32, (tb, 1), 0)
            kpos = j * tb + lax.broadcasted_iota(I32, (1, tb), 1)
            vis = jnp.logical_and(kpos <= qpos, kpos >= n_lead)
        for p in range(pairs):
            k = k_ref[pl.ds(start, tb), p * LANES:(p + 1) * LANES]
            v = v_ref[pl.ds(start, tb), p * LANES:(p + 1) * LANES]
            for r in range(2):
                hh = 2 * p + r
                ci = cc[:, GATE_FF + hh:GATE_FF + hh + 1]
                ck = cr_ref[0, hh:hh + 1, pl.ds(start, tb)]
                s = _dot_nt(qs[p][r], k) + (ci - ck)
                if masked:
                    s = jnp.where(vis, s, MASK_VALUE)
                _online_softmax_step(s, v, *scratch[3 * hh:3 * hh + 3])

    step(0, True)

    def body(j, c):
        step(j, False)
        return c

    lax.fori_loop(1, i, body, 0)

    @pl.when(i > 0)
    def _():
        step(i, True)

    for p in range(pairs):
        _, l_lo, a_lo = scratch[6 * p:6 * p + 3]
        _, l_hi, a_hi = scratch[6 * p + 3:6 * p + 6]
        o = jnp.where(lane < FOX_HEAD_DIM, a_lo[...] / l_lo[...], a_hi[...] / l_hi[...])
        o_ref[:, p * LANES:(p + 1) * LANES] = o.astype(o_ref.dtype)


def _fox_attention(attn, cum_col, cum_row, seq_len, n_lead, tb):
    t_all = attn.shape[0]
    bsz = t_all // seq_len
    nq = seq_len // tb
    base = (2 * DIFF_QK_WIDTH + DIFF_V_WIDTH) // FOX_WIDTH
    scratch = []
    for _ in range(FOX_HEADS):
        scratch += [pltpu.VMEM((tb, 1), F32), pltpu.VMEM((tb, 1), F32), pltpu.VMEM((tb, LANES), F32)]
    return pl.pallas_call(
        functools.partial(_fox_attn_kernel, tb=tb, n_lead=n_lead),
        out_shape=jax.ShapeDtypeStruct((t_all, FOX_WIDTH), BF16),
        grid=(bsz, nq),
        in_specs=[pl.BlockSpec((tb, FOX_WIDTH), lambda b, i: (b * nq + i, base)),
                  pl.BlockSpec((seq_len, FOX_WIDTH), lambda b, i: (b, base + 1)),
                  pl.BlockSpec((seq_len, FOX_WIDTH), lambda b, i: (b, base + 2)),
                  pl.BlockSpec((tb, LANES), lambda b, i: (b * nq + i, 0)),
                  pl.BlockSpec((1, SUBLANES, seq_len), lambda b, i: (b, 0, 0))],
        out_specs=pl.BlockSpec((tb, FOX_WIDTH), lambda b, i: (b * nq + i, 0)),
        scratch_shapes=scratch,
        compiler_params=_cparams("arbitrary", "arbitrary"),
        name="fox_attention",
    )(attn, attn, attn, cum_col, cum_row)


def _forget_cumsum_kernel(x_ref, b_ref, col_ref, row_ref, carry_ref, *, tb):
    i = pl.program_id(1)

    @pl.when(i == 0)
    def _():
        carry_ref[...] = jnp.zeros_like(carry_ref)

    log_f = _log_sigmoid(x_ref[...] + b_ref[...])
    r = lax.broadcasted_iota(I32, (tb, tb), 0)
    c = lax.broadcasted_iota(I32, (tb, tb), 1)
    tri = jnp.where(c <= r, 1.0, 0.0).astype(BF16)
    cum = _dot_exact_lhs(tri, log_f) + carry_ref[...]
    col_ref[...] = cum
    row_ref[0] = cum.T[:SUBLANES, :]
    carry_ref[...] = cum[tb - 1:tb, :]


def _forget_cumsum(gates, forget_bias_lanes, seq_len):
    t_all = gates.shape[0]
    bsz = t_all // seq_len
    tb = LANES
    nb = seq_len // tb
    return pl.pallas_call(
        functools.partial(_forget_cumsum_kernel, tb=tb),
        out_shape=(jax.ShapeDtypeStruct((t_all, LANES), F32),
                   jax.ShapeDtypeStruct((bsz, SUBLANES, seq_len), F32)),
        grid=(bsz, nb),
        in_specs=[pl.BlockSpec((tb, LANES), lambda b, i: (b * nb + i, 0)),
                  pl.BlockSpec((1, LANES), lambda b, i: (0, 0))],
        out_specs=(pl.BlockSpec((tb, LANES), lambda b, i: (b * nb + i, 0)),
                   pl.BlockSpec((1, SUBLANES, tb), lambda b, i: (b, 0, i))),
        scratch_shapes=[pltpu.VMEM((1, LANES), F32)],
        compiler_params=_cparams("arbitrary", "arbitrary"),
        name="forget_cumsum",
    )(gates, forget_bias_lanes)


def _gdn_kernel(x_ref, gate_ref, cw_ref, alog_ref, dtb_ref, ng_ref, o_ref,
                halo_ref, xs_ref, state_ref, *, rows):
    cg = pl.program_id(1)
    g_chunks = rows // CHUNK
    hd = GDN_HEAD_DIM

    @pl.when(cg == 0)
    def _():
        halo_ref[...] = jnp.zeros_like(halo_ref)
        state_ref[...] = jnp.zeros_like(state_ref)

    xin = x_ref[:, :GDN_CONV_CH]
    xs_ref[0:SUBLANES, :] = halo_ref[...]
    xs_ref[SUBLANES:SUBLANES + rows, :] = xin
    halo_ref[...] = xin[rows - SUBLANES:rows, :]
    y = jnp.zeros((rows, GDN_CONV_CH), F32)
    for tap in range(CONV_K):
        back = CONV_K - 1 - tap
        y = y + cw_ref[tap:tap + 1, :] * xs_ref[SUBLANES - back:SUBLANES - back + rows, :]
    y = y * _sigmoid(y)

    gates = gate_ref[...]
    beta_all = _sigmoid(gates)
    g_all = -jnp.exp(alog_ref[...]) * _softplus(gates + dtb_ref[...])
    rowi = lax.broadcasted_iota(I32, (rows, 1), 0) & (CHUNK - 1)
    gc_all = g_all
    shift = 1
    while shift < CHUNK:
        gc_all = gc_all + jnp.where(rowi >= shift, pltpu.roll(gc_all, shift, 0), 0.0)
        shift *= 2

    ri = lax.broadcasted_iota(I32, (CHUNK, CHUNK), 0)
    ci = lax.broadcasted_iota(I32, (CHUNK, CHUNK), 1)
    eye = jnp.where(ri == ci, 1.0, 0.0).astype(F32)
    causal = (ci <= ri)[None]
    strict = (ci < ri)[None]
    ones_b = jnp.ones((g_chunks, CHUNK, CHUNK), BF16)
    eye_b = jnp.broadcast_to(eye[None], (g_chunks, CHUNK, CHUNK))

    for h in range(GDN_HEADS):
        def head3(base):
            return y[:, base + h * hd:base + (h + 1) * hd]
        q = head3(0)
        k = head3(GDN_WIDTH)
        v = head3(2 * GDN_WIDTH)
        q = q * lax.rsqrt(jnp.sum(q * q, axis=-1, keepdims=True) + 1e-6) * (hd ** -0.5)
        k = k * lax.rsqrt(jnp.sum(k * k, axis=-1, keepdims=True) + 1e-6)
        beta = beta_all[:, GATE_BETA + h:GATE_BETA + h + 1]
        gcol = jnp.broadcast_to(gc_all[:, GATE_DECAY + h:GATE_DECAY + h + 1], (rows, hd))
        egc = jnp.exp(gcol)

        q3 = q.reshape(g_chunks, CHUNK, hd)
        k3 = k.reshape(g_chunks, CHUNK, hd)
        gcol3 = gcol.reshape(g_chunks, CHUNK, CHUNK)
        beta3 = jnp.broadcast_to(beta, (rows, hd)).reshape(g_chunks, CHUNK, hd)
        grow3 = _bdot_exact_lhs(ones_b, eye_b * gcol3)
        decay = jnp.exp(jnp.where(causal, gcol3 - grow3, MASK_VALUE))
        a = jnp.where(strict, _bdot_nt(k3, k3) * beta3 * decay, 0.0)

        x = -a
        tinv = eye_b + x
        span = 2
        while span < CHUNK:
            x = _bdot3(x, x)
            tinv = tinv + _bdot3(tinv, x)
            span *= 2

        vb = (v * beta).reshape(g_chunks, CHUNK, hd)
        kb = (k * beta * egc).reshape(g_chunks, CHUNK, hd)
        u3 = _bdot3(tinv, vb)
        w3 = _bdot3(tinv, kb)
        qk3 = _bdot_nt(q3, k3) * decay
        qd3 = (q * egc).reshape(g_chunks, CHUNK, hd)
        glast3 = gcol3[:, CHUNK - 1:CHUNK, :]
        kd3 = k3 * jnp.exp(glast3 - gcol3)
        kdt3 = _bdot_nt(eye_b, kd3)
        cd3 = jnp.exp(glast3)

        state = state_ref[h]
        outs = []
        for c in range(g_chunks):
            v_new = u3[c] - _dot(w3[c], state)
            outs.append(_dot(qd3[c], state) + _dot(qk3[c], v_new))
            state = state * cd3[c] + _dot(kdt3[c], v_new)
        state_ref[h] = state
        o = jnp.concatenate(outs, axis=0)
        o = o * lax.rsqrt(jnp.mean(o * o, axis=-1, keepdims=True) + 1e-6) * ng_ref[...]
        z = x_ref[:, GDN_CONV_CH + h * hd:GDN_CONV_CH + (h + 1) * hd]
        o_ref[:, h * hd:(h + 1) * hd] = (o * (z * _sigmoid(z))).astype(o_ref.dtype)


def _gated_deltanet(gdn_slab, gates, conv_w, a_log_lanes, dt_bias_lanes, norm_g, seq_len, rows):
    t_all = gdn_slab.shape[0]
    bsz = t_all // seq_len
    nb = seq_len // rows
    row = lambda b, i: (b * nb + i, 0)
    const = lambda b, i: (0, 0)
    return pl.pallas_call(
        functools.partial(_gdn_kernel, rows=rows),
        out_shape=jax.ShapeDtypeStruct((t_all, GDN_WIDTH), BF16),
        grid=(bsz, nb),
        in_specs=[pl.BlockSpec((rows, GDN_SLAB), row),
                  pl.BlockSpec((rows, LANES), row),
                  pl.BlockSpec((SUBLANES, GDN_CONV_CH), const),
                  pl.BlockSpec((1, LANES), const),
                  pl.BlockSpec((1, LANES), const),
                  pl.BlockSpec((1, GDN_HEAD_DIM), const)],
        out_specs=pl.BlockSpec((rows, GDN_WIDTH), row),
        scratch_shapes=[pltpu.VMEM((SUBLANES, GDN_CONV_CH), F32),
                        pltpu.VMEM((rows + SUBLANES, GDN_CONV_CH), F32),
                        pltpu.VMEM((GDN_HEADS, GDN_HEAD_DIM, GDN_HEAD_DIM), F32)],
        compiler_params=_cparams("arbitrary", "arbitrary"),
        name="gated_deltanet",
    )(gdn_slab, gates, conv_w, a_log_lanes, dt_bias_lanes, norm_g.reshape(1, GDN_HEAD_DIM))


def _mix_router_kernel(x_ref, yd_ref, yf_ref, yg_ref, wd_ref, wf_ref, wg_ref, g_ref, b_ref,
                       rw_ref, rb_ref, x1_ref, sel_ref, gate_ref, cnt_ref, carry_ref, *, tm, alpha):
    i = pl.program_id(0)

    @pl.when(i == 0)
    def _():
        carry_ref[...] = jnp.zeros_like(carry_ref)

    mix = (jnp.dot(yd_ref[...], wd_ref[...], preferred_element_type=F32)
           + jnp.dot(yf_ref[...], wf_ref[...], preferred_element_type=F32)
           + jnp.dot(yg_ref[...], wg_ref[...], preferred_element_type=F32))
    x1 = _layer_norm(alpha * x_ref[...] + mix, g_ref[...], b_ref[...])
    x1_ref[...] = x1

    lane = lax.broadcasted_iota(I32, (tm, LANES), 1)
    lane_f = lane.astype(F32)
    logits = _dot3(x1, rw_ref[...]) + rb_ref[...]
    vals = jnp.where(lane < N_EXPERTS, logits, -jnp.inf)
    onehot = jnp.zeros((tm, LANES), F32)
    sels, tops = [], []
    for _ in range(TOP_K):
        top = jnp.max(vals, axis=-1, keepdims=True)
        idx = jnp.min(jnp.where(vals == top, lane_f, float(LANES)), axis=-1, keepdims=True)
        hit = lane_f == idx
        onehot = onehot + jnp.where(hit, 1.0, 0.0)
        vals = jnp.where(hit, -jnp.inf, vals)
        sels.append((idx, hit))
        tops.append(top)
    exps = [jnp.exp(t - tops[0]) for t in tops]
    denom = exps[0] + exps[1] + exps[2] + exps[3]

    r = lax.broadcasted_iota(I32, (tm, tm), 0)
    c = lax.broadcasted_iota(I32, (tm, tm), 1)
    tri = jnp.where(c < r, 1.0, 0.0).astype(BF16)
    prefix = jnp.dot(tri, onehot.astype(BF16), preferred_element_type=F32) + carry_ref[...]
    sel_out = jnp.zeros((tm, LANES), I32)
    gate_out = jnp.zeros((tm, LANES), F32)
    for kk in range(TOP_K):
        idx, hit = sels[kk]
        rank = jnp.sum(jnp.where(hit, prefix, 0.0), axis=-1, keepdims=True).astype(I32)
        sel_out = jnp.where(lane == kk, idx.astype(I32), sel_out)
        sel_out = jnp.where(lane == TOP_K + kk, rank, sel_out)
        gate_out = jnp.where(lane == kk, exps[kk] / denom, gate_out)
    sel_ref[...] = sel_out
    gate_ref[...] = gate_out
    carry_ref[...] = carry_ref[...] + jnp.sum(onehot, axis=0, keepdims=True)
    cnt_ref[...] = jnp.broadcast_to(carry_ref[...], cnt_ref.shape)


def _mix_router(stream, yd, yf, yg, w_out_b, ln_g, ln_b, rw, rb, alpha, tm):
    t_all, d = stream.shape
    row = lambda i: (i, 0)
    const = lambda i: (0, 0)
    wd = w_out_b[:DIFF_V_WIDTH]
    wf = w_out_b[DIFF_V_WIDTH:DIFF_V_WIDTH + FOX_WIDTH]
    wg = w_out_b[DIFF_V_WIDTH + FOX_WIDTH:]
    return pl.pallas_call(
        functools.partial(_mix_router_kernel, tm=tm, alpha=alpha),
        out_shape=(jax.ShapeDtypeStruct((t_all, d), F32),
                   jax.ShapeDtypeStruct((t_all, LANES), I32),
                   jax.ShapeDtypeStruct((t_all, LANES), F32),
                   jax.ShapeDtypeStruct((SUBLANES, LANES), F32)),
        grid=(t_all // tm,),
        in_specs=[pl.BlockSpec((tm, d), row),
                  pl.BlockSpec((tm, DIFF_V_WIDTH), row),
                  pl.BlockSpec((tm, FOX_WIDTH), row),
                  pl.BlockSpec((tm, GDN_WIDTH), row),
                  pl.BlockSpec((DIFF_V_WIDTH, d), const),
                  pl.BlockSpec((FOX_WIDTH, d), const),
                  pl.BlockSpec((GDN_WIDTH, d), const),
                  pl.BlockSpec((1, d), const),
                  pl.BlockSpec((1, d), const),
                  pl.BlockSpec((d, LANES), const),
                  pl.BlockSpec((1, LANES), const)],
        out_specs=(pl.BlockSpec((tm, d), row),
                   pl.BlockSpec((tm, LANES), row),
                   pl.BlockSpec((tm, LANES), row),
                   pl.BlockSpec((SUBLANES, LANES), const)),
        scratch_shapes=[pltpu.VMEM((1, LANES), F32)],
        compiler_params=_cparams("arbitrary"),
        name="mix_router",
    )(stream, yd, yf, yg, wd, wf, wg, ln_g.reshape(1, d), ln_b.reshape(1, d), rw, rb)


DISPATCH_LAG = 16


def _dispatch_kernel(dest_ref, x_ref, xs_in_ref, xs_ref, sem, *, n_tok):
    del xs_in_ref

    def row_copy(t, kk):
        return pltpu.make_async_copy(x_ref.at[pl.ds(t, 1)], xs_ref.at[pl.ds(dest_ref[t * TOP_K + kk], 1)], sem)

    def body(t, c):
        for kk in range(TOP_K):
            row_copy(t, kk).start()

        @pl.when(t >= DISPATCH_LAG)
        def _():
            for kk in range(TOP_K):
                row_copy(t - DISPATCH_LAG, kk).wait()
        return c

    lax.fori_loop(0, n_tok, body, 0)

    def drain(t, c):
        for kk in range(TOP_K):
            row_copy(t, kk).wait()
        return c

    lax.fori_loop(max(n_tok - DISPATCH_LAG, 0), n_tok, drain, 0)


def _dispatch(dest, x1, n_slots):
    t_all, d = x1.shape
    xs0 = jnp.zeros((n_slots, d), F32)
    return pl.pallas_call(
        functools.partial(_dispatch_kernel, n_tok=t_all),
        out_shape=jax.ShapeDtypeStruct((n_slots, d), F32),
        grid_spec=pltpu.PrefetchScalarGridSpec(
            num_scalar_prefetch=1,
            grid=(1,),
            in_specs=[pl.BlockSpec(memory_space=pl.ANY), pl.BlockSpec(memory_space=pl.ANY)],
            out_specs=pl.BlockSpec(memory_space=pl.ANY),
            scratch_shapes=[pltpu.SemaphoreType.DMA]),
        input_output_aliases={2: 0},
        compiler_params=_cparams("arbitrary"),
        name="moe_dispatch",
    )(dest, x1, xs0)


def _expert_kernel(be_ref, na_ref, x_ref, w1_ref, b1_ref, w2_ref, b2_ref, o_ref, *, d_expert):
    i = pl.program_id(0)

    @pl.when(i < na_ref[0])
    def _():
        hid = jnp.dot(x_ref[...].astype(BF16), w1_ref[0], preferred_element_type=F32) + b1_ref[0]
        glu = jnp.minimum(hid[:, :d_expert], SWIGLU_LIMIT)
        lin = jnp.clip(hid[:, d_expert:], -SWIGLU_LIMIT, SWIGLU_LIMIT)
        act = glu * _sigmoid(SWIGLU_ALPHA * glu) * (lin + 1.0)
        o_ref[...] = jnp.dot(act.astype(BF16), w2_ref[0], preferred_element_type=F32) + b2_ref[0]

    @pl.when(i >= na_ref[0])
    def _():
        o_ref[...] = jnp.zeros_like(o_ref)


def _experts(block_expert, n_active, xs, w1, b1, w2, b2, mb):
    n_slots, d = xs.shape
    n_exp, _, f2 = w1.shape
    f = f2 // 2
    nb = n_slots // mb
    return pl.pallas_call(
        functools.partial(_expert_kernel, d_expert=f),
        out_shape=jax.ShapeDtypeStruct((n_slots, d), F32),
        grid_spec=pltpu.PrefetchScalarGridSpec(
            num_scalar_prefetch=2,
            grid=(nb,),
            in_specs=[pl.BlockSpec((mb, d), lambda i, be, na: (i, 0)),
                      pl.BlockSpec((1, d, f2), lambda i, be, na: (be[i], 0, 0)),
                      pl.BlockSpec((1, 1, f2), lambda i, be, na: (be[i], 0, 0)),
                      pl.BlockSpec((1, f, d), lambda i, be, na: (be[i], 0, 0)),
                      pl.BlockSpec((1, 1, d), lambda i, be, na: (be[i], 0, 0))],
            out_specs=pl.BlockSpec((mb, d), lambda i, be, na: (i, 0))),
        compiler_params=_cparams("arbitrary"),
        name="moe_experts",
    )(block_expert, n_active, xs, w1, b1.reshape(n_exp, 1, f2), w2, b2.reshape(n_exp, 1, d))


def _combine_kernel(dest_ref, x1_ref, gate_ref, g_ref, b_ref, ys_ref, o_ref, buf_ref, sem, *, tm, alpha):
    i = pl.program_id(0)
    base = i * tm

    def row_copy(r, kk):
        slot = dest_ref[(base + r) * TOP_K + kk]
        return pltpu.make_async_copy(ys_ref.at[pl.ds(slot, 1)], buf_ref.at[kk, pl.ds(r, 1)], sem)

    def issue(r, c):
        for kk in range(TOP_K):
            row_copy(r, kk).start()
        return c

    lax.fori_loop(0, tm, issue, 0)

    def drain(r, c):
        for kk in range(TOP_K):
            row_copy(r, kk).wait()
        return c

    lax.fori_loop(0, tm, drain, 0)

    gate = gate_ref[...]
    acc = alpha * x1_ref[...]
    for kk in range(TOP_K):
        acc = acc + gate[:, kk:kk + 1] * buf_ref[kk]
    o_ref[...] = _layer_norm(acc, g_ref[...], b_ref[...])


def _combine(dest, x1, gate, ln_g, ln_b, ys, alpha, tm):
    t_all, d = x1.shape
    row = lambda i, dst: (i, 0)
    const = lambda i, dst: (0, 0)
    return pl.pallas_call(
        functools.partial(_combine_kernel, tm=tm, alpha=alpha),
        out_shape=jax.ShapeDtypeStruct((t_all, d), F32),
        grid_spec=pltpu.PrefetchScalarGridSpec(
            num_scalar_prefetch=1,
            grid=(t_all // tm,),
            in_specs=[pl.BlockSpec((tm, d), row),
                      pl.BlockSpec((tm, LANES), row),
                      pl.BlockSpec((1, d), const),
                      pl.BlockSpec((1, d), const),
                      pl.BlockSpec(memory_space=pl.ANY)],
            out_specs=pl.BlockSpec((tm, d), row),
            scratch_shapes=[pltpu.VMEM((TOP_K, tm, d), F32), pltpu.SemaphoreType.DMA]),
        compiler_params=_cparams("arbitrary"),
        name="moe_combine",
    )(dest, x1, gate, ln_g.reshape(1, d), ln_b.reshape(1, d), ys)


def _lanes(values, offset):
    return jnp.zeros((1, LANES), F32).at[0, offset:offset + values.shape[0]].set(values.astype(F32))


def _split_in_proj(w_in_layer):
    offs = np.concatenate([[0], np.cumsum(IN_PROJ_SIZES)])
    seg = lambda n: w_in_layer[:, offs[n]:offs[n + 1]]
    qk_scale = DIFF_QK_DIM ** -0.5
    fox_scale = FOX_HEAD_DIM ** -0.5
    wa = jnp.concatenate([seg(0) * qk_scale, seg(1), seg(2), seg(3) * fox_scale, seg(4), seg(5)], axis=1)
    wg = jnp.concatenate([seg(7), seg(10)], axis=1)
    d = w_in_layer.shape[0]
    ws = jnp.zeros((d, LANES), F32)
    ws = ws.at[:, GATE_FF:GATE_FF + FOX_HEADS].set(seg(6))
    ws = ws.at[:, GATE_BETA:GATE_BETA + GDN_HEADS].set(seg(8))
    ws = ws.at[:, GATE_DECAY:GATE_DECAY + GDN_HEADS].set(seg(9))
    return wa.astype(BF16), wg.astype(BF16), ws


def _diff_lambda_init(layer):
    return 0.8 - 0.6 * math.exp(-0.3 * layer)


def _pick_block(seq_len, candidates):
    for c in candidates:
        if seq_len % c == 0:
            return c
    raise ValueError(f"unsupported sequence length {seq_len}")


MOE_ROWS = 256


def _layer(stream, layer, seq_len, n_lead, alpha, p):
    t_all, d = stream.shape
    attn_tb = _pick_block(seq_len, (640, 512, 256, 128))
    wa, wg, ws = _split_in_proj(p["w_in"])
    attn, gdn_slab, gates = _in_proj(stream, wa, wg, ws, seq_len, n_lead, attn_tb)

    lv = p["diff_lambda"].astype(F32)
    lambda_init = _diff_lambda_init(layer)
    lam = (jnp.exp(jnp.sum(lv[0] * lv[1])) - jnp.exp(jnp.sum(lv[2] * lv[3])) + lambda_init).reshape(1)
    y_diff = _diff_attention(attn, lam, p["diff_subln_g"], seq_len, n_lead, lambda_init, attn_tb)

    cum_col, cum_row = _forget_cumsum(gates, _lanes(p["fox_forget_b"], GATE_FF), seq_len)
    y_fox = _fox_attention(attn, cum_col, cum_row, seq_len, n_lead, attn_tb)

    conv_w = jnp.zeros((SUBLANES, GDN_CONV_CH), F32).at[:CONV_K].set(p["gdn_conv_w"])
    y_gdn = _gated_deltanet(gdn_slab, gates, conv_w, _lanes(p["gdn_a_log"], GATE_DECAY),
                            _lanes(p["gdn_dt_bias"], GATE_DECAY), p["gdn_norm_g"], seq_len, attn_tb)

    rw = jnp.zeros((d, LANES), F32).at[:, :N_EXPERTS].set(p["router_w"])
    rb = _lanes(p["router_b"], 0)
    x1, sel, gate, counts = _mix_router(stream, y_diff, y_fox, y_gdn, p["w_out"].astype(BF16),
                                        p["ln1_g"], p["ln1_b"], rw, rb, alpha, 256)

    mb = MOE_ROWS
    n_assign = t_all * TOP_K
    n_blocks = -(-(n_assign + N_EXPERTS * (mb - 1)) // mb)
    n_slots = n_blocks * mb
    cnt = counts[0, :N_EXPERTS].astype(I32)
    padded = (cnt + mb - 1) // mb * mb
    pad_end = jnp.cumsum(padded)
    pad_start = pad_end - padded
    dest = (pad_start[sel[:, :TOP_K]] + sel[:, TOP_K:2 * TOP_K]).reshape(-1).astype(I32)
    block_expert = jnp.minimum(jnp.searchsorted(pad_end, jnp.arange(n_blocks, dtype=I32) * mb, side="right"),
                               N_EXPERTS - 1).astype(I32)
    n_active = (pad_end[-1] // mb).astype(I32).reshape(1)

    xs = _dispatch(dest, x1, n_slots)
    ys = _experts(block_expert, n_active, xs, p["expert_w1"].astype(BF16), p["expert_b1"],
                  p["expert_w2"].astype(BF16), p["expert_b2"], mb)
    return _combine(dest, x1, gate, p["ln2_g"], p["ln2_b"], ys, alpha, 128)


def kernel(x, meta_tokens, ln_in_g, ln_in_b, w_in, diff_lambda, diff_subln_g, fox_forget_b, gdn_conv_w,
           gdn_a_log, gdn_dt_bias, gdn_norm_g, w_out, ln1_g, ln1_b, router_w, router_b, expert_w1,
           expert_b1, expert_w2, expert_b2, ln2_g, ln2_b):
    bsz, s, d = x.shape
    depth = w_in.shape[0]
    n_lead = (-(s + N_META)) % Q_BLOCK
    if n_lead + N_META != Q_BLOCK:
        raise ValueError("sequence length must be a multiple of the query block")
    seq_len = n_lead + N_META + s
    alpha = (2 * depth) ** 0.25
    stream = _ln_in(x, meta_tokens.astype(x.dtype), ln_in_g, ln_in_b, n_lead).reshape(bsz * seq_len, d)
    per_layer = dict(w_in=w_in, diff_lambda=diff_lambda, diff_subln_g=diff_subln_g, fox_forget_b=fox_forget_b,
                     gdn_conv_w=gdn_conv_w, gdn_a_log=gdn_a_log, gdn_dt_bias=gdn_dt_bias,
                     gdn_norm_g=gdn_norm_g, w_out=w_out, ln1_g=ln1_g, ln1_b=ln1_b, router_w=router_w,
                     router_b=router_b, expert_w1=expert_w1, expert_b1=expert_b1, expert_w2=expert_w2,
                     expert_b2=expert_b2, ln2_g=ln2_g, ln2_b=ln2_b)
    for layer in range(depth):
        stream = _layer(stream, layer, seq_len, n_lead, alpha, {k: v[layer] for k, v in per_layer.items()})
    return stream.reshape(bsz, seq_len, d)[:, n_lead + N_META:]
```

```python
import functools
import math

import numpy as np
import jax
import jax.numpy as jnp
from jax import lax
from jax.experimental import pallas as pl
from jax.experimental.pallas import tpu as pltpu

F32 = jnp.float32
BF16 = jnp.bfloat16
I32 = jnp.int32

CHUNK = 64
CHUNK_SHIFT = 6
Q_BLOCK = 128
N_META = 16
DIFF_HEADS = 4
DIFF_QK_DIM = 64
DIFF_V_DIM = 128
FOX_HEADS = 4
FOX_HEAD_DIM = 64
GDN_HEADS = 4
GDN_HEAD_DIM = 64
CONV_K = 4
N_EXPERTS = 32
TOP_K = 4
SWIGLU_LIMIT = 7.0
SWIGLU_ALPHA = 1.702
MASK_VALUE = -1e30

DIFF_QK_WIDTH = DIFF_HEADS * 2 * DIFF_QK_DIM
DIFF_V_WIDTH = DIFF_HEADS * DIFF_V_DIM
FOX_WIDTH = FOX_HEADS * FOX_HEAD_DIM
GDN_WIDTH = GDN_HEADS * GDN_HEAD_DIM
GDN_CONV_CH = 3 * GDN_WIDTH
IN_PROJ_SIZES = (DIFF_QK_WIDTH, DIFF_QK_WIDTH, DIFF_V_WIDTH, FOX_WIDTH, FOX_WIDTH, FOX_WIDTH, FOX_HEADS,
                 GDN_CONV_CH, GDN_HEADS, GDN_HEADS, GDN_WIDTH)

LANES = 128
SUBLANES = 8
VMEM_LIMIT = 56 * 1024 * 1024

GATE_FF = 0
GATE_BETA = 4
GATE_DECAY = 8

ATTN_WIDTH = 2 * DIFF_QK_WIDTH + DIFF_V_WIDTH + 3 * FOX_WIDTH
GDN_SLAB = GDN_CONV_CH + GDN_WIDTH


def _cparams(*sem):
    return pltpu.CompilerParams(dimension_semantics=tuple(sem), vmem_limit_bytes=VMEM_LIMIT)


def _dot(a, b):
    return jnp.dot(a.astype(BF16), b.astype(BF16), preferred_element_type=F32)


def _dot_nt(a, b):
    return lax.dot_general(a.astype(BF16), b.astype(BF16), (((1,), (1,)), ((), ())),
                           preferred_element_type=F32)


def _split2(x):
    hi = x.astype(BF16)
    lo = (x - hi.astype(F32)).astype(BF16)
    return hi, lo


def _split3(x):
    hi = x.astype(BF16)
    r = x - hi.astype(F32)
    mid = r.astype(BF16)
    lo = (r - mid.astype(F32)).astype(BF16)
    return hi, mid, lo


def _dot3(a, b):
    ah, al = _split2(a)
    bh, bl = _split2(b)
    return (jnp.dot(ah, bh, preferred_element_type=F32) + jnp.dot(ah, bl, preferred_element_type=F32)
            + jnp.dot(al, bh, preferred_element_type=F32))


def _dot_exact_lhs(a01, b):
    a = a01.astype(BF16)
    hi, mid, lo = _split3(b)
    return (jnp.dot(a, hi, preferred_element_type=F32) + jnp.dot(a, mid, preferred_element_type=F32)
            + jnp.dot(a, lo, preferred_element_type=F32))


_BDIMS = (((2,), (1,)), ((0,), (0,)))
_BDIMS_NT = (((2,), (2,)), ((0,), (0,)))


def _bdot(a, b):
    return lax.dot_general(a.astype(BF16), b.astype(BF16), _BDIMS, preferred_element_type=F32)


def _bdot_nt(a, b):
    return lax.dot_general(a.astype(BF16), b.astype(BF16), _BDIMS_NT, preferred_element_type=F32)


def _bdot3(a, b):
    ah, al = _split2(a)
    bh, bl = _split2(b)
    return (lax.dot_general(ah, bh, _BDIMS, preferred_element_type=F32)
            + lax.dot_general(ah, bl, _BDIMS, preferred_element_type=F32)
            + lax.dot_general(al, bh, _BDIMS, preferred_element_type=F32))


def _bdot_exact_lhs(a01, b):
    a = a01.astype(BF16)
    hi, mid, lo = _split3(b)
    return (lax.dot_general(a, hi, _BDIMS, preferred_element_type=F32)
            + lax.dot_general(a, mid, _BDIMS, preferred_element_type=F32)
            + lax.dot_general(a, lo, _BDIMS, preferred_element_type=F32))


def _layer_norm(x, g, b, eps=1e-5):
    mu = jnp.mean(x, axis=-1, keepdims=True)
    xc = x - mu
    var = jnp.mean(xc * xc, axis=-1, keepdims=True)
    return xc * lax.rsqrt(var + eps) * g + b


def _sigmoid(x):
    return 1.0 / (1.0 + jnp.exp(-x))


def _softplus(x):
    return jnp.maximum(x, 0.0) + jnp.log1p(jnp.exp(-jnp.abs(x)))


def _log_sigmoid(x):
    return -_softplus(-x)


def _ln_in_kernel(x_ref, meta_ref, g_ref, b_ref, o_ref, *, n_lead):
    i = pl.program_id(1)
    g = g_ref[...]
    b = b_ref[...]

    @pl.when(i == 0)
    def _():
        o_ref[0, :n_lead, :] = jnp.broadcast_to(b, (n_lead, b.shape[-1]))
        o_ref[0, n_lead:, :] = _layer_norm(meta_ref[...], g, b)

    @pl.when(i > 0)
    def _():
        o_ref[0] = _layer_norm(x_ref[0], g, b)


def _ln_in(x, meta, g, b, n_lead):
    bsz, s, d = x.shape
    l = n_lead + N_META + s
    blk = n_lead + N_META
    return pl.pallas_call(
        functools.partial(_ln_in_kernel, n_lead=n_lead),
        out_shape=jax.ShapeDtypeStruct((bsz, l, d), F32),
        grid=(bsz, l // blk),
        in_specs=[pl.BlockSpec((1, blk, d), lambda bb, i: (bb, jnp.maximum(i - 1, 0), 0)),
                  pl.BlockSpec((N_META, d), lambda bb, i: (0, 0)),
                  pl.BlockSpec((1, d), lambda bb, i: (0, 0)),
                  pl.BlockSpec((1, d), lambda bb, i: (0, 0))],
        out_specs=pl.BlockSpec((1, blk, d), lambda bb, i: (bb, i, 0)),
        compiler_params=_cparams("arbitrary", "arbitrary"),
        name="ln_in",
    )(x, meta, g.reshape(1, d), b.reshape(1, d))


PROJ_COL_CHUNK = 768


def _proj_kernel(x_ref, wa_ref, wg_ref, ws_ref, oa_ref, og_ref, os_ref, *, tm, n_lead):
    i = pl.program_id(1)
    pos = i * tm + lax.broadcasted_iota(I32, (tm, 1), 0)
    x = jnp.where(pos >= n_lead, x_ref[...], 0.0)
    xb = x.astype(BF16)
    for c in range(0, ATTN_WIDTH, PROJ_COL_CHUNK):
        oa_ref[:, c:c + PROJ_COL_CHUNK] = jnp.dot(
            xb, wa_ref[:, c:c + PROJ_COL_CHUNK], preferred_element_type=F32).astype(BF16)
    og_ref[...] = jnp.dot(xb, wg_ref[...], preferred_element_type=F32)
    os_ref[...] = _dot3(x, ws_ref[...])


def _in_proj(stream, wa, wg, ws, seq_len, n_lead, tm):
    t_all, d = stream.shape
    nb = seq_len // tm
    row = lambda bb, i: (bb * nb + i, 0)
    const = lambda bb, i: (0, 0)
    return pl.pallas_call(
        functools.partial(_proj_kernel, tm=tm, n_lead=n_lead),
        out_shape=(jax.ShapeDtypeStruct((t_all, ATTN_WIDTH), BF16),
                   jax.ShapeDtypeStruct((t_all, GDN_SLAB), F32),
                   jax.ShapeDtypeStruct((t_all, LANES), F32)),
        grid=(t_all // seq_len, nb),
        in_specs=[pl.BlockSpec((tm, d), row),
                  pl.BlockSpec((d, ATTN_WIDTH), const),
                  pl.BlockSpec((d, GDN_SLAB), const),
                  pl.BlockSpec((d, LANES), const)],
        out_specs=(pl.BlockSpec((tm, ATTN_WIDTH), row),
                   pl.BlockSpec((tm, GDN_SLAB), row),
                   pl.BlockSpec((tm, LANES), row)),
        compiler_params=_cparams("arbitrary", "arbitrary"),
        name="in_proj",
    )(stream, wa, wg, ws)


def _online_softmax_step(s, v, m_ref, l_ref, a_ref):
    m_old = m_ref[...]
    m_new = jnp.maximum(m_old, jnp.max(s, axis=-1, keepdims=True))
    alpha = jnp.exp(m_old - m_new)
    p = jnp.exp(s - m_new)
    l_ref[...] = alpha * l_ref[...] + jnp.sum(p, axis=-1, keepdims=True)
    a_ref[...] = alpha * a_ref[...] + jnp.dot(p.astype(BF16), v, preferred_element_type=F32)
    m_ref[...] = m_new


def _init_softmax_state(m_ref, l_ref, a_ref):
    m_ref[...] = jnp.full(m_ref.shape, MASK_VALUE, F32)
    l_ref[...] = jnp.zeros(l_ref.shape, F32)
    a_ref[...] = jnp.zeros(a_ref.shape, F32)


def _diff_attn_kernel(lam_ref, q_ref, k_ref, v_ref, g_ref, o_ref,
                      m0, l0, a0, m1, l1, a1, *, tb, n_lead, out_scale):
    i = pl.program_id(2)
    lane = lax.broadcasted_iota(I32, (1, LANES), 1)
    q = q_ref[...]
    zero = jnp.zeros_like(q)
    q0 = jnp.where(lane < DIFF_QK_DIM, q, zero)
    q1 = jnp.where(lane >= DIFF_QK_DIM, q, zero)
    _init_softmax_state(m0, l0, a0)
    _init_softmax_state(m1, l1, a1)

    def step(j, masked):
        start = pl.multiple_of(j * tb, tb)
        k = k_ref[pl.ds(start, tb), :]
        v = v_ref[pl.ds(start, tb), :]
        s0 = _dot_nt(q0, k)
        s1 = _dot_nt(q1, k)
        if masked:
            qpos = i * tb + lax.broadcasted_iota(I32, (tb, 1), 0)
            kpos = j * tb + lax.broadcasted_iota(I32, (1, tb), 1)
            vis = jnp.logical_and(kpos >> CHUNK_SHIFT <= qpos >> CHUNK_SHIFT, kpos >= n_lead)
            s0 = jnp.where(vis, s0, MASK_VALUE)
            s1 = jnp.where(vis, s1, MASK_VALUE)
        _online_softmax_step(s0, v, m0, l0, a0)
        _online_softmax_step(s1, v, m1, l1, a1)

    step(0, True)

    def body(j, c):
        step(j, False)
        return c

    lax.fori_loop(1, i, body, 0)

    @pl.when(i > 0)
    def _():
        step(i, True)

    lam = lam_ref[0]
    o = a0[...] / l0[...] - lam * (a1[...] / l1[...])
    o = o * lax.rsqrt(jnp.mean(o * o, axis=-1, keepdims=True) + 1e-5) * g_ref[...] * out_scale
    o_ref[...] = o.astype(o_ref.dtype)


def _diff_attention(attn, lam, subln_g, seq_len, n_lead, lambda_init, tb):
    t_all = attn.shape[0]
    bsz = t_all // seq_len
    nq = seq_len // tb
    kcol = DIFF_QK_WIDTH // LANES
    vcol = 2 * DIFF_QK_WIDTH // LANES
    scratch = []
    for _ in range(2):
        scratch += [pltpu.VMEM((tb, 1), F32), pltpu.VMEM((tb, 1), F32), pltpu.VMEM((tb, DIFF_V_DIM), F32)]
    return pl.pallas_call(
        functools.partial(_diff_attn_kernel, tb=tb, n_lead=n_lead, out_scale=1.0 - lambda_init),
        out_shape=jax.ShapeDtypeStruct((t_all, DIFF_V_WIDTH), BF16),
        grid_spec=pltpu.PrefetchScalarGridSpec(
            num_scalar_prefetch=1,
            grid=(bsz, DIFF_HEADS, nq),
            in_specs=[pl.BlockSpec((tb, LANES), lambda b, h, i, lam: (b * nq + i, h)),
                      pl.BlockSpec((seq_len, LANES), lambda b, h, i, lam: (b, kcol + h)),
                      pl.BlockSpec((seq_len, LANES), lambda b, h, i, lam: (b, vcol + h)),
                      pl.BlockSpec((1, DIFF_V_DIM), lambda b, h, i, lam: (0, 0))],
            out_specs=pl.BlockSpec((tb, DIFF_V_DIM), lambda b, h, i, lam: (b * nq + i, h)),
            scratch_shapes=scratch),
        compiler_params=_cparams("arbitrary", "arbitrary", "arbitrary"),
        name="diff_attention",
    )(lam, attn, attn, attn, subln_g.reshape(1, DIFF_V_DIM))


def _fox_attn_kernel(q_ref, k_ref, v_ref, cc_ref, cr_ref, o_ref, *scratch, tb, n_lead):
    i = pl.program_id(1)
    lane = lax.broadcasted_iota(I32, (1, LANES), 1)
    pairs = FOX_HEADS // 2
    qs = []
    for p in range(pairs):
        q = q_ref[:, p * LANES:(p + 1) * LANES]
        zero = jnp.zeros_like(q)
        qs.append((jnp.where(lane < FOX_HEAD_DIM, q, zero), jnp.where(lane >= FOX_HEAD_DIM, q, zero)))
    for hh in range(FOX_HEADS):
        _init_softmax_state(*scratch[3 * hh:3 * hh + 3])
    cc = cc_ref[...]

    def step(j, masked):
        start = pl.multiple_of(j * tb, tb)
        if masked:
            qpos = i * tb + lax.broadcasted_iota(I32, (tb, 1), 0)
            kpos = j * tb + lax.broadcasted_iota(I32, (1, tb), 1)
            vis = jnp.logical_and(kpos <= qpos, kpos >= n_lead)
        for p in range(pairs):
            k = k_ref[pl.ds(start, tb), p * LANES:(p + 1) * LANES]
            v = v_ref[pl.ds(start, tb), p * LANES:(p + 1) * LANES]
            for r in range(2):
                hh = 2 * p + r
                ci = cc[:, GATE_FF + hh:GATE_FF + hh + 1]
                ck = cr_ref[0, hh:hh + 1, pl.ds(start, tb)]
                s = _dot_nt(qs[p][r], k) + (ci - ck)
                if masked:
                    s = jnp.where(vis, s, MASK_VALUE)
                _online_softmax_step(s, v, *scratch[3 * hh:3 * hh + 3])

    step(0, True)

    def body(j, c):
        step(j, False)
        return c

    lax.fori_loop(1, i, body, 0)

    @pl.when(i > 0)
    def _():
        step(i, True)

    for p in range(pairs):
        _, l_lo, a_lo = scratch[6 * p:6 * p + 3]
        _, l_hi, a_hi = scratch[6 * p + 3:6 * p + 6]
        o = jnp.where(lane < FOX_HEAD_DIM, a_lo[...] / l_lo[...], a_hi[...] / l_hi[...])
        o_ref[:, p * LANES:(p + 1) * LANES] = o.astype(o_ref.dtype)


def _fox_attention(attn, cum_col, cum_row, seq_len, n_lead, tb):
    t_all = attn.shape[0]
    bsz = t_all // seq_len
    nq = seq_len // tb
    base = (2 * DIFF_QK_WIDTH + DIFF_V_WIDTH) // FOX_WIDTH
    scratch = []
    for _ in range(FOX_HEADS):
        scratch += [pltpu.VMEM((tb, 1), F32), pltpu.VMEM((tb, 1), F32), pltpu.VMEM((tb, LANES), F32)]
    return pl.pallas_call(
        functools.partial(_fox_attn_kernel, tb=tb, n_lead=n_lead),
        out_shape=jax.ShapeDtypeStruct((t_all, FOX_WIDTH), BF16),
        grid=(bsz, nq),
        in_specs=[pl.BlockSpec((tb, FOX_WIDTH), lambda b, i: (b * nq + i, base)),
                  pl.BlockSpec((seq_len, FOX_WIDTH), lambda b, i: (b, base + 1)),
                  pl.BlockSpec((seq_len, FOX_WIDTH), lambda b, i: (b, base + 2)),
                  pl.BlockSpec((tb, LANES), lambda b, i: (b * nq + i, 0)),
                  pl.BlockSpec((1, SUBLANES, seq_len), lambda b, i: (b, 0, 0))],
        out_specs=pl.BlockSpec((tb, FOX_WIDTH), lambda b, i: (b * nq + i, 0)),
        scratch_shapes=scratch,
        compiler_params=_cparams("arbitrary", "arbitrary"),
        name="fox_attention",
    )(attn, attn, attn, cum_col, cum_row)


def _forget_cumsum_kernel(x_ref, b_ref, col_ref, row_ref, carry_ref, *, tb):
    i = pl.program_id(1)

    @pl.when(i == 0)
    def _():
        carry_ref[...] = jnp.zeros_like(carry_ref)

    log_f = _log_sigmoid(x_ref[...] + b_ref[...])
    r = lax.broadcasted_iota(I32, (tb, tb), 0)
    c = lax.broadcasted_iota(I32, (tb, tb), 1)
    tri = jnp.where(c <= r, 1.0, 0.0).astype(BF16)
    cum = _dot_exact_lhs(tri, log_f) + carry_ref[...]
    col_ref[...] = cum
    row_ref[0] = cum.T[:SUBLANES, :]
    carry_ref[...] = cum[tb - 1:tb, :]


def _forget_cumsum(gates, forget_bias_lanes, seq_len):
    t_all = gates.shape[0]
    bsz = t_all // seq_len
    tb = LANES
    nb = seq_len // tb
    return pl.pallas_call(
        functools.partial(_forget_cumsum_kernel, tb=tb),
        out_shape=(jax.ShapeDtypeStruct((t_all, LANES), F32),
                   jax.ShapeDtypeStruct((bsz, SUBLANES, seq_len), F32)),
        grid=(bsz, nb),
        in_specs=[pl.BlockSpec((tb, LANES), lambda b, i: (b * nb + i, 0)),
                  pl.BlockSpec((1, LANES), lambda b, i: (0, 0))],
        out_specs=(pl.BlockSpec((tb, LANES), lambda b, i: (b * nb + i, 0)),
                   pl.BlockSpec((1, SUBLANES, tb), lambda b, i: (b, 0, i))),
        scratch_shapes=[pltpu.VMEM((1, LANES), F32)],
        compiler_params=_cparams("arbitrary", "arbitrary"),
        name="forget_cumsum",
    )(gates, forget_bias_lanes)


def _gdn_kernel(x_ref, gate_ref, cw_ref, alog_ref, dtb_ref, ng_ref, o_ref,
                halo_ref, xs_ref, state_ref, *, rows):
    cg = pl.program_id(1)
    g_chunks = rows // CHUNK
    hd = GDN_HEAD_DIM

    @pl.when(cg == 0)
    def _():
        halo_ref[...] = jnp.zeros_like(halo_ref)
        state_ref[...] = jnp.zeros_like(state_ref)

    xin = x_ref[:, :GDN_CONV_CH]
    xs_ref[0:SUBLANES, :] = halo_ref[...]
    xs_ref[SUBLANES:SUBLANES + rows, :] = xin
    halo_ref[...] = xin[rows - SUBLANES:rows, :]
    y = jnp.zeros((rows, GDN_CONV_CH), F32)
    for tap in range(CONV_K):
        back = CONV_K - 1 - tap
        y = y + cw_ref[tap:tap + 1, :] * xs_ref[SUBLANES - back:SUBLANES - back + rows, :]
    y = y * _sigmoid(y)

    gates = gate_ref[...]
    beta_all = _sigmoid(gates)
    g_all = -jnp.exp(alog_ref[...]) * _softplus(gates + dtb_ref[...])
    rowi = lax.broadcasted_iota(I32, (rows, 1), 0) & (CHUNK - 1)
    gc_all = g_all
    shift = 1
    while shift < CHUNK:
        gc_all = gc_all + jnp.where(rowi >= shift, pltpu.roll(gc_all, shift, 0), 0.0)
        shift *= 2

    ri = lax.broadcasted_iota(I32, (CHUNK, CHUNK), 0)
    ci = lax.broadcasted_iota(I32, (CHUNK, CHUNK), 1)
    eye = jnp.where(ri == ci, 1.0, 0.0).astype(F32)
    causal = (ci <= ri)[None]
    strict = (ci < ri)[None]
    ones_b = jnp.ones((g_chunks, CHUNK, CHUNK), BF16)
    eye_b = jnp.broadcast_to(eye[None], (g_chunks, CHUNK, CHUNK))

    for h in range(GDN_HEADS):
        def head3(base):
            return y[:, base + h * hd:base + (h + 1) * hd]
        q = head3(0)
        k = head3(GDN_WIDTH)
        v = head3(2 * GDN_WIDTH)
        q = q * lax.rsqrt(jnp.sum(q * q, axis=-1, keepdims=True) + 1e-6) * (hd ** -0.5)
        k = k * lax.rsqrt(jnp.sum(k * k, axis=-1, keepdims=True) + 1e-6)
        beta = beta_all[:, GATE_BETA + h:GATE_BETA + h + 1]
        gcol = jnp.broadcast_to(gc_all[:, GATE_DECAY + h:GATE_DECAY + h + 1], (rows, hd))
        egc = jnp.exp(gcol)

        q3 = q.reshape(g_chunks, CHUNK, hd)
        k3 = k.reshape(g_chunks, CHUNK, hd)
        gcol3 = gcol.reshape(g_chunks, CHUNK, CHUNK)
        beta3 = jnp.broadcast_to(beta, (rows, hd)).reshape(g_chunks, CHUNK, hd)
        grow3 = _bdot_exact_lhs(ones_b, eye_b * gcol3)
        decay = jnp.exp(jnp.where(causal, gcol3 - grow3, MASK_VALUE))
        a = jnp.where(strict, _bdot_nt(k3, k3) * beta3 * decay, 0.0)

        x = -a
        tinv = eye_b + x
        span = 2
        while span < CHUNK:
            x = _bdot3(x, x)
            tinv = tinv + _bdot3(tinv, x)
            span *= 2

        vb = (v * beta).reshape(g_chunks, CHUNK, hd)
        kb = (k * beta * egc).reshape(g_chunks, CHUNK, hd)
        u3 = _bdot3(tinv, vb)
        w3 = _bdot3(tinv, kb)
        qk3 = _bdot_nt(q3, k3) * decay
        qd3 = (q * egc).reshape(g_chunks, CHUNK, hd)
        glast3 = gcol3[:, CHUNK - 1:CHUNK, :]
        kd3 = k3 * jnp.exp(glast3 - gcol3)
        kdt3 = _bdot_nt(eye_b, kd3)
        cd3 = jnp.exp(glast3)

        state = state_ref[h]
        outs = []
        for c in range(g_chunks):
            v_new = u3[c] - _dot(w3[c], state)
            outs.append(_dot(qd3[c], state) + _dot(qk3[c], v_new))
            state = state * cd3[c] + _dot(kdt3[c], v_new)
        state_ref[h] = state
        o = jnp.concatenate(outs, axis=0)
        o = o * lax.rsqrt(jnp.mean(o * o, axis=-1, keepdims=True) + 1e-6) * ng_ref[...]
        z = x_ref[:, GDN_CONV_CH + h * hd:GDN_CONV_CH + (h + 1) * hd]
        o_ref[:, h * hd:(h + 1) * hd] = (o * (z * _sigmoid(z))).astype(o_ref.dtype)


def _gated_deltanet(gdn_slab, gates, conv_w, a_log_lanes, dt_bias_lanes, norm_g, seq_len, rows):
    t_all = gdn_slab.shape[0]
    bsz = t_all // seq_len
    nb = seq_len // rows
    row = lambda b, i: (b * nb + i, 0)
    const = lambda b, i: (0, 0)
    return pl.pallas_call(
        functools.partial(_gdn_kernel, rows=rows),
        out_shape=jax.ShapeDtypeStruct((t_all, GDN_WIDTH), BF16),
        grid=(bsz, nb),
        in_specs=[pl.BlockSpec((rows, GDN_SLAB), row),
                  pl.BlockSpec((rows, LANES), row),
                  pl.BlockSpec((SUBLANES, GDN_CONV_CH), const),
                  pl.BlockSpec((1, LANES), const),
                  pl.BlockSpec((1, LANES), const),
                  pl.BlockSpec((1, GDN_HEAD_DIM), const)],
        out_specs=pl.BlockSpec((rows, GDN_WIDTH), row),
        scratch_shapes=[pltpu.VMEM((SUBLANES, GDN_CONV_CH), F32),
                        pltpu.VMEM((rows + SUBLANES, GDN_CONV_CH), F32),
                        pltpu.VMEM((GDN_HEADS, GDN_HEAD_DIM, GDN_HEAD_DIM), F32)],
        compiler_params=_cparams("arbitrary", "arbitrary"),
        name="gated_deltanet",
    )(gdn_slab, gates, conv_w, a_log_lanes, dt_bias_lanes, norm_g.reshape(1, GDN_HEAD_DIM))


def _mix_router_kernel(x_ref, yd_ref, yf_ref, yg_ref, wd_ref, wf_ref, wg_ref, g_ref, b_ref,
                       rw_ref, rb_ref, x1_ref, sel_ref, gate_ref, cnt_ref, carry_ref, *, tm, alpha):
    i = pl.program_id(0)

    @pl.when(i == 0)
    def _():
        carry_ref[...] = jnp.zeros_like(carry_ref)

    mix = (jnp.dot(yd_ref[...], wd_ref[...], preferred_element_type=F32)
           + jnp.dot(yf_ref[...], wf_ref[...], preferred_element_type=F32)
           + jnp.dot(yg_ref[...], wg_ref[...], preferred_element_type=F32))
    x1 = _layer_norm(alpha * x_ref[...] + mix, g_ref[...], b_ref[...])
    x1_ref[...] = x1

    lane = lax.broadcasted_iota(I32, (tm, LANES), 1)
    lane_f = lane.astype(F32)
    logits = _dot3(x1, rw_ref[...]) + rb_ref[...]
    vals = jnp.where(lane < N_EXPERTS, logits, -jnp.inf)
    onehot = jnp.zeros((tm, LANES), F32)
    sels, tops = [], []
    for _ in range(TOP_K):
        top = jnp.max(vals, axis=-1, keepdims=True)
        idx = jnp.min(jnp.where(vals == top, lane_f, float(LANES)), axis=-1, keepdims=True)
        hit = lane_f == idx
        onehot = onehot + jnp.where(hit, 1.0, 0.0)
        vals = jnp.where(hit, -jnp.inf, vals)
        sels.append((idx, hit))
        tops.append(top)
    exps = [jnp.exp(t - tops[0]) for t in tops]
    denom = exps[0] + exps[1] + exps[2] + exps[3]

    r = lax.broadcasted_iota(I32, (tm, tm), 0)
    c = lax.broadcasted_iota(I32, (tm, tm), 1)
    tri = jnp.where(c < r, 1.0, 0.0).astype(BF16)
    prefix = jnp.dot(tri, onehot.astype(BF16), preferred_element_type=F32) + carry_ref[...]
    sel_out = jnp.zeros((tm, LANES), I32)
    gate_out = jnp.zeros((tm, LANES), F32)
    for kk in range(TOP_K):
        idx, hit = sels[kk]
        rank = jnp.sum(jnp.where(hit, prefix, 0.0), axis=-1, keepdims=True).astype(I32)
        sel_out = jnp.where(lane == kk, idx.astype(I32), sel_out)
        sel_out = jnp.where(lane == TOP_K + kk, rank, sel_out)
        gate_out = jnp.where(lane == kk, exps[kk] / denom, gate_out)
    sel_ref[...] = sel_out
    gate_ref[...] = gate_out
    carry_ref[...] = carry_ref[...] + jnp.sum(onehot, axis=0, keepdims=True)
    cnt_ref[...] = jnp.broadcast_to(carry_ref[...], cnt_ref.shape)


def _mix_router(stream, yd, yf, yg, w_out_b, ln_g, ln_b, rw, rb, alpha, tm):
    t_all, d = stream.shape
    row = lambda i: (i, 0)
    const = lambda i: (0, 0)
    wd = w_out_b[:DIFF_V_WIDTH]
    wf = w_out_b[DIFF_V_WIDTH:DIFF_V_WIDTH + FOX_WIDTH]
    wg = w_out_b[DIFF_V_WIDTH + FOX_WIDTH:]
    return pl.pallas_call(
        functools.partial(_mix_router_kernel, tm=tm, alpha=alpha),
        out_shape=(jax.ShapeDtypeStruct((t_all, d), F32),
                   jax.ShapeDtypeStruct((t_all, LANES), I32),
                   jax.ShapeDtypeStruct((t_all, LANES), F32),
                   jax.ShapeDtypeStruct((SUBLANES, LANES), F32)),
        grid=(t_all // tm,),
        in_specs=[pl.BlockSpec((tm, d), row),
                  pl.BlockSpec((tm, DIFF_V_WIDTH), row),
                  pl.BlockSpec((tm, FOX_WIDTH), row),
                  pl.BlockSpec((tm, GDN_WIDTH), row),
                  pl.BlockSpec((DIFF_V_WIDTH, d), const),
                  pl.BlockSpec((FOX_WIDTH, d), const),
                  pl.BlockSpec((GDN_WIDTH, d), const),
                  pl.BlockSpec((1, d), const),
                  pl.BlockSpec((1, d), const),
                  pl.BlockSpec((d, LANES), const),
                  pl.BlockSpec((1, LANES), const)],
        out_specs=(pl.BlockSpec((tm, d), row),
                   pl.BlockSpec((tm, LANES), row),
                   pl.BlockSpec((tm, LANES), row),
                   pl.BlockSpec((SUBLANES, LANES), const)),
        scratch_shapes=[pltpu.VMEM((1, LANES), F32)],
        compiler_params=_cparams("arbitrary"),
        name="mix_router",
    )(stream, yd, yf, yg, wd, wf, wg, ln_g.reshape(1, d), ln_b.reshape(1, d), rw, rb)


def _dispatch_kernel(dest_ref, x_ref, xs_in_ref, xs_ref, sem, *, tm):
    del xs_in_ref
    base = pl.program_id(0) * tm

    def row_copy(r, kk):
        slot = dest_ref[(base + r) * TOP_K + kk]
        return pltpu.make_async_copy(x_ref.at[pl.ds(r, 1)], xs_ref.at[pl.ds(slot, 1)], sem)

    def issue(r, c):
        for kk in range(TOP_K):
            row_copy(r, kk).start()
        return c

    lax.fori_loop(0, tm, issue, 0)

    def drain(r, c):
        for kk in range(TOP_K):
            row_copy(r, kk).wait()
        return c

    lax.fori_loop(0, tm, drain, 0)


def _dispatch(dest, x1, n_slots, tm):
    t_all, d = x1.shape
    xs0 = jnp.zeros((n_slots, d), F32)
    return pl.pallas_call(
        functools.partial(_dispatch_kernel, tm=tm),
        out_shape=jax.ShapeDtypeStruct((n_slots, d), F32),
        grid_spec=pltpu.PrefetchScalarGridSpec(
            num_scalar_prefetch=1,
            grid=(t_all // tm,),
            in_specs=[pl.BlockSpec((tm, d), lambda i, dst: (i, 0)), pl.BlockSpec(memory_space=pl.ANY)],
            out_specs=pl.BlockSpec(memory_space=pl.ANY),
            scratch_shapes=[pltpu.SemaphoreType.DMA]),
        input_output_aliases={2: 0},
        compiler_params=_cparams("arbitrary"),
        name="moe_dispatch",
    )(dest, x1, xs0)


def _expert_kernel(be_ref, na_ref, x_ref, w1_ref, b1_ref, w2_ref, b2_ref, o_ref, w1b_ref, w2b_ref, *, d_expert):
    i = pl.program_id(0)
    active = i < na_ref[0]
    new_expert = jnp.logical_or(i == 0, be_ref[i] != be_ref[jnp.maximum(i - 1, 0)])

    @pl.when(jnp.logical_and(active, new_expert))
    def _():
        w1b_ref[...] = w1_ref[0].astype(BF16)
        w2b_ref[...] = w2_ref[0].astype(BF16)

    @pl.when(active)
    def _():
        hid = jnp.dot(x_ref[...].astype(BF16), w1b_ref[...], preferred_element_type=F32) + b1_ref[0]
        glu = jnp.minimum(hid[:, :d_expert], SWIGLU_LIMIT)
        lin = jnp.clip(hid[:, d_expert:], -SWIGLU_LIMIT, SWIGLU_LIMIT)
        act = glu * _sigmoid(SWIGLU_ALPHA * glu) * (lin + 1.0)
        o_ref[...] = jnp.dot(act.astype(BF16), w2b_ref[...], preferred_element_type=F32) + b2_ref[0]

    @pl.when(jnp.logical_not(active))
    def _():
        o_ref[...] = jnp.zeros_like(o_ref)


def _experts(block_expert, n_active, xs, w1_all, b1, w2_all, b2, layer, mb):
    n_slots, d = xs.shape
    depth, n_exp, _, f2 = w1_all.shape
    f = f2 // 2
    nb = n_slots // mb
    w1 = w1_all.reshape(depth * n_exp, d, f2)
    w2 = w2_all.reshape(depth * n_exp, f, d)
    block_expert = block_expert + layer * n_exp
    return pl.pallas_call(
        functools.partial(_expert_kernel, d_expert=f),
        out_shape=jax.ShapeDtypeStruct((n_slots, d), F32),
        grid_spec=pltpu.PrefetchScalarGridSpec(
            num_scalar_prefetch=2,
            grid=(nb,),
            in_specs=[pl.BlockSpec((mb, d), lambda i, be, na: (i, 0)),
                      pl.BlockSpec((1, d, f2), lambda i, be, na: (be[i], 0, 0)),
                      pl.BlockSpec((1, 1, f2), lambda i, be, na: (be[i], 0, 0)),
                      pl.BlockSpec((1, f, d), lambda i, be, na: (be[i], 0, 0)),
                      pl.BlockSpec((1, 1, d), lambda i, be, na: (be[i], 0, 0))],
            out_specs=pl.BlockSpec((mb, d), lambda i, be, na: (i, 0)),
            scratch_shapes=[pltpu.VMEM((d, f2), BF16), pltpu.VMEM((f, d), BF16)]),
        compiler_params=_cparams("arbitrary"),
        name="moe_experts",
    )(block_expert, n_active, xs, w1, b1.reshape(depth * n_exp, 1, f2), w2, b2.reshape(depth * n_exp, 1, d))


def _combine_kernel(dest_ref, x1_ref, gate_ref, g_ref, b_ref, ys_ref, o_ref, buf_ref, sem, *,
                    tm, alpha, blocks_per_seq, skip):
    base = (pl.program_id(0) * blocks_per_seq + skip + pl.program_id(1)) * tm

    def row_copy(r, kk):
        slot = dest_ref[(base + r) * TOP_K + kk]
        return pltpu.make_async_copy(ys_ref.at[pl.ds(slot, 1)], buf_ref.at[kk, pl.ds(r, 1)], sem)

    def issue(r, c):
        for kk in range(TOP_K):
            row_copy(r, kk).start()
        return c

    lax.fori_loop(0, tm, issue, 0)

    def drain(r, c):
        for kk in range(TOP_K):
            row_copy(r, kk).wait()
        return c

    lax.fori_loop(0, tm, drain, 0)

    gate = gate_ref[...]
    acc = alpha * x1_ref[...]
    for kk in range(TOP_K):
        acc = acc + gate[:, kk:kk + 1] * buf_ref[kk]
    o_ref[...] = _layer_norm(acc, g_ref[...], b_ref[...])


def _combine(dest, x1, gate, ln_g, ln_b, ys, alpha, tm, seq_len, skip_rows):
    t_all, d = x1.shape
    bsz = t_all // seq_len
    nb = seq_len // tm
    skip = skip_rows // tm
    row = lambda b, i, dst: (b * nb + skip + i, 0)
    const = lambda b, i, dst: (0, 0)
    return pl.pallas_call(
        functools.partial(_combine_kernel, tm=tm, alpha=alpha, blocks_per_seq=nb, skip=skip),
        out_shape=jax.ShapeDtypeStruct((bsz * (seq_len - skip_rows), d), F32),
        grid_spec=pltpu.PrefetchScalarGridSpec(
            num_scalar_prefetch=1,
            grid=(bsz, nb - skip),
            in_specs=[pl.BlockSpec((tm, d), row),
                      pl.BlockSpec((tm, LANES), row),
                      pl.BlockSpec((1, d), const),
                      pl.BlockSpec((1, d), const),
                      pl.BlockSpec(memory_space=pl.ANY)],
            out_specs=pl.BlockSpec((tm, d), lambda b, i, dst: (b * (nb - skip) + i, 0)),
            scratch_shapes=[pltpu.VMEM((TOP_K, tm, d), F32), pltpu.SemaphoreType.DMA]),
        compiler_params=_cparams("arbitrary", "arbitrary"),
        name="moe_combine",
    )(dest, x1, gate, ln_g.reshape(1, d), ln_b.reshape(1, d), ys)


def _lanes(values, offset):
    return jnp.zeros((1, LANES), F32).at[0, offset:offset + values.shape[0]].set(values.astype(F32))


def _split_in_proj(w_in_layer):
    offs = np.concatenate([[0], np.cumsum(IN_PROJ_SIZES)])
    seg = lambda n: w_in_layer[:, offs[n]:offs[n + 1]]
    qk_scale = DIFF_QK_DIM ** -0.5
    fox_scale = FOX_HEAD_DIM ** -0.5
    wa = jnp.concatenate([seg(0) * qk_scale, seg(1), seg(2), seg(3) * fox_scale, seg(4), seg(5)], axis=1)
    wg = jnp.concatenate([seg(7), seg(10)], axis=1)
    d = w_in_layer.shape[0]
    ws = jnp.zeros((d, LANES), F32)
    ws = ws.at[:, GATE_FF:GATE_FF + FOX_HEADS].set(seg(6))
    ws = ws.at[:, GATE_BETA:GATE_BETA + GDN_HEADS].set(seg(8))
    ws = ws.at[:, GATE_DECAY:GATE_DECAY + GDN_HEADS].set(seg(9))
    return wa.astype(BF16), wg.astype(BF16), ws


def _diff_lambda_init(layer):
    return 0.8 - 0.6 * math.exp(-0.3 * layer)


def _pick_block(seq_len, candidates):
    for c in candidates:
        if seq_len % c == 0:
            return c
    raise ValueError(f"unsupported sequence length {seq_len}")


MOE_ROWS = 256


def _layer(stream, layer, seq_len, n_lead, alpha, p, skip_rows):
    t_all, d = stream.shape
    attn_tb = _pick_block(seq_len, (640, 512, 256, 128))
    wa, wg, ws = _split_in_proj(p["w_in"])
    attn, gdn_slab, gates = _in_proj(stream, wa, wg, ws, seq_len, n_lead, attn_tb)

    lv = p["diff_lambda"].astype(F32)
    lambda_init = _diff_lambda_init(layer)
    lam = (jnp.exp(jnp.sum(lv[0] * lv[1])) - jnp.exp(jnp.sum(lv[2] * lv[3])) + lambda_init).reshape(1)
    y_diff = _diff_attention(attn, lam, p["diff_subln_g"], seq_len, n_lead, lambda_init, attn_tb)

    cum_col, cum_row = _forget_cumsum(gates, _lanes(p["fox_forget_b"], GATE_FF), seq_len)
    y_fox = _fox_attention(attn, cum_col, cum_row, seq_len, n_lead, attn_tb)

    conv_w = jnp.zeros((SUBLANES, GDN_CONV_CH), F32).at[:CONV_K].set(p["gdn_conv_w"])
    y_gdn = _gated_deltanet(gdn_slab, gates, conv_w, _lanes(p["gdn_a_log"], GATE_DECAY),
                            _lanes(p["gdn_dt_bias"], GATE_DECAY), p["gdn_norm_g"], seq_len, attn_tb)

    rw = jnp.zeros((d, LANES), F32).at[:, :N_EXPERTS].set(p["router_w"])
    rb = _lanes(p["router_b"], 0)
    x1, sel, gate, counts = _mix_router(stream, y_diff, y_fox, y_gdn, p["w_out"].astype(BF16),
                                        p["ln1_g"], p["ln1_b"], rw, rb, alpha, 256)

    mb = MOE_ROWS
    n_assign = t_all * TOP_K
    n_blocks = -(-(n_assign + N_EXPERTS * (mb - 1)) // mb)
    n_slots = n_blocks * mb
    cnt = counts[0, :N_EXPERTS].astype(I32)
    padded = (cnt + mb - 1) // mb * mb
    pad_end = jnp.cumsum(padded)
    pad_start = pad_end - padded
    experts = jnp.arange(N_EXPERTS, dtype=I32)
    start_of = jnp.sum(jnp.where(sel[:, :TOP_K, None] == experts, pad_start, 0), axis=-1)
    dest = (start_of + sel[:, TOP_K:2 * TOP_K]).reshape(-1).astype(I32)
    block_first = jnp.arange(n_blocks, dtype=I32) * mb
    block_expert = jnp.minimum(jnp.sum((pad_end[None, :] <= block_first[:, None]).astype(I32), axis=1),
                               N_EXPERTS - 1)
    n_active = (pad_end[-1] // mb).astype(I32).reshape(1)

    xs = _dispatch(dest, x1, n_slots, 256)
    ys = _experts(block_expert, n_active, xs, p["expert_w1"], p["expert_b1"], p["expert_w2"], p["expert_b2"],
                  layer, mb)
    return _combine(dest, x1, gate, p["ln2_g"], p["ln2_b"], ys, alpha, 128, seq_len, skip_rows)


def kernel(x, meta_tokens, ln_in_g, ln_in_b, w_in, diff_lambda, diff_subln_g, fox_forget_b, gdn_conv_w,
           gdn_a_log, gdn_dt_bias, gdn_norm_g, w_out, ln1_g, ln1_b, router_w, router_b, expert_w1,
           expert_b1, expert_w2, expert_b2, ln2_g, ln2_b):
    bsz, s, d = x.shape
    depth = w_in.shape[0]
    n_lead = (-(s + N_META)) % Q_BLOCK
    if n_lead + N_META != Q_BLOCK:
        raise ValueError("sequence length must be a multiple of the query block")
    seq_len = n_lead + N_META + s
    alpha = (2 * depth) ** 0.25
    stream = _ln_in(x, meta_tokens.astype(x.dtype), ln_in_g, ln_in_b, n_lead).reshape(bsz * seq_len, d)
    per_layer = dict(w_in=w_in, diff_lambda=diff_lambda, diff_subln_g=diff_subln_g, fox_forget_b=fox_forget_b,
                     gdn_conv_w=gdn_conv_w, gdn_a_log=gdn_a_log, gdn_dt_bias=gdn_dt_bias,
                     gdn_norm_g=gdn_norm_g, w_out=w_out, ln1_g=ln1_g, ln1_b=ln1_b, router_w=router_w,
                     router_b=router_b, expert_w1=expert_w1, expert_b1=expert_b1, expert_w2=expert_w2,
                     expert_b2=expert_b2, ln2_g=ln2_g, ln2_b=ln2_b)
    stacked = ("expert_w1", "expert_b1", "expert_w2", "expert_b2")
    for layer in range(depth):
        p = {k: (v if k in stacked else v[layer]) for k, v in per_layer.items()}
        last = layer == depth - 1
        stream = _layer(stream, layer, seq_len, n_lead, alpha, p, n_lead + N_META if last else 0)
    return stream.reshape(bsz, s, d)
```

```python
import functools
import math

import numpy as np
import jax
import jax.numpy as jnp
from jax import lax
from jax.experimental import pallas as pl
from jax.experimental.pallas import tpu as pltpu

F32 = jnp.float32
BF16 = jnp.bfloat16
I32 = jnp.int32

CHUNK = 64
CHUNK_SHIFT = 6
Q_BLOCK = 128
N_META = 16
DIFF_HEADS = 4
DIFF_QK_DIM = 64
DIFF_V_DIM = 128
FOX_HEADS = 4
FOX_HEAD_DIM = 64
GDN_HEADS = 4
GDN_HEAD_DIM = 64
CONV_K = 4
N_EXPERTS = 32
TOP_K = 4
SWIGLU_LIMIT = 7.0
SWIGLU_ALPHA = 1.702
MASK_VALUE = -1e30
LOG2_E = 1.4426950408889634
FOX_BIAS_PIECES = 3

DIFF_QK_WIDTH = DIFF_HEADS * 2 * DIFF_QK_DIM
DIFF_V_WIDTH = DIFF_HEADS * DIFF_V_DIM
FOX_WIDTH = FOX_HEADS * FOX_HEAD_DIM
GDN_WIDTH = GDN_HEADS * GDN_HEAD_DIM
GDN_CONV_CH = 3 * GDN_WIDTH
IN_PROJ_SIZES = (DIFF_QK_WIDTH, DIFF_QK_WIDTH, DIFF_V_WIDTH, FOX_WIDTH, FOX_WIDTH, FOX_WIDTH, FOX_HEADS,
                 GDN_CONV_CH, GDN_HEADS, GDN_HEADS, GDN_WIDTH)

LANES = 128
SUBLANES = 8
VMEM_LIMIT = 56 * 1024 * 1024

GATE_FF = 0
GATE_BETA = 4
GATE_DECAY = 8

ATTN_WIDTH = 2 * DIFF_QK_WIDTH + DIFF_V_WIDTH + 3 * FOX_WIDTH
GDN_SLAB = GDN_CONV_CH + GDN_WIDTH


def _cparams(*sem):
    return pltpu.CompilerParams(dimension_semantics=tuple(sem), vmem_limit_bytes=VMEM_LIMIT)


def _dot(a, b):
    return jnp.dot(a.astype(BF16), b.astype(BF16), preferred_element_type=F32)


def _dot_nt(a, b):
    return lax.dot_general(a.astype(BF16), b.astype(BF16), (((1,), (1,)), ((), ())),
                           preferred_element_type=F32)


def _split2(x):
    hi = x.astype(BF16)
    lo = (x - hi.astype(F32)).astype(BF16)
    return hi, lo


def _split3(x):
    hi = x.astype(BF16)
    r = x - hi.astype(F32)
    mid = r.astype(BF16)
    lo = (r - mid.astype(F32)).astype(BF16)
    return hi, mid, lo


def _dot3(a, b):
    ah, al = _split2(a)
    bh, bl = _split2(b)
    return (jnp.dot(ah, bh, preferred_element_type=F32) + jnp.dot(ah, bl, preferred_element_type=F32)
            + jnp.dot(al, bh, preferred_element_type=F32))


def _dot_exact_lhs(a01, b):
    a = a01.astype(BF16)
    hi, mid, lo = _split3(b)
    return (jnp.dot(a, hi, preferred_element_type=F32) + jnp.dot(a, mid, preferred_element_type=F32)
            + jnp.dot(a, lo, preferred_element_type=F32))


_BDIMS = (((2,), (1,)), ((0,), (0,)))
_BDIMS_NT = (((2,), (2,)), ((0,), (0,)))


def _bdot(a, b):
    return lax.dot_general(a.astype(BF16), b.astype(BF16), _BDIMS, preferred_element_type=F32)


def _bdot_nt(a, b):
    return lax.dot_general(a.astype(BF16), b.astype(BF16), _BDIMS_NT, preferred_element_type=F32)


def _bdot3(a, b):
    ah, al = _split2(a)
    bh, bl = _split2(b)
    return (lax.dot_general(ah, bh, _BDIMS, preferred_element_type=F32)
            + lax.dot_general(ah, bl, _BDIMS, preferred_element_type=F32)
            + lax.dot_general(al, bh, _BDIMS, preferred_element_type=F32))


def _bdot_exact_lhs(a01, b):
    a = a01.astype(BF16)
    hi, mid, lo = _split3(b)
    return (lax.dot_general(a, hi, _BDIMS, preferred_element_type=F32)
            + lax.dot_general(a, mid, _BDIMS, preferred_element_type=F32)
            + lax.dot_general(a, lo, _BDIMS, preferred_element_type=F32))


def _layer_norm(x, g, b, eps=1e-5):
    mu = jnp.mean(x, axis=-1, keepdims=True)
    xc = x - mu
    var = jnp.mean(xc * xc, axis=-1, keepdims=True)
    return xc * lax.rsqrt(var + eps) * g + b


def _sigmoid(x):
    return 1.0 / (1.0 + jnp.exp(-x))


def _softplus(x):
    return jnp.maximum(x, 0.0) + jnp.log1p(jnp.exp(-jnp.abs(x)))


def _log_sigmoid(x):
    return -_softplus(-x)


def _ln_in_kernel(x_ref, meta_ref, g_ref, b_ref, o_ref, *, n_lead):
    i = pl.program_id(1)
    g = g_ref[...]
    b = b_ref[...]

    @pl.when(i == 0)
    def _():
        o_ref[0, :n_lead, :] = jnp.broadcast_to(b, (n_lead, b.shape[-1]))
        o_ref[0, n_lead:, :] = _layer_norm(meta_ref[...], g, b)

    @pl.when(i > 0)
    def _():
        o_ref[0] = _layer_norm(x_ref[0], g, b)


def _ln_in(x, meta, g, b, n_lead):
    bsz, s, d = x.shape
    l = n_lead + N_META + s
    blk = n_lead + N_META
    return pl.pallas_call(
        functools.partial(_ln_in_kernel, n_lead=n_lead),
        out_shape=jax.ShapeDtypeStruct((bsz, l, d), F32),
        grid=(bsz, l // blk),
        in_specs=[pl.BlockSpec((1, blk, d), lambda bb, i: (bb, jnp.maximum(i - 1, 0), 0)),
                  pl.BlockSpec((N_META, d), lambda bb, i: (0, 0)),
                  pl.BlockSpec((1, d), lambda bb, i: (0, 0)),
                  pl.BlockSpec((1, d), lambda bb, i: (0, 0))],
        out_specs=pl.BlockSpec((1, blk, d), lambda bb, i: (bb, i, 0)),
        compiler_params=_cparams("arbitrary", "arbitrary"),
        name="ln_in",
    )(x, meta, g.reshape(1, d), b.reshape(1, d))


PROJ_COL_CHUNK = 768


def _proj_kernel(x_ref, wa_ref, wg_ref, ws_ref, oa_ref, og_ref, os_ref, *, tm, n_lead):
    i = pl.program_id(1)
    pos = i * tm + lax.broadcasted_iota(I32, (tm, 1), 0)
    x = jnp.where(pos >= n_lead, x_ref[...], 0.0)
    xb = x.astype(BF16)
    for c in range(0, ATTN_WIDTH, PROJ_COL_CHUNK):
        oa_ref[:, c:c + PROJ_COL_CHUNK] = jnp.dot(
            xb, wa_ref[:, c:c + PROJ_COL_CHUNK], preferred_element_type=F32).astype(BF16)
    og_ref[...] = jnp.dot(xb, wg_ref[...], preferred_element_type=F32)
    os_ref[...] = _dot3(x, ws_ref[...])


def _in_proj(stream, wa, wg, ws, seq_len, n_lead, tm):
    t_all, d = stream.shape
    nb = seq_len // tm
    row = lambda bb, i: (bb * nb + i, 0)
    const = lambda bb, i: (0, 0)
    return pl.pallas_call(
        functools.partial(_proj_kernel, tm=tm, n_lead=n_lead),
        out_shape=(jax.ShapeDtypeStruct((t_all, ATTN_WIDTH), BF16),
                   jax.ShapeDtypeStruct((t_all, GDN_SLAB), F32),
                   jax.ShapeDtypeStruct((t_all, LANES), F32)),
        grid=(t_all // seq_len, nb),
        in_specs=[pl.BlockSpec((tm, d), row),
                  pl.BlockSpec((d, ATTN_WIDTH), const),
                  pl.BlockSpec((d, GDN_SLAB), const),
                  pl.BlockSpec((d, LANES), const)],
        out_specs=(pl.BlockSpec((tm, ATTN_WIDTH), row),
                   pl.BlockSpec((tm, GDN_SLAB), row),
                   pl.BlockSpec((tm, LANES), row)),
        compiler_params=_cparams("arbitrary", "arbitrary"),
        name="in_proj",
    )(stream, wa, wg, ws)


def _online_softmax_step(s, vt, m_ref, l_ref, a_ref):
    m_old = m_ref[...]
    m_new = jnp.maximum(m_old, jnp.max(s, axis=0, keepdims=True))
    alpha = jnp.exp2(m_old - m_new)
    p = jnp.exp2(s - m_new)
    l_ref[...] = alpha * l_ref[...] + jnp.sum(p, axis=0, keepdims=True)
    a_ref[...] = alpha * a_ref[...] + jnp.dot(vt, p.astype(BF16), preferred_element_type=F32)
    m_ref[...] = m_new


def _pipelined_blocks(i, buf_a, buf_b, produce_first, produce, consume):
    produce_first(0, buf_a)

    def pair(t, carry):
        j = 2 * t
        produce(j + 1, buf_b)
        consume(j, buf_a)
        produce(j + 2, buf_a)
        consume(j + 1, buf_b)
        return carry

    lax.fori_loop(0, i // 2, pair, 0)

    @pl.when(i % 2 == 0)
    def _():
        consume(i, buf_a, diagonal=True)

    @pl.when(i % 2 == 1)
    def _():
        produce(i, buf_b)
        consume(i - 1, buf_a)
        consume(i, buf_b, diagonal=True)


def _transpose_to(dst_ref, src_ref, n_rows):
    for c in range(src_ref.shape[1] // LANES):
        def body(t, carry):
            r0 = pl.multiple_of(t * LANES, LANES)
            tile = src_ref[pl.ds(r0, LANES), c * LANES:(c + 1) * LANES].astype(F32)
            dst_ref[c * LANES:(c + 1) * LANES, pl.ds(r0, LANES)] = tile.T.astype(dst_ref.dtype)
            return carry
        lax.fori_loop(0, n_rows // LANES, body, 0)


def _init_softmax_state(m_ref, l_ref, a_ref):
    m_ref[...] = jnp.full(m_ref.shape, MASK_VALUE, F32)
    l_ref[...] = jnp.zeros(l_ref.shape, F32)
    a_ref[...] = jnp.zeros(a_ref.shape, F32)


def _diff_attn_kernel(lam_ref, q_ref, k_ref, v_ref, g_ref, o_ref, vt_ref, sa_ref, sb_ref,
                      m0, l0, a0, m1, l1, a1, *, tb, n_lead, seq_len, out_scale):
    i = pl.program_id(2)

    @pl.when(i == 0)
    def _():
        _transpose_to(vt_ref, v_ref, seq_len)

    qt = q_ref[...].astype(F32).T
    row = lax.broadcasted_iota(I32, (LANES, 1), 0)
    qt0 = jnp.where(row < DIFF_QK_DIM, qt, 0.0).astype(BF16)
    qt1 = jnp.where(row >= DIFF_QK_DIM, qt, 0.0).astype(BF16)
    _init_softmax_state(m0, l0, a0)
    _init_softmax_state(m1, l1, a1)

    def produce(j, buf, pad_bias=False):
        start = pl.multiple_of(j * tb, tb)
        k = k_ref[pl.ds(start, tb), :]
        for half, qth in enumerate((qt0, qt1)):
            s = jnp.dot(k, qth, preferred_element_type=F32)
            if pad_bias:
                kpos = lax.broadcasted_iota(I32, (tb, 1), 0)
                s = s + jnp.where(kpos >= n_lead, 0.0, MASK_VALUE)
            buf[half] = s

    def consume(j, buf, diagonal=False):
        start = pl.multiple_of(j * tb, tb)
        vt = vt_ref[:, pl.ds(start, tb)]
        for half, state in enumerate(((m0, l0, a0), (m1, l1, a1))):
            s = buf[half]
            if diagonal:
                kpos = j * tb + lax.broadcasted_iota(I32, (tb, 1), 0)
                qpos = i * tb + lax.broadcasted_iota(I32, (1, tb), 1)
                s = jnp.where(kpos >> CHUNK_SHIFT <= qpos >> CHUNK_SHIFT, s, MASK_VALUE)
            _online_softmax_step(s, vt, *state)

    _pipelined_blocks(i, sa_ref, sb_ref, functools.partial(produce, pad_bias=True), produce, consume)

    lam = lam_ref[0]
    o = a0[...] / l0[...] - lam * (a1[...] / l1[...])
    o = o * lax.rsqrt(jnp.mean(o * o, axis=0, keepdims=True) + 1e-5) * g_ref[...] * out_scale
    o_ref[...] = o.T.astype(o_ref.dtype)


def _diff_attention(attn, lam, subln_g, seq_len, n_lead, lambda_init, tb):
    t_all = attn.shape[0]
    bsz = t_all // seq_len
    nq = seq_len // tb
    kcol = DIFF_QK_WIDTH // LANES
    vcol = 2 * DIFF_QK_WIDTH // LANES
    scratch = [pltpu.VMEM((DIFF_V_DIM, seq_len), BF16),
               pltpu.VMEM((2, tb, tb), F32), pltpu.VMEM((2, tb, tb), F32)]
    for _ in range(2):
        scratch += [pltpu.VMEM((1, tb), F32), pltpu.VMEM((1, tb), F32), pltpu.VMEM((DIFF_V_DIM, tb), F32)]
    return pl.pallas_call(
        functools.partial(_diff_attn_kernel, tb=tb, n_lead=n_lead, seq_len=seq_len,
                          out_scale=1.0 - lambda_init),
        out_shape=jax.ShapeDtypeStruct((t_all, DIFF_V_WIDTH), BF16),
        grid_spec=pltpu.PrefetchScalarGridSpec(
            num_scalar_prefetch=1,
            grid=(bsz, DIFF_HEADS, nq),
            in_specs=[pl.BlockSpec((tb, LANES), lambda b, h, i, lam: (b * nq + i, h)),
                      pl.BlockSpec((seq_len, LANES), lambda b, h, i, lam: (b, kcol + h)),
                      pl.BlockSpec((seq_len, LANES), lambda b, h, i, lam: (b, vcol + h)),
                      pl.BlockSpec((DIFF_V_DIM, 1), lambda b, h, i, lam: (0, 0))],
            out_specs=pl.BlockSpec((tb, DIFF_V_DIM), lambda b, h, i, lam: (b * nq + i, h)),
            scratch_shapes=scratch),
        compiler_params=_cparams("arbitrary", "arbitrary", "arbitrary"),
        name="diff_attention",
    )(lam, attn, attn, attn, subln_g.reshape(DIFF_V_DIM, 1))


def _fox_bias_lane(hh):
    return (1 - hh % 2) * FOX_HEAD_DIM + FOX_BIAS_PIECES * (hh // 2)


def _fox_attn_kernel(q_ref, k_ref, v_ref, kb_ref, o_ref, vt_ref, sa_ref, sb_ref, *scratch, tb, seq_len):
    i = pl.program_id(1)
    pairs = FOX_HEADS // 2

    @pl.when(i == 0)
    def _():
        _transpose_to(vt_ref, v_ref, seq_len)

    row = lax.broadcasted_iota(I32, (LANES, 1), 0)
    lane = lax.broadcasted_iota(I32, (1, LANES), 1)
    qts = []
    for p in range(pairs):
        qt = q_ref[:, p * LANES:(p + 1) * LANES].astype(F32).T
        for r in range(2):
            own = jnp.logical_and(row >= r * FOX_HEAD_DIM, row < (r + 1) * FOX_HEAD_DIM)
            b0 = _fox_bias_lane(2 * p + r)
            ones = jnp.logical_and(row >= b0, row < b0 + FOX_BIAS_PIECES)
            qts.append(jnp.where(own, qt, jnp.where(ones, 1.0, 0.0)).astype(BF16))
    for hh in range(FOX_HEADS):
        _init_softmax_state(*scratch[3 * hh:3 * hh + 3])

    def produce(j, buf):
        start = pl.multiple_of(j * tb, tb)
        kb = kb_ref[pl.ds(start, tb), :]
        for p in range(pairs):
            k = k_ref[pl.ds(start, tb), p * LANES:(p + 1) * LANES]
            for r in range(2):
                own = jnp.logical_and(lane >= r * FOX_HEAD_DIM, lane < (r + 1) * FOX_HEAD_DIM)
                buf[2 * p + r] = jnp.dot(jnp.where(own, k, kb), qts[2 * p + r],
                                         preferred_element_type=F32)

    def consume(j, buf, diagonal=False):
        start = pl.multiple_of(j * tb, tb)
        for hh in range(FOX_HEADS):
            vt = vt_ref[hh * FOX_HEAD_DIM:(hh + 1) * FOX_HEAD_DIM, pl.ds(start, tb)]
            s = buf[hh]
            if diagonal:
                kpos = j * tb + lax.broadcasted_iota(I32, (tb, 1), 0)
                qpos = i * tb + lax.broadcasted_iota(I32, (1, tb), 1)
                s = jnp.where(kpos <= qpos, s, MASK_VALUE)
            _online_softmax_step(s, vt, *scratch[3 * hh:3 * hh + 3])

    _pipelined_blocks(i, sa_ref, sb_ref, produce, produce, consume)

    for p in range(pairs):
        _, l_lo, a_lo = scratch[6 * p:6 * p + 3]
        _, l_hi, a_hi = scratch[6 * p + 3:6 * p + 6]
        o = jnp.concatenate([a_lo[...] / l_lo[...], a_hi[...] / l_hi[...]], axis=0)
        o_ref[:, p * LANES:(p + 1) * LANES] = o.T.astype(o_ref.dtype)


def _fox_attention(attn, key_bias, seq_len, tb):
    t_all = attn.shape[0]
    bsz = t_all // seq_len
    nq = seq_len // tb
    base = (2 * DIFF_QK_WIDTH + DIFF_V_WIDTH) // FOX_WIDTH
    scratch = [pltpu.VMEM((FOX_WIDTH, seq_len), BF16),
               pltpu.VMEM((FOX_HEADS, tb, tb), F32), pltpu.VMEM((FOX_HEADS, tb, tb), F32)]
    for _ in range(FOX_HEADS):
        scratch += [pltpu.VMEM((1, tb), F32), pltpu.VMEM((1, tb), F32), pltpu.VMEM((FOX_HEAD_DIM, tb), F32)]
    return pl.pallas_call(
        functools.partial(_fox_attn_kernel, tb=tb, seq_len=seq_len),
        out_shape=jax.ShapeDtypeStruct((t_all, FOX_WIDTH), BF16),
        grid=(bsz, nq),
        in_specs=[pl.BlockSpec((tb, FOX_WIDTH), lambda b, i: (b * nq + i, base)),
                  pl.BlockSpec((seq_len, FOX_WIDTH), lambda b, i: (b, base + 1)),
                  pl.BlockSpec((seq_len, FOX_WIDTH), lambda b, i: (b, base + 2)),
                  pl.BlockSpec((seq_len, LANES), lambda b, i: (b, 0))],
        out_specs=pl.BlockSpec((tb, FOX_WIDTH), lambda b, i: (b * nq + i, 0)),
        scratch_shapes=scratch,
        compiler_params=_cparams("arbitrary", "arbitrary"),
        name="fox_attention",
    )(attn, attn, attn, key_bias)


def _forget_cumsum_kernel(x_ref, b_ref, kb_ref, carry_ref, *, tb, n_lead):
    i = pl.program_id(1)

    @pl.when(i == 0)
    def _():
        carry_ref[...] = jnp.zeros_like(carry_ref)

    log_f = _log_sigmoid(x_ref[...] + b_ref[...])
    r = lax.broadcasted_iota(I32, (tb, tb), 0)
    c = lax.broadcasted_iota(I32, (tb, tb), 1)
    tri = jnp.where(c <= r, 1.0, 0.0).astype(BF16)
    cum = _dot_exact_lhs(tri, log_f) + carry_ref[...]
    carry_ref[...] = cum[tb - 1:tb, :]

    pos = i * tb + lax.broadcasted_iota(I32, (tb, 1), 0)
    lane = lax.broadcasted_iota(I32, (1, LANES), 1)
    out = jnp.zeros((tb, LANES), F32)
    for hh in range(FOX_HEADS):
        bias = jnp.where(pos >= n_lead, -LOG2_E * cum[:, GATE_FF + hh:GATE_FF + hh + 1], MASK_VALUE)
        pieces = _split3(bias)
        for n, piece in enumerate(pieces):
            out = jnp.where(lane == _fox_bias_lane(hh) + n, piece.astype(F32), out)
    kb_ref[...] = out.astype(BF16)


def _forget_cumsum(gates, forget_bias_lanes, seq_len, n_lead):
    t_all = gates.shape[0]
    bsz = t_all // seq_len
    tb = LANES
    nb = seq_len // tb
    return pl.pallas_call(
        functools.partial(_forget_cumsum_kernel, tb=tb, n_lead=n_lead),
        out_shape=jax.ShapeDtypeStruct((t_all, LANES), BF16),
        grid=(bsz, nb),
        in_specs=[pl.BlockSpec((tb, LANES), lambda b, i: (b * nb + i, 0)),
                  pl.BlockSpec((1, LANES), lambda b, i: (0, 0))],
        out_specs=pl.BlockSpec((tb, LANES), lambda b, i: (b * nb + i, 0)),
        scratch_shapes=[pltpu.VMEM((1, LANES), F32)],
        compiler_params=_cparams("arbitrary", "arbitrary"),
        name="forget_cumsum",
    )(gates, forget_bias_lanes)


def _gdn_kernel(x_ref, gate_ref, cw_ref, alog_ref, dtb_ref, ng_ref, o_ref,
                halo_ref, xs_ref, state_ref, *, rows):
    cg = pl.program_id(1)
    g_chunks = rows // CHUNK
    hd = GDN_HEAD_DIM

    @pl.when(cg == 0)
    def _():
        halo_ref[...] = jnp.zeros_like(halo_ref)
        state_ref[...] = jnp.zeros_like(state_ref)

    xin = x_ref[:, :GDN_CONV_CH]
    xs_ref[0:SUBLANES, :] = halo_ref[...]
    xs_ref[SUBLANES:SUBLANES + rows, :] = xin
    halo_ref[...] = xin[rows - SUBLANES:rows, :]
    y = jnp.zeros((rows, GDN_CONV_CH), F32)
    for tap in range(CONV_K):
        back = CONV_K - 1 - tap
        y = y + cw_ref[tap:tap + 1, :] * xs_ref[SUBLANES - back:SUBLANES - back + rows, :]
    y = y * _sigmoid(y)

    gates = gate_ref[...]
    beta_all = _sigmoid(gates)
    g_all = -jnp.exp(alog_ref[...]) * _softplus(gates + dtb_ref[...])
    rowi = lax.broadcasted_iota(I32, (rows, 1), 0) & (CHUNK - 1)
    gc_all = g_all
    shift = 1
    while shift < CHUNK:
        gc_all = gc_all + jnp.where(rowi >= shift, pltpu.roll(gc_all, shift, 0), 0.0)
        shift *= 2

    ri = lax.broadcasted_iota(I32, (CHUNK, CHUNK), 0)
    ci = lax.broadcasted_iota(I32, (CHUNK, CHUNK), 1)
    eye = jnp.where(ri == ci, 1.0, 0.0).astype(F32)
    causal = (ci <= ri)[None]
    strict = (ci < ri)[None]
    eye_b = jnp.broadcast_to(eye[None], (g_chunks, CHUNK, CHUNK))

    prep = []
    for h in range(GDN_HEADS):
        def head3(base):
            return y[:, base + h * hd:base + (h + 1) * hd]
        q = head3(0)
        k = head3(GDN_WIDTH)
        v = head3(2 * GDN_WIDTH)
        q = q * lax.rsqrt(jnp.sum(q * q, axis=-1, keepdims=True) + 1e-6) * (hd ** -0.5)
        k = k * lax.rsqrt(jnp.sum(k * k, axis=-1, keepdims=True) + 1e-6)
        beta = beta_all[:, GATE_BETA + h:GATE_BETA + h + 1]
        gcol = jnp.broadcast_to(gc_all[:, GATE_DECAY + h:GATE_DECAY + h + 1], (rows, hd))
        egc = jnp.exp(gcol)

        q3 = q.reshape(g_chunks, CHUNK, hd)
        k3 = k.reshape(g_chunks, CHUNK, hd)
        gcol3 = gcol.reshape(g_chunks, CHUNK, CHUNK)
        beta3 = jnp.broadcast_to(beta, (rows, hd)).reshape(g_chunks, CHUNK, hd)
        grow3 = jnp.swapaxes(gcol3, 1, 2)
        decay = jnp.exp(jnp.where(causal, gcol3 - grow3, MASK_VALUE))
        a = jnp.where(strict, _bdot_nt(k3, k3) * beta3 * decay, 0.0)

        x = -a
        tinv = eye_b + x
        span = 2
        while span < CHUNK:
            x = _bdot(x, x)
            tinv = tinv + _bdot(tinv, x)
            span *= 2

        vb = (v * beta).reshape(g_chunks, CHUNK, hd)
        kb = (k * beta * egc).reshape(g_chunks, CHUNK, hd)
        u3 = _bdot(tinv, vb)
        w3 = _bdot(tinv, kb)
        qk3 = _bdot_nt(q3, k3) * decay
        qd3 = (q * egc).reshape(g_chunks, CHUNK, hd)
        glast3 = gcol3[:, CHUNK - 1:CHUNK, :]
        kdt3 = jnp.swapaxes(k3 * jnp.exp(glast3 - gcol3), 1, 2)
        cd3 = jnp.exp(glast3)
        prep.append((_bdot(kdt3, w3), _bdot(kdt3, u3), qd3 - _bdot(qk3, w3), _bdot(qk3, u3), cd3))

    states = [state_ref[h] for h in range(GDN_HEADS)]
    outs = [[] for _ in range(GDN_HEADS)]
    for c in range(g_chunks):
        for h in range(GDN_HEADS):
            kw3, ku3, qp3, o03, cd3 = prep[h]
            state = states[h]
            outs[h].append(_dot(qp3[c], state) + o03[c])
            states[h] = state * cd3[c] + (ku3[c] - _dot(kw3[c], state))

    for h in range(GDN_HEADS):
        state_ref[h] = states[h]
        o = jnp.concatenate(outs[h], axis=0)
        o = o * lax.rsqrt(jnp.mean(o * o, axis=-1, keepdims=True) + 1e-6) * ng_ref[...]
        z = x_ref[:, GDN_CONV_CH + h * hd:GDN_CONV_CH + (h + 1) * hd]
        o_ref[:, h * hd:(h + 1) * hd] = (o * (z * _sigmoid(z))).astype(o_ref.dtype)


def _gated_deltanet(gdn_slab, gates, conv_w, a_log_lanes, dt_bias_lanes, norm_g, seq_len, rows):
    t_all = gdn_slab.shape[0]
    bsz = t_all // seq_len
    nb = seq_len // rows
    row = lambda b, i: (b * nb + i, 0)
    const = lambda b, i: (0, 0)
    return pl.pallas_call(
        functools.partial(_gdn_kernel, rows=rows),
        out_shape=jax.ShapeDtypeStruct((t_all, GDN_WIDTH), BF16),
        grid=(bsz, nb),
        in_specs=[pl.BlockSpec((rows, GDN_SLAB), row),
                  pl.BlockSpec((rows, LANES), row),
                  pl.BlockSpec((SUBLANES, GDN_CONV_CH), const),
                  pl.BlockSpec((1, LANES), const),
                  pl.BlockSpec((1, LANES), const),
                  pl.BlockSpec((1, GDN_HEAD_DIM), const)],
        out_specs=pl.BlockSpec((rows, GDN_WIDTH), row),
        scratch_shapes=[pltpu.VMEM((SUBLANES, GDN_CONV_CH), F32),
                        pltpu.VMEM((rows + SUBLANES, GDN_CONV_CH), F32),
                        pltpu.VMEM((GDN_HEADS, GDN_HEAD_DIM, GDN_HEAD_DIM), F32)],
        compiler_params=_cparams("arbitrary", "arbitrary"),
        name="gated_deltanet",
    )(gdn_slab, gates, conv_w, a_log_lanes, dt_bias_lanes, norm_g.reshape(1, GDN_HEAD_DIM))


def _mix_router_kernel(x_ref, yd_ref, yf_ref, yg_ref, wd_ref, wf_ref, wg_ref, g_ref, b_ref,
                       rw_ref, rb_ref, x1_ref, sel_ref, gate_ref, cnt_ref, carry_ref, *, tm, alpha):
    i = pl.program_id(0)

    @pl.when(i == 0)
    def _():
        carry_ref[...] = jnp.zeros_like(carry_ref)

    mix = (jnp.dot(yd_ref[...], wd_ref[...], preferred_element_type=F32)
           + jnp.dot(yf_ref[...], wf_ref[...], preferred_element_type=F32)
           + jnp.dot(yg_ref[...], wg_ref[...], preferred_element_type=F32))
    x1 = _layer_norm(alpha * x_ref[...] + mix, g_ref[...], b_ref[...])
    x1_ref[...] = x1

    lane = lax.broadcasted_iota(I32, (tm, LANES), 1)
    lane_f = lane.astype(F32)
    logits = _dot3(x1, rw_ref[...]) + rb_ref[...]
    vals = jnp.where(lane < N_EXPERTS, logits, -jnp.inf)
    onehot = jnp.zeros((tm, LANES), F32)
    sels, tops = [], []
    for _ in range(TOP_K):
        top = jnp.max(vals, axis=-1, keepdims=True)
        idx = jnp.min(jnp.where(vals == top, lane_f, float(LANES)), axis=-1, keepdims=True)
        hit = lane_f == idx
        onehot = onehot + jnp.where(hit, 1.0, 0.0)
        vals = jnp.where(hit, -jnp.inf, vals)
        sels.append((idx, hit))
        tops.append(top)
    exps = [jnp.exp(t - tops[0]) for t in tops]
    denom = exps[0] + exps[1] + exps[2] + exps[3]

    r = lax.broadcasted_iota(I32, (tm, tm), 0)
    c = lax.broadcasted_iota(I32, (tm, tm), 1)
    tri = jnp.where(c < r, 1.0, 0.0).astype(BF16)
    prefix = jnp.dot(tri, onehot.astype(BF16), preferred_element_type=F32) + carry_ref[...]
    sel_out = jnp.zeros((tm, LANES), I32)
    gate_out = jnp.zeros((tm, LANES), F32)
    for kk in range(TOP_K):
        idx, hit = sels[kk]
        rank = jnp.sum(jnp.where(hit, prefix, 0.0), axis=-1, keepdims=True).astype(I32)
        sel_out = jnp.where(lane == kk, idx.astype(I32), sel_out)
        sel_out = jnp.where(lane == TOP_K + kk, rank, sel_out)
        gate_out = jnp.where(lane == kk, exps[kk] / denom, gate_out)
    sel_ref[...] = sel_out
    gate_ref[...] = gate_out
    carry_ref[...] = carry_ref[...] + jnp.sum(onehot, axis=0, keepdims=True)
    cnt_ref[...] = jnp.broadcast_to(carry_ref[...], cnt_ref.shape)


def _mix_router(stream, yd, yf, yg, w_out_b, ln_g, ln_b, rw, rb, alpha, tm):
    t_all, d = stream.shape
    row = lambda i: (i, 0)
    const = lambda i: (0, 0)
    wd = w_out_b[:DIFF_V_WIDTH]
    wf = w_out_b[DIFF_V_WIDTH:DIFF_V_WIDTH + FOX_WIDTH]
    wg = w_out_b[DIFF_V_WIDTH + FOX_WIDTH:]
    return pl.pallas_call(
        functools.partial(_mix_router_kernel, tm=tm, alpha=alpha),
        out_shape=(jax.ShapeDtypeStruct((t_all, d), F32),
                   jax.ShapeDtypeStruct((t_all, LANES), I32),
                   jax.ShapeDtypeStruct((t_all, LANES), F32),
                   jax.ShapeDtypeStruct((SUBLANES, LANES), F32)),
        grid=(t_all // tm,),
        in_specs=[pl.BlockSpec((tm, d), row),
                  pl.BlockSpec((tm, DIFF_V_WIDTH), row),
                  pl.BlockSpec((tm, FOX_WIDTH), row),
                  pl.BlockSpec((tm, GDN_WIDTH), row),
                  pl.BlockSpec((DIFF_V_WIDTH, d), const),
                  pl.BlockSpec((FOX_WIDTH, d), const),
                  pl.BlockSpec((GDN_WIDTH, d), const),
                  pl.BlockSpec((1, d), const),
                  pl.BlockSpec((1, d), const),
                  pl.BlockSpec((d, LANES), const),
                  pl.BlockSpec((1, LANES), const)],
        out_specs=(pl.BlockSpec((tm, d), row),
                   pl.BlockSpec((tm, LANES), row),
                   pl.BlockSpec((tm, LANES), row),
                   pl.BlockSpec((SUBLANES, LANES), const)),
        scratch_shapes=[pltpu.VMEM((1, LANES), F32)],
        compiler_params=_cparams("arbitrary"),
        name="mix_router",
    )(stream, yd, yf, yg, wd, wf, wg, ln_g.reshape(1, d), ln_b.reshape(1, d), rw, rb)


def _dispatch_kernel(dest_ref, x_ref, xs_in_ref, xs_ref, sem, *, tm):
    del xs_in_ref
    base = pl.program_id(0) * tm

    def row_copy(r, kk):
        slot = dest_ref[(base + r) * TOP_K + kk]
        return pltpu.make_async_copy(x_ref.at[pl.ds(r, 1)], xs_ref.at[pl.ds(slot, 1)], sem)

    def issue(r, c):
        for kk in range(TOP_K):
            row_copy(r, kk).start()
        return c

    lax.fori_loop(0, tm, issue, 0)

    def drain(r, c):
        for kk in range(TOP_K):
            row_copy(r, kk).wait()
        return c

    lax.fori_loop(0, tm, drain, 0)


def _dispatch(dest, x1, n_slots, tm):
    t_all, d = x1.shape
    xs0 = jnp.zeros((n_slots, d), F32)
    return pl.pallas_call(
        functools.partial(_dispatch_kernel, tm=tm),
        out_shape=jax.ShapeDtypeStruct((n_slots, d), F32),
        grid_spec=pltpu.PrefetchScalarGridSpec(
            num_scalar_prefetch=1,
            grid=(t_all // tm,),
            in_specs=[pl.BlockSpec((tm, d), lambda i, dst: (i, 0)), pl.BlockSpec(memory_space=pl.ANY)],
            out_specs=pl.BlockSpec(memory_space=pl.ANY),
            scratch_shapes=[pltpu.SemaphoreType.DMA]),
        input_output_aliases={2: 0},
        compiler_params=_cparams("arbitrary"),
        name="moe_dispatch",
    )(dest, x1, xs0)


def _expert_kernel(be_ref, na_ref, x_ref, w1_ref, b1_ref, w2_ref, b2_ref, o_ref, w1b_ref, w2b_ref, *, d_expert):
    i = pl.program_id(0)
    active = i < na_ref[0]
    new_expert = jnp.logical_or(i == 0, be_ref[i] != be_ref[jnp.maximum(i - 1, 0)])

    @pl.when(jnp.logical_and(active, new_expert))
    def _():
        w1b_ref[...] = w1_ref[0].astype(BF16)
        w2b_ref[...] = w2_ref[0].astype(BF16)

    @pl.when(active)
    def _():
        hid = jnp.dot(x_ref[...].astype(BF16), w1b_ref[...], preferred_element_type=F32) + b1_ref[0]
        glu = jnp.minimum(hid[:, :d_expert], SWIGLU_LIMIT)
        lin = jnp.clip(hid[:, d_expert:], -SWIGLU_LIMIT, SWIGLU_LIMIT)
        act = glu * _sigmoid(SWIGLU_ALPHA * glu) * (lin + 1.0)
        o_ref[...] = jnp.dot(act.astype(BF16), w2b_ref[...], preferred_element_type=F32) + b2_ref[0]

    @pl.when(jnp.logical_not(active))
    def _():
        o_ref[...] = jnp.zeros_like(o_ref)


def _experts(block_expert, n_active, xs, w1_all, b1, w2_all, b2, layer, mb):
    n_slots, d = xs.shape
    depth, n_exp, _, f2 = w1_all.shape
    f = f2 // 2
    nb = n_slots // mb
    w1 = w1_all.reshape(depth * n_exp, d, f2)
    w2 = w2_all.reshape(depth * n_exp, f, d)
    block_expert = block_expert + layer * n_exp
    return pl.pallas_call(
        functools.partial(_expert_kernel, d_expert=f),
        out_shape=jax.ShapeDtypeStruct((n_slots, d), F32),
        grid_spec=pltpu.PrefetchScalarGridSpec(
            num_scalar_prefetch=2,
            grid=(nb,),
            in_specs=[pl.BlockSpec((mb, d), lambda i, be, na: (i, 0)),
                      pl.BlockSpec((1, d, f2), lambda i, be, na: (be[i], 0, 0)),
                      pl.BlockSpec((1, 1, f2), lambda i, be, na: (be[i], 0, 0)),
                      pl.BlockSpec((1, f, d), lambda i, be, na: (be[i], 0, 0)),
                      pl.BlockSpec((1, 1, d), lambda i, be, na: (be[i], 0, 0))],
            out_specs=pl.BlockSpec((mb, d), lambda i, be, na: (i, 0)),
            scratch_shapes=[pltpu.VMEM((d, f2), BF16), pltpu.VMEM((f, d), BF16)]),
        compiler_params=_cparams("arbitrary"),
        name="moe_experts",
    )(block_expert, n_active, xs, w1, b1.reshape(depth * n_exp, 1, f2), w2, b2.reshape(depth * n_exp, 1, d))


def _combine_kernel(dest_ref, x1_ref, gate_ref, g_ref, b_ref, ys_ref, o_ref, buf_ref, sem, *,
                    tm, alpha, blocks_per_seq, skip):
    base = (pl.program_id(0) * blocks_per_seq + skip + pl.program_id(1)) * tm

    def row_copy(r, kk):
        slot = dest_ref[(base + r) * TOP_K + kk]
        return pltpu.make_async_copy(ys_ref.at[pl.ds(slot, 1)], buf_ref.at[kk, pl.ds(r, 1)], sem)

    def issue(r, c):
        for kk in range(TOP_K):
            row_copy(r, kk).start()
        return c

    lax.fori_loop(0, tm, issue, 0)

    def drain(r, c):
        for kk in range(TOP_K):
            row_copy(r, kk).wait()
        return c

    lax.fori_loop(0, tm, drain, 0)

    gate = gate_ref[...]
    acc = alpha * x1_ref[...]
    for kk in range(TOP_K):
        acc = acc + gate[:, kk:kk + 1] * buf_ref[kk]
    o_ref[...] = _layer_norm(acc, g_ref[...], b_ref[...])


def _combine(dest, x1, gate, ln_g, ln_b, ys, alpha, tm, seq_len, skip_rows):
    t_all, d = x1.shape
    bsz = t_all // seq_len
    nb = seq_len // tm
    skip = skip_rows // tm
    row = lambda b, i, dst: (b * nb + skip + i, 0)
    const = lambda b, i, dst: (0, 0)
    return pl.pallas_call(
        functools.partial(_combine_kernel, tm=tm, alpha=alpha, blocks_per_seq=nb, skip=skip),
        out_shape=jax.ShapeDtypeStruct((bsz * (seq_len - skip_rows), d), F32),
        grid_spec=pltpu.PrefetchScalarGridSpec(
            num_scalar_prefetch=1,
            grid=(bsz, nb - skip),
            in_specs=[pl.BlockSpec((tm, d), row),
                      pl.BlockSpec((tm, LANES), row),
                      pl.BlockSpec((1, d), const),
                      pl.BlockSpec((1, d), const),
                      pl.BlockSpec(memory_space=pl.ANY)],
            out_specs=pl.BlockSpec((tm, d), lambda b, i, dst: (b * (nb - skip) + i, 0)),
            scratch_shapes=[pltpu.VMEM((TOP_K, tm, d), F32), pltpu.SemaphoreType.DMA]),
        compiler_params=_cparams("arbitrary", "arbitrary"),
        name="moe_combine",
    )(dest, x1, gate, ln_g.reshape(1, d), ln_b.reshape(1, d), ys)


def _lanes(values, offset):
    return jnp.zeros((1, LANES), F32).at[0, offset:offset + values.shape[0]].set(values.astype(F32))


def _split_in_proj(w_in_layer):
    offs = np.concatenate([[0], np.cumsum(IN_PROJ_SIZES)])
    seg = lambda n: w_in_layer[:, offs[n]:offs[n + 1]]
    qk_scale = DIFF_QK_DIM ** -0.5 * LOG2_E
    fox_scale = FOX_HEAD_DIM ** -0.5 * LOG2_E
    wa = jnp.concatenate([seg(0) * qk_scale, seg(1), seg(2), seg(3) * fox_scale, seg(4), seg(5)], axis=1)
    wg = jnp.concatenate([seg(7), seg(10)], axis=1)
    d = w_in_layer.shape[0]
    ws = jnp.zeros((d, LANES), F32)
    ws = ws.at[:, GATE_FF:GATE_FF + FOX_HEADS].set(seg(6))
    ws = ws.at[:, GATE_BETA:GATE_BETA + GDN_HEADS].set(seg(8))
    ws = ws.at[:, GATE_DECAY:GATE_DECAY + GDN_HEADS].set(seg(9))
    return wa.astype(BF16), wg.astype(BF16), ws


def _diff_lambda_init(layer):
    return 0.8 - 0.6 * math.exp(-0.3 * layer)


def _pick_block(seq_len, candidates):
    for c in candidates:
        if seq_len % c == 0:
            return c
    raise ValueError(f"unsupported sequence length {seq_len}")


MOE_ROWS = 256


def _layer(stream, layer, seq_len, n_lead, alpha, p, skip_rows):
    t_all, d = stream.shape
    attn_tb = _pick_block(seq_len, (640, 512, 256, 128))
    wa, wg, ws = _split_in_proj(p["w_in"])
    attn, gdn_slab, gates = _in_proj(stream, wa, wg, ws, seq_len, n_lead, attn_tb)

    lv = p["diff_lambda"].astype(F32)
    lambda_init = _diff_lambda_init(layer)
    lam = (jnp.exp(jnp.sum(lv[0] * lv[1])) - jnp.exp(jnp.sum(lv[2] * lv[3])) + lambda_init).reshape(1)
    y_diff = _diff_attention(attn, lam, p["diff_subln_g"], seq_len, n_lead, lambda_init, attn_tb)

    key_bias = _forget_cumsum(gates, _lanes(p["fox_forget_b"], GATE_FF), seq_len, n_lead)
    y_fox = _fox_attention(attn, key_bias, seq_len, attn_tb)

    conv_w = jnp.zeros((SUBLANES, GDN_CONV_CH), F32).at[:CONV_K].set(p["gdn_conv_w"])
    y_gdn = _gated_deltanet(gdn_slab, gates, conv_w, _lanes(p["gdn_a_log"], GATE_DECAY),
                            _lanes(p["gdn_dt_bias"], GATE_DECAY), p["gdn_norm_g"], seq_len, attn_tb)

    rw = jnp.zeros((d, LANES), F32).at[:, :N_EXPERTS].set(p["router_w"])
    rb = _lanes(p["router_b"], 0)
    x1, sel, gate, counts = _mix_router(stream, y_diff, y_fox, y_gdn, p["w_out"].astype(BF16),
                                        p["ln1_g"], p["ln1_b"], rw, rb, alpha, 256)

    mb = MOE_ROWS
    n_assign = t_all * TOP_K
    n_blocks = -(-(n_assign + N_EXPERTS * (mb - 1)) // mb)
    n_slots = n_blocks * mb
    cnt = counts[0, :N_EXPERTS].astype(I32)
    padded = (cnt + mb - 1) // mb * mb
    pad_end = jnp.cumsum(padded)
    pad_start = pad_end - padded
    experts = jnp.arange(N_EXPERTS, dtype=I32)
    start_of = jnp.sum(jnp.where(sel[:, :TOP_K, None] == experts, pad_start, 0), axis=-1)
    dest = (start_of + sel[:, TOP_K:2 * TOP_K]).reshape(-1).astype(I32)
    block_first = jnp.arange(n_blocks, dtype=I32) * mb
    block_expert = jnp.minimum(jnp.sum((pad_end[None, :] <= block_first[:, None]).astype(I32), axis=1),
                               N_EXPERTS - 1)
    n_active = (pad_end[-1] // mb).astype(I32).reshape(1)

    xs = _dispatch(dest, x1, n_slots, 256)
    ys = _experts(block_expert, n_active, xs, p["expert_w1"], p["expert_b1"], p["expert_w2"], p["expert_b2"],
                  layer, mb)
    return _combine(dest, x1, gate, p["ln2_g"], p["ln2_b"], ys, alpha, 128, seq_len, skip_rows)


def kernel(x, meta_tokens, ln_in_g, ln_in_b, w_in, diff_lambda, diff_subln_g, fox_forget_b, gdn_conv_w,
           gdn_a_log, gdn_dt_bias, gdn_norm_g, w_out, ln1_g, ln1_b, router_w, router_b, expert_w1,
           expert_b1, expert_w2, expert_b2, ln2_g, ln2_b):
    bsz, s, d = x.shape
    depth = w_in.shape[0]
    n_lead = (-(s + N_META)) % Q_BLOCK
    if n_lead + N_META != Q_BLOCK:
        raise ValueError("sequence length must be a multiple of the query block")
    seq_len = n_lead + N_META + s
    alpha = (2 * depth) ** 0.25
    stream = _ln_in(x, meta_tokens.astype(x.dtype), ln_in_g, ln_in_b, n_lead).reshape(bsz * seq_len, d)
    per_layer = dict(w_in=w_in, diff_lambda=diff_lambda, diff_subln_g=diff_subln_g, fox_forget_b=fox_forget_b,
                     gdn_conv_w=gdn_conv_w, gdn_a_log=gdn_a_log, gdn_dt_bias=gdn_dt_bias,
                     gdn_norm_g=gdn_norm_g, w_out=w_out, ln1_g=ln1_g, ln1_b=ln1_b, router_w=router_w,
                     router_b=router_b, expert_w1=expert_w1, expert_b1=expert_b1, expert_w2=expert_w2,
                     expert_b2=expert_b2, ln2_g=ln2_g, ln2_b=ln2_b)
    stacked = ("expert_w1", "expert_b1", "expert_w2", "expert_b2")
    for layer in range(depth):
        p = {k: (v if k in stacked else v[layer]) for k, v in per_layer.items()}
        last = layer == depth - 1
        stream = _layer(stream, layer, seq_len, n_lead, alpha, p, n_lead + N_META if last else 0)
    return stream.reshape(bsz, s, d)
```

```python
import functools
import math

import numpy as np
import jax
import jax.numpy as jnp
from jax import lax
from jax.experimental import pallas as pl
from jax.experimental.pallas import tpu as pltpu

F32 = jnp.float32
BF16 = jnp.bfloat16
I32 = jnp.int32

CHUNK = 64
CHUNK_SHIFT = 6
Q_BLOCK = 128
N_META = 16
DIFF_HEADS = 4
DIFF_QK_DIM = 64
DIFF_V_DIM = 128
FOX_HEADS = 4
FOX_HEAD_DIM = 64
GDN_HEADS = 4
GDN_HEAD_DIM = 64
CONV_K = 4
N_EXPERTS = 32
TOP_K = 4
SWIGLU_LIMIT = 7.0
SWIGLU_ALPHA = 1.702
MASK_VALUE = -1e30
LOG2_E = 1.4426950408889634
FOX_BIAS_PIECES = 3

DIFF_QK_WIDTH = DIFF_HEADS * 2 * DIFF_QK_DIM
DIFF_V_WIDTH = DIFF_HEADS * DIFF_V_DIM
FOX_WIDTH = FOX_HEADS * FOX_HEAD_DIM
GDN_WIDTH = GDN_HEADS * GDN_HEAD_DIM
GDN_CONV_CH = 3 * GDN_WIDTH
IN_PROJ_SIZES = (DIFF_QK_WIDTH, DIFF_QK_WIDTH, DIFF_V_WIDTH, FOX_WIDTH, FOX_WIDTH, FOX_WIDTH, FOX_HEADS,
                 GDN_CONV_CH, GDN_HEADS, GDN_HEADS, GDN_WIDTH)

LANES = 128
SUBLANES = 8
VMEM_LIMIT = 56 * 1024 * 1024

GATE_FF = 0
GATE_BETA = 4
GATE_DECAY = 8

ATTN_WIDTH = 2 * DIFF_QK_WIDTH + DIFF_V_WIDTH + 3 * FOX_WIDTH
GDN_SLAB = GDN_CONV_CH + GDN_WIDTH


def _cparams(*sem):
    return pltpu.CompilerParams(dimension_semantics=tuple(sem), vmem_limit_bytes=VMEM_LIMIT)


def _dot(a, b):
    return jnp.dot(a.astype(BF16), b.astype(BF16), preferred_element_type=F32)


def _dot_nt(a, b):
    return lax.dot_general(a.astype(BF16), b.astype(BF16), (((1,), (1,)), ((), ())),
                           preferred_element_type=F32)


def _split2(x):
    hi = x.astype(BF16)
    lo = (x - hi.astype(F32)).astype(BF16)
    return hi, lo


def _split3(x):
    hi = x.astype(BF16)
    r = x - hi.astype(F32)
    mid = r.astype(BF16)
    lo = (r - mid.astype(F32)).astype(BF16)
    return hi, mid, lo


def _dot3(a, b):
    ah, al = _split2(a)
    bh, bl = _split2(b)
    return (jnp.dot(ah, bh, preferred_element_type=F32) + jnp.dot(ah, bl, preferred_element_type=F32)
            + jnp.dot(al, bh, preferred_element_type=F32))


def _dot_exact_lhs(a01, b):
    a = a01.astype(BF16)
    hi, mid, lo = _split3(b)
    return (jnp.dot(a, hi, preferred_element_type=F32) + jnp.dot(a, mid, preferred_element_type=F32)
            + jnp.dot(a, lo, preferred_element_type=F32))


_BDIMS = (((2,), (1,)), ((0,), (0,)))
_BDIMS_NT = (((2,), (2,)), ((0,), (0,)))


def _bdot(a, b):
    return lax.dot_general(a.astype(BF16), b.astype(BF16), _BDIMS, preferred_element_type=F32)


def _bdot_nt(a, b):
    return lax.dot_general(a.astype(BF16), b.astype(BF16), _BDIMS_NT, preferred_element_type=F32)


def _bdot3(a, b):
    ah, al = _split2(a)
    bh, bl = _split2(b)
    return (lax.dot_general(ah, bh, _BDIMS, preferred_element_type=F32)
            + lax.dot_general(ah, bl, _BDIMS, preferred_element_type=F32)
            + lax.dot_general(al, bh, _BDIMS, preferred_element_type=F32))


def _bdot_exact_lhs(a01, b):
    a = a01.astype(BF16)
    hi, mid, lo = _split3(b)
    return (lax.dot_general(a, hi, _BDIMS, preferred_element_type=F32)
            + lax.dot_general(a, mid, _BDIMS, preferred_element_type=F32)
            + lax.dot_general(a, lo, _BDIMS, preferred_element_type=F32))


def _layer_norm(x, g, b, eps=1e-5):
    mu = jnp.mean(x, axis=-1, keepdims=True)
    xc = x - mu
    var = jnp.mean(xc * xc, axis=-1, keepdims=True)
    return xc * lax.rsqrt(var + eps) * g + b


def _sigmoid(x):
    return 1.0 / (1.0 + jnp.exp(-x))


def _softplus(x):
    return jnp.maximum(x, 0.0) + jnp.log1p(jnp.exp(-jnp.abs(x)))


def _log_sigmoid(x):
    return -_softplus(-x)


def _ln_in_kernel(x_ref, meta_ref, g_ref, b_ref, o_ref, *, n_lead):
    i = pl.program_id(1)
    g = g_ref[...]
    b = b_ref[...]

    @pl.when(i == 0)
    def _():
        o_ref[0, :n_lead, :] = jnp.broadcast_to(b, (n_lead, b.shape[-1]))
        o_ref[0, n_lead:, :] = _layer_norm(meta_ref[...], g, b)

    @pl.when(i > 0)
    def _():
        o_ref[0] = _layer_norm(x_ref[0], g, b)


def _ln_in(x, meta, g, b, n_lead):
    bsz, s, d = x.shape
    l = n_lead + N_META + s
    blk = n_lead + N_META
    return pl.pallas_call(
        functools.partial(_ln_in_kernel, n_lead=n_lead),
        out_shape=jax.ShapeDtypeStruct((bsz, l, d), F32),
        grid=(bsz, l // blk),
        in_specs=[pl.BlockSpec((1, blk, d), lambda bb, i: (bb, jnp.maximum(i - 1, 0), 0)),
                  pl.BlockSpec((N_META, d), lambda bb, i: (0, 0)),
                  pl.BlockSpec((1, d), lambda bb, i: (0, 0)),
                  pl.BlockSpec((1, d), lambda bb, i: (0, 0))],
        out_specs=pl.BlockSpec((1, blk, d), lambda bb, i: (bb, i, 0)),
        compiler_params=_cparams("arbitrary", "arbitrary"),
        name="ln_in",
    )(x, meta, g.reshape(1, d), b.reshape(1, d))


PROJ_COL_CHUNK = 768


def _proj_kernel(x_ref, wa_ref, wg_ref, ws_ref, oa_ref, og_ref, os_ref, *, tm, n_lead):
    i = pl.program_id(1)
    pos = i * tm + lax.broadcasted_iota(I32, (tm, 1), 0)
    x = jnp.where(pos >= n_lead, x_ref[...], 0.0)
    xb = x.astype(BF16)
    for c in range(0, ATTN_WIDTH, PROJ_COL_CHUNK):
        oa_ref[:, c:c + PROJ_COL_CHUNK] = jnp.dot(
            xb, wa_ref[:, c:c + PROJ_COL_CHUNK], preferred_element_type=F32).astype(BF16)
    og_ref[...] = jnp.dot(xb, wg_ref[...], preferred_element_type=F32)
    os_ref[...] = _dot3(x, ws_ref[...])


def _in_proj(stream, wa, wg, ws, seq_len, n_lead, tm):
    t_all, d = stream.shape
    nb = seq_len // tm
    row = lambda bb, i: (bb * nb + i, 0)
    const = lambda bb, i: (0, 0)
    return pl.pallas_call(
        functools.partial(_proj_kernel, tm=tm, n_lead=n_lead),
        out_shape=(jax.ShapeDtypeStruct((t_all, ATTN_WIDTH), BF16),
                   jax.ShapeDtypeStruct((t_all, GDN_SLAB), F32),
                   jax.ShapeDtypeStruct((t_all, LANES), F32)),
        grid=(t_all // seq_len, nb),
        in_specs=[pl.BlockSpec((tm, d), row),
                  pl.BlockSpec((d, ATTN_WIDTH), const),
                  pl.BlockSpec((d, GDN_SLAB), const),
                  pl.BlockSpec((d, LANES), const)],
        out_specs=(pl.BlockSpec((tm, ATTN_WIDTH), row),
                   pl.BlockSpec((tm, GDN_SLAB), row),
                   pl.BlockSpec((tm, LANES), row)),
        compiler_params=_cparams("arbitrary", "arbitrary"),
        name="in_proj",
    )(stream, wa, wg, ws)


def _online_softmax_step(s, vt, m_ref, l_ref, a_ref):
    m_old = m_ref[...]
    m_new = jnp.maximum(m_old, jnp.max(s, axis=0, keepdims=True))
    alpha = jnp.exp2(m_old - m_new)
    p = jnp.exp2(s - m_new)
    l_ref[...] = alpha * l_ref[...] + jnp.sum(p, axis=0, keepdims=True)
    a_ref[...] = alpha * a_ref[...] + jnp.dot(vt, p.astype(BF16), preferred_element_type=F32)
    m_ref[...] = m_new


def _pipelined_blocks(i, buf_a, buf_b, produce_first, produce, consume):
    produce_first(0, buf_a)

    def pair(t, carry):
        j = 2 * t
        produce(j + 1, buf_b)
        consume(j, buf_a)
        produce(j + 2, buf_a)
        consume(j + 1, buf_b)
        return carry

    lax.fori_loop(0, i // 2, pair, 0)

    @pl.when(i % 2 == 0)
    def _():
        consume(i, buf_a, diagonal=True)

    @pl.when(i % 2 == 1)
    def _():
        produce(i, buf_b)
        consume(i - 1, buf_a)
        consume(i, buf_b, diagonal=True)


def _transpose_to(dst_ref, src_ref, n_rows):
    for c in range(src_ref.shape[1] // LANES):
        def body(t, carry):
            r0 = pl.multiple_of(t * LANES, LANES)
            tile = src_ref[pl.ds(r0, LANES), c * LANES:(c + 1) * LANES].astype(F32)
            dst_ref[c * LANES:(c + 1) * LANES, pl.ds(r0, LANES)] = tile.T.astype(dst_ref.dtype)
            return carry
        lax.fori_loop(0, n_rows // LANES, body, 0)


def _init_softmax_state(m_ref, l_ref, a_ref):
    m_ref[...] = jnp.full(m_ref.shape, MASK_VALUE, F32)
    l_ref[...] = jnp.zeros(l_ref.shape, F32)
    a_ref[...] = jnp.zeros(a_ref.shape, F32)


def _diff_attn_kernel(lam_ref, q_ref, k_ref, v_ref, g_ref, o_ref, vt_ref, sa_ref, sb_ref,
                      m0, l0, a0, m1, l1, a1, *, tb, n_lead, seq_len, out_scale):
    i = pl.program_id(2)

    @pl.when(i == 0)
    def _():
        _transpose_to(vt_ref, v_ref, seq_len)

    qt = q_ref[...].astype(F32).T
    row = lax.broadcasted_iota(I32, (LANES, 1), 0)
    qt0 = jnp.where(row < DIFF_QK_DIM, qt, 0.0).astype(BF16)
    qt1 = jnp.where(row >= DIFF_QK_DIM, qt, 0.0).astype(BF16)
    _init_softmax_state(m0, l0, a0)
    _init_softmax_state(m1, l1, a1)

    def produce(j, buf, pad_bias=False):
        start = pl.multiple_of(j * tb, tb)
        k = k_ref[pl.ds(start, tb), :]
        for half, qth in enumerate((qt0, qt1)):
            s = jnp.dot(k, qth, preferred_element_type=F32)
            if pad_bias:
                kpos = lax.broadcasted_iota(I32, (tb, 1), 0)
                s = s + jnp.where(kpos >= n_lead, 0.0, MASK_VALUE)
            buf[half] = s

    def consume(j, buf, diagonal=False):
        start = pl.multiple_of(j * tb, tb)
        vt = vt_ref[:, pl.ds(start, tb)]
        for half, state in enumerate(((m0, l0, a0), (m1, l1, a1))):
            s = buf[half]
            if diagonal:
                kpos = j * tb + lax.broadcasted_iota(I32, (tb, 1), 0)
                qpos = i * tb + lax.broadcasted_iota(I32, (1, tb), 1)
                s = jnp.where(kpos >> CHUNK_SHIFT <= qpos >> CHUNK_SHIFT, s, MASK_VALUE)
            _online_softmax_step(s, vt, *state)

    _pipelined_blocks(i, sa_ref, sb_ref, functools.partial(produce, pad_bias=True), produce, consume)

    lam = lam_ref[0]
    o = a0[...] / l0[...] - lam * (a1[...] / l1[...])
    o = o * lax.rsqrt(jnp.mean(o * o, axis=0, keepdims=True) + 1e-5) * g_ref[...] * out_scale
    o_ref[...] = o.T.astype(o_ref.dtype)


def _diff_attention(attn, lam, subln_g, seq_len, n_lead, lambda_init, tb):
    t_all = attn.shape[0]
    bsz = t_all // seq_len
    nq = seq_len // tb
    kcol = DIFF_QK_WIDTH // LANES
    vcol = 2 * DIFF_QK_WIDTH // LANES
    scratch = [pltpu.VMEM((DIFF_V_DIM, seq_len), BF16),
               pltpu.VMEM((2, tb, tb), F32), pltpu.VMEM((2, tb, tb), F32)]
    for _ in range(2):
        scratch += [pltpu.VMEM((1, tb), F32), pltpu.VMEM((1, tb), F32), pltpu.VMEM((DIFF_V_DIM, tb), F32)]
    return pl.pallas_call(
        functools.partial(_diff_attn_kernel, tb=tb, n_lead=n_lead, seq_len=seq_len,
                          out_scale=1.0 - lambda_init),
        out_shape=jax.ShapeDtypeStruct((t_all, DIFF_V_WIDTH), BF16),
        grid_spec=pltpu.PrefetchScalarGridSpec(
            num_scalar_prefetch=1,
            grid=(bsz, DIFF_HEADS, nq),
            in_specs=[pl.BlockSpec((tb, LANES), lambda b, h, i, lam: (b * nq + i, h)),
                      pl.BlockSpec((seq_len, LANES), lambda b, h, i, lam: (b, kcol + h)),
                      pl.BlockSpec((seq_len, LANES), lambda b, h, i, lam: (b, vcol + h)),
                      pl.BlockSpec((DIFF_V_DIM, 1), lambda b, h, i, lam: (0, 0))],
            out_specs=pl.BlockSpec((tb, DIFF_V_DIM), lambda b, h, i, lam: (b * nq + i, h)),
            scratch_shapes=scratch),
        compiler_params=_cparams("arbitrary", "arbitrary", "arbitrary"),
        name="diff_attention",
    )(lam, attn, attn, attn, subln_g.reshape(DIFF_V_DIM, 1))


def _fox_bias_lane(hh):
    return (1 - hh % 2) * FOX_HEAD_DIM + FOX_BIAS_PIECES * (hh // 2)


def _fox_attn_kernel(q_ref, k_ref, v_ref, kb_ref, o_ref, vt_ref, sa_ref, sb_ref, *scratch, tb, seq_len):
    i = pl.program_id(1)
    pairs = FOX_HEADS // 2

    @pl.when(i == 0)
    def _():
        _transpose_to(vt_ref, v_ref, seq_len)

    row = lax.broadcasted_iota(I32, (LANES, 1), 0)
    lane = lax.broadcasted_iota(I32, (1, LANES), 1)
    qts = []
    for p in range(pairs):
        qt = q_ref[:, p * LANES:(p + 1) * LANES].astype(F32).T
        for r in range(2):
            own = jnp.logical_and(row >= r * FOX_HEAD_DIM, row < (r + 1) * FOX_HEAD_DIM)
            b0 = _fox_bias_lane(2 * p + r)
            ones = jnp.logical_and(row >= b0, row < b0 + FOX_BIAS_PIECES)
            qts.append(jnp.where(own, qt, jnp.where(ones, 1.0, 0.0)).astype(BF16))
    for hh in range(FOX_HEADS):
        _init_softmax_state(*scratch[3 * hh:3 * hh + 3])

    def produce(j, buf):
        start = pl.multiple_of(j * tb, tb)
        kb = kb_ref[pl.ds(start, tb), :]
        for p in range(pairs):
            k = k_ref[pl.ds(start, tb), p * LANES:(p + 1) * LANES]
            for r in range(2):
                own = jnp.logical_and(lane >= r * FOX_HEAD_DIM, lane < (r + 1) * FOX_HEAD_DIM)
                buf[2 * p + r] = jnp.dot(jnp.where(own, k, kb), qts[2 * p + r],
                                         preferred_element_type=F32)

    def consume(j, buf, diagonal=False):
        start = pl.multiple_of(j * tb, tb)
        for hh in range(FOX_HEADS):
            vt = vt_ref[hh * FOX_HEAD_DIM:(hh + 1) * FOX_HEAD_DIM, pl.ds(start, tb)]
            s = buf[hh]
            if diagonal:
                kpos = j * tb + lax.broadcasted_iota(I32, (tb, 1), 0)
                qpos = i * tb + lax.broadcasted_iota(I32, (1, tb), 1)
                s = jnp.where(kpos <= qpos, s, MASK_VALUE)
            _online_softmax_step(s, vt, *scratch[3 * hh:3 * hh + 3])

    _pipelined_blocks(i, sa_ref, sb_ref, produce, produce, consume)

    for p in range(pairs):
        _, l_lo, a_lo = scratch[6 * p:6 * p + 3]
        _, l_hi, a_hi = scratch[6 * p + 3:6 * p + 6]
        o = jnp.concatenate([a_lo[...] / l_lo[...], a_hi[...] / l_hi[...]], axis=0)
        o_ref[:, p * LANES:(p + 1) * LANES] = o.T.astype(o_ref.dtype)


def _fox_attention(attn, key_bias, seq_len, tb):
    t_all = attn.shape[0]
    bsz = t_all // seq_len
    nq = seq_len // tb
    base = (2 * DIFF_QK_WIDTH + DIFF_V_WIDTH) // FOX_WIDTH
    scratch = [pltpu.VMEM((FOX_WIDTH, seq_len), BF16),
               pltpu.VMEM((FOX_HEADS, tb, tb), F32), pltpu.VMEM((FOX_HEADS, tb, tb), F32)]
    for _ in range(FOX_HEADS):
        scratch += [pltpu.VMEM((1, tb), F32), pltpu.VMEM((1, tb), F32), pltpu.VMEM((FOX_HEAD_DIM, tb), F32)]
    return pl.pallas_call(
        functools.partial(_fox_attn_kernel, tb=tb, seq_len=seq_len),
        out_shape=jax.ShapeDtypeStruct((t_all, FOX_WIDTH), BF16),
        grid=(bsz, nq),
        in_specs=[pl.BlockSpec((tb, FOX_WIDTH), lambda b, i: (b * nq + i, base)),
                  pl.BlockSpec((seq_len, FOX_WIDTH), lambda b, i: (b, base + 1)),
                  pl.BlockSpec((seq_len, FOX_WIDTH), lambda b, i: (b, base + 2)),
                  pl.BlockSpec((seq_len, LANES), lambda b, i: (b, 0))],
        out_specs=pl.BlockSpec((tb, FOX_WIDTH), lambda b, i: (b * nq + i, 0)),
        scratch_shapes=scratch,
        compiler_params=_cparams("arbitrary", "arbitrary"),
        name="fox_attention",
    )(attn, attn, attn, key_bias)


def _forget_cumsum_kernel(x_ref, b_ref, kb_ref, carry_ref, *, tb, n_lead):
    i = pl.program_id(1)

    @pl.when(i == 0)
    def _():
        carry_ref[...] = jnp.zeros_like(carry_ref)

    log_f = _log_sigmoid(x_ref[...] + b_ref[...])
    r = lax.broadcasted_iota(I32, (tb, tb), 0)
    c = lax.broadcasted_iota(I32, (tb, tb), 1)
    tri = jnp.where(c <= r, 1.0, 0.0).astype(BF16)
    cum = _dot_exact_lhs(tri, log_f) + carry_ref[...]
    carry_ref[...] = cum[tb - 1:tb, :]

    pos = i * tb + lax.broadcasted_iota(I32, (tb, 1), 0)
    lane = lax.broadcasted_iota(I32, (1, LANES), 1)
    out = jnp.zeros((tb, LANES), F32)
    for hh in range(FOX_HEADS):
        bias = jnp.where(pos >= n_lead, -LOG2_E * cum[:, GATE_FF + hh:GATE_FF + hh + 1], MASK_VALUE)
        pieces = _split3(bias)
        for n, piece in enumerate(pieces):
            out = jnp.where(lane == _fox_bias_lane(hh) + n, piece.astype(F32), out)
    kb_ref[...] = out.astype(BF16)


def _forget_cumsum(gates, forget_bias_lanes, seq_len, n_lead):
    t_all = gates.shape[0]
    bsz = t_all // seq_len
    tb = LANES
    nb = seq_len // tb
    return pl.pallas_call(
        functools.partial(_forget_cumsum_kernel, tb=tb, n_lead=n_lead),
        out_shape=jax.ShapeDtypeStruct((t_all, LANES), BF16),
        grid=(bsz, nb),
        in_specs=[pl.BlockSpec((tb, LANES), lambda b, i: (b * nb + i, 0)),
                  pl.BlockSpec((1, LANES), lambda b, i: (0, 0))],
        out_specs=pl.BlockSpec((tb, LANES), lambda b, i: (b * nb + i, 0)),
        scratch_shapes=[pltpu.VMEM((1, LANES), F32)],
        compiler_params=_cparams("arbitrary", "arbitrary"),
        name="forget_cumsum",
    )(gates, forget_bias_lanes)


def _gdn_kernel(x_ref, gate_ref, cw_ref, alog_ref, dtb_ref, ng_ref, o_ref,
                halo_ref, xs_ref, state_ref, *, rows):
    cg = pl.program_id(1)
    g_chunks = rows // CHUNK
    hd = GDN_HEAD_DIM

    @pl.when(cg == 0)
    def _():
        halo_ref[...] = jnp.zeros_like(halo_ref)
        state_ref[...] = jnp.zeros_like(state_ref)

    xin = x_ref[:, :GDN_CONV_CH]
    xs_ref[0:SUBLANES, :] = halo_ref[...]
    xs_ref[SUBLANES:SUBLANES + rows, :] = xin
    halo_ref[...] = xin[rows - SUBLANES:rows, :]
    y = jnp.zeros((rows, GDN_CONV_CH), F32)
    for tap in range(CONV_K):
        back = CONV_K - 1 - tap
        y = y + cw_ref[tap:tap + 1, :] * xs_ref[SUBLANES - back:SUBLANES - back + rows, :]
    y = y * _sigmoid(y)

    gates = gate_ref[...]
    beta_all = _sigmoid(gates)
    g_all = -jnp.exp(alog_ref[...]) * _softplus(gates + dtb_ref[...])
    rowi = lax.broadcasted_iota(I32, (rows, 1), 0) & (CHUNK - 1)
    gc_all = g_all
    shift = 1
    while shift < CHUNK:
        gc_all = gc_all + jnp.where(rowi >= shift, pltpu.roll(gc_all, shift, 0), 0.0)
        shift *= 2

    ri = lax.broadcasted_iota(I32, (CHUNK, CHUNK), 0)
    ci = lax.broadcasted_iota(I32, (CHUNK, CHUNK), 1)
    eye = jnp.where(ri == ci, 1.0, 0.0).astype(F32)
    causal = (ci <= ri)[None]
    strict = (ci < ri)[None]
    eye_b = jnp.broadcast_to(eye[None], (g_chunks, CHUNK, CHUNK))

    prep = []
    for h in range(GDN_HEADS):
        def head3(base):
            return y[:, base + h * hd:base + (h + 1) * hd]
        q = head3(0)
        k = head3(GDN_WIDTH)
        v = head3(2 * GDN_WIDTH)
        q = q * lax.rsqrt(jnp.sum(q * q, axis=-1, keepdims=True) + 1e-6) * (hd ** -0.5)
        k = k * lax.rsqrt(jnp.sum(k * k, axis=-1, keepdims=True) + 1e-6)
        beta = beta_all[:, GATE_BETA + h:GATE_BETA + h + 1]
        gcol = jnp.broadcast_to(gc_all[:, GATE_DECAY + h:GATE_DECAY + h + 1], (rows, hd))
        egc = jnp.exp(gcol)

        q3 = q.reshape(g_chunks, CHUNK, hd)
        k3 = k.reshape(g_chunks, CHUNK, hd)
        gcol3 = gcol.reshape(g_chunks, CHUNK, CHUNK)
        beta3 = jnp.broadcast_to(beta, (rows, hd)).reshape(g_chunks, CHUNK, hd)
        grow3 = jnp.swapaxes(gcol3, 1, 2)
        decay = jnp.exp(jnp.where(causal, gcol3 - grow3, MASK_VALUE))
        a = jnp.where(strict, _bdot_nt(k3, k3) * beta3 * decay, 0.0)

        x = -a
        tinv = eye_b + x
        span = 2
        while span < CHUNK:
            x = _bdot(x, x)
            tinv = tinv + _bdot(tinv, x)
            span *= 2

        vb = (v * beta).reshape(g_chunks, CHUNK, hd)
        kb = (k * beta * egc).reshape(g_chunks, CHUNK, hd)
        u3 = _bdot(tinv, vb)
        w3 = _bdot(tinv, kb)
        qk3 = _bdot_nt(q3, k3) * decay
        qd3 = (q * egc).reshape(g_chunks, CHUNK, hd)
        glast3 = gcol3[:, CHUNK - 1:CHUNK, :]
        kdt3 = jnp.swapaxes(k3 * jnp.exp(glast3 - gcol3), 1, 2)
        cd3 = jnp.exp(glast3)
        prep.append((_bdot(kdt3, w3), _bdot(kdt3, u3), qd3 - _bdot(qk3, w3), _bdot(qk3, u3), cd3))

    states = [state_ref[h] for h in range(GDN_HEADS)]
    outs = [[] for _ in range(GDN_HEADS)]
    for c in range(g_chunks):
        for h in range(GDN_HEADS):
            kw3, ku3, qp3, o03, cd3 = prep[h]
            state = states[h]
            outs[h].append(_dot(qp3[c], state) + o03[c])
            states[h] = state * cd3[c] + (ku3[c] - _dot(kw3[c], state))

    for h in range(GDN_HEADS):
        state_ref[h] = states[h]
        o = jnp.concatenate(outs[h], axis=0)
        o = o * lax.rsqrt(jnp.mean(o * o, axis=-1, keepdims=True) + 1e-6) * ng_ref[...]
        z = x_ref[:, GDN_CONV_CH + h * hd:GDN_CONV_CH + (h + 1) * hd]
        o_ref[:, h * hd:(h + 1) * hd] = (o * (z * _sigmoid(z))).astype(o_ref.dtype)


def _gated_deltanet(gdn_slab, gates, conv_w, a_log_lanes, dt_bias_lanes, norm_g, seq_len, rows):
    t_all = gdn_slab.shape[0]
    bsz = t_all // seq_len
    nb = seq_len // rows
    row = lambda b, i: (b * nb + i, 0)
    const = lambda b, i: (0, 0)
    return pl.pallas_call(
        functools.partial(_gdn_kernel, rows=rows),
        out_shape=jax.ShapeDtypeStruct((t_all, GDN_WIDTH), BF16),
        grid=(bsz, nb),
        in_specs=[pl.BlockSpec((rows, GDN_SLAB), row),
                  pl.BlockSpec((rows, LANES), row),
                  pl.BlockSpec((SUBLANES, GDN_CONV_CH), const),
                  pl.BlockSpec((1, LANES), const),
                  pl.BlockSpec((1, LANES), const),
                  pl.BlockSpec((1, GDN_HEAD_DIM), const)],
        out_specs=pl.BlockSpec((rows, GDN_WIDTH), row),
        scratch_shapes=[pltpu.VMEM((SUBLANES, GDN_CONV_CH), F32),
                        pltpu.VMEM((rows + SUBLANES, GDN_CONV_CH), F32),
                        pltpu.VMEM((GDN_HEADS, GDN_HEAD_DIM, GDN_HEAD_DIM), F32)],
        compiler_params=_cparams("arbitrary", "arbitrary"),
        name="gated_deltanet",
    )(gdn_slab, gates, conv_w, a_log_lanes, dt_bias_lanes, norm_g.reshape(1, GDN_HEAD_DIM))


def _mix_router_kernel(x_ref, yd_ref, yf_ref, yg_ref, wd_ref, wf_ref, wg_ref, g_ref, b_ref,
                       rw_ref, rb_ref, x1_ref, lr_ref, lrt_ref, ge_ref, cnt_ref, *, tm, alpha):
    mix = (jnp.dot(yd_ref[...], wd_ref[...], preferred_element_type=F32)
           + jnp.dot(yf_ref[...], wf_ref[...], preferred_element_type=F32)
           + jnp.dot(yg_ref[...], wg_ref[...], preferred_element_type=F32))
    x1 = _layer_norm(alpha * x_ref[...] + mix, g_ref[...], b_ref[...])
    x1_ref[...] = x1

    lane = lax.broadcasted_iota(I32, (tm, LANES), 1)
    lane_f = lane.astype(F32)
    logits = _dot3(x1, rw_ref[...]) + rb_ref[...]
    vals = jnp.where(lane < N_EXPERTS, logits, -jnp.inf)
    hits, tops = [], []
    for _ in range(TOP_K):
        top = jnp.max(vals, axis=-1, keepdims=True)
        idx = jnp.min(jnp.where(vals == top, lane_f, float(LANES)), axis=-1, keepdims=True)
        hit = lane_f == idx
        vals = jnp.where(hit, -jnp.inf, vals)
        hits.append(hit)
        tops.append(top)
    exps = [jnp.exp(t - tops[0]) for t in tops]
    denom = exps[0] + exps[1] + exps[2] + exps[3]
    onehot = jnp.zeros((tm, LANES), F32)
    gate_by_expert = jnp.zeros((tm, LANES), F32)
    for kk in range(TOP_K):
        onehot = jnp.where(hits[kk], 1.0, onehot)
        gate_by_expert = jnp.where(hits[kk], exps[kk] / denom, gate_by_expert)

    r = lax.broadcasted_iota(I32, (tm, tm), 0)
    c = lax.broadcasted_iota(I32, (tm, tm), 1)
    tri = jnp.where(c < r, 1.0, 0.0).astype(BF16)
    rank = jnp.dot(tri, onehot.astype(BF16), preferred_element_type=F32)
    local = jnp.where(onehot > 0.0, rank, -1.0)
    lr_ref[...] = local
    lrt_ref[...] = local.T[:N_EXPERTS, :]
    ge_ref[...] = gate_by_expert
    cnt_ref[0] = jnp.sum(onehot, axis=0, keepdims=True)


def _mix_router(stream, yd, yf, yg, w_out_b, ln_g, ln_b, rw, rb, alpha, tm):
    t_all, d = stream.shape
    row = lambda i: (i, 0)
    const = lambda i: (0, 0)
    wd = w_out_b[:DIFF_V_WIDTH]
    wf = w_out_b[DIFF_V_WIDTH:DIFF_V_WIDTH + FOX_WIDTH]
    wg = w_out_b[DIFF_V_WIDTH + FOX_WIDTH:]
    return pl.pallas_call(
        functools.partial(_mix_router_kernel, tm=tm, alpha=alpha),
        out_shape=(jax.ShapeDtypeStruct((t_all, d), F32),
                   jax.ShapeDtypeStruct((t_all, LANES), F32),
                   jax.ShapeDtypeStruct((t_all // tm * N_EXPERTS, tm), F32),
                   jax.ShapeDtypeStruct((t_all, LANES), F32),
                   jax.ShapeDtypeStruct((t_all // tm, 1, LANES), F32)),
        grid=(t_all // tm,),
        in_specs=[pl.BlockSpec((tm, d), row),
                  pl.BlockSpec((tm, DIFF_V_WIDTH), row),
                  pl.BlockSpec((tm, FOX_WIDTH), row),
                  pl.BlockSpec((tm, GDN_WIDTH), row),
                  pl.BlockSpec((DIFF_V_WIDTH, d), const),
                  pl.BlockSpec((FOX_WIDTH, d), const),
                  pl.BlockSpec((GDN_WIDTH, d), const),
                  pl.BlockSpec((1, d), const),
                  pl.BlockSpec((1, d), const),
                  pl.BlockSpec((d, LANES), const),
                  pl.BlockSpec((1, LANES), const)],
        out_specs=(pl.BlockSpec((tm, d), row),
                   pl.BlockSpec((tm, LANES), row),
                   pl.BlockSpec((N_EXPERTS, tm), row),
                   pl.BlockSpec((tm, LANES), row),
                   pl.BlockSpec((1, 1, LANES), lambda i: (i, 0, 0))),
        compiler_params=_cparams("arbitrary"),
        name="mix_router",
    )(stream, yd, yf, yg, wd, wf, wg, ln_g.reshape(1, d), ln_b.reshape(1, d), rw, rb)


SLAB = 64
GROUP_ALIGN = 8
ZERO_FILL_SIZES = (256, 128, 64, 32, 16, 8)


def _slab_rows(src_ref, rank_row0):
    want = (rank_row0 + lax.broadcasted_iota(I32, (SLAB, 1), 0)).astype(F32)
    return jnp.concatenate([jnp.where(src_ref[e:e + 1, :] == want, 1.0, 0.0).astype(BF16)
                            for e in range(N_EXPERTS)], axis=0)


def _dispatch_kernel(gstart_ref, gcnt_ref, npass_ref, zstart_ref, zrows_ref, x_ref, lrt_ref, xs_ref,
                     zbuf, zero_ref, sem, zsem, *, nb, n_slots):
    b = pl.program_id(0)
    slot = b % 2

    def slab_copy(bb, c, e, s):
        start = pl.multiple_of(gstart_ref[bb * N_EXPERTS + e] + c * SLAB, GROUP_ALIGN)
        return pltpu.make_async_copy(zbuf.at[s, pl.ds(e * SLAB, SLAB)], xs_ref.at[pl.ds(start, SLAB)], sem.at[s])

    def zero_fill(act):
        for e in range(N_EXPERTS):
            rows = zrows_ref[e]
            for size in ZERO_FILL_SIZES:
                @pl.when((rows & size) != 0)
                def _():
                    start = pl.multiple_of(zstart_ref[e] + (rows & ~(2 * size - 1)), GROUP_ALIGN)
                    act(pltpu.make_async_copy(zero_ref.at[pl.ds(0, size)], xs_ref.at[pl.ds(start, size)], zsem))

    def zero_tail(act):
        first = zstart_ref[N_EXPERTS - 1] + zrows_ref[N_EXPERTS - 1]
        size = ZERO_FILL_SIZES[0]

        def body(t, carry):
            start = pl.multiple_of(first + t * size, GROUP_ALIGN)
            act(pltpu.make_async_copy(zero_ref, xs_ref.at[pl.ds(start, size)], zsem))
            return carry

        lax.fori_loop(0, (n_slots - first) // size, body, 0)

    @pl.when(b == 0)
    def _():
        zero_ref[...] = jnp.zeros_like(zero_ref)
        for fill in (zero_fill, zero_tail):
            fill(lambda cp: cp.start())
        for fill in (zero_fill, zero_tail):
            fill(lambda cp: cp.wait())

    xb = x_ref[...].astype(BF16)
    zbuf[slot] = jnp.dot(_slab_rows(lrt_ref, 0), xb, preferred_element_type=F32)

    @pl.when(b > 0)
    def _():
        @pl.when(npass_ref[b - 1] <= 1)
        def _():
            for e in range(N_EXPERTS):
                slab_copy(b - 1, 0, e, 1 - slot).wait()

    for e in range(N_EXPERTS):
        slab_copy(b, 0, e, slot).start()

    @pl.when(npass_ref[b] > 1)
    def _():
        for e in range(N_EXPERTS):
            slab_copy(b, 0, e, slot).wait()

        def extra_pass(c, carry):
            zbuf[slot] = jnp.dot(_slab_rows(lrt_ref, c * SLAB), xb, preferred_element_type=F32)
            for action in ("start", "wait"):
                for e in range(N_EXPERTS):
                    @pl.when(gcnt_ref[b * N_EXPERTS + e] > c * SLAB)
                    def _():
                        getattr(slab_copy(b, c, e, slot), action)()
            return carry

        lax.fori_loop(1, npass_ref[b], extra_pass, 0)

    @pl.when(b == nb - 1)
    def _():
        @pl.when(npass_ref[b] <= 1)
        def _():
            for e in range(N_EXPERTS):
                slab_copy(b, 0, e, slot).wait()


def _dispatch(tables, x1, lrt, n_slots, tm):
    t_all, d = x1.shape
    nb = t_all // tm
    ntab = len(tables)
    return pl.pallas_call(
        functools.partial(_dispatch_kernel, nb=nb, n_slots=n_slots),
        out_shape=jax.ShapeDtypeStruct((n_slots, d), F32),
        grid_spec=pltpu.PrefetchScalarGridSpec(
            num_scalar_prefetch=ntab,
            grid=(nb,),
            in_specs=[pl.BlockSpec((tm, d), lambda i, *_: (i, 0)),
                      pl.BlockSpec((N_EXPERTS, tm), lambda i, *_: (i, 0))],
            out_specs=pl.BlockSpec(memory_space=pl.ANY),
            scratch_shapes=[pltpu.VMEM((2, N_EXPERTS * SLAB, d), F32),
                            pltpu.VMEM((ZERO_FILL_SIZES[0], d), F32),
                            pltpu.SemaphoreType.DMA((2,)),
                            pltpu.SemaphoreType.DMA]),
        compiler_params=_cparams("arbitrary"),
        name="moe_dispatch",
    )(*tables, x1, lrt)


def _expert_kernel(be_ref, na_ref, x_ref, w1_ref, b1_ref, w2_ref, b2_ref, o_ref, w1b_ref, w2b_ref, *, d_expert):
    i = pl.program_id(0)
    active = i < na_ref[0]
    new_expert = jnp.logical_or(i == 0, be_ref[i] != be_ref[jnp.maximum(i - 1, 0)])

    @pl.when(jnp.logical_and(active, new_expert))
    def _():
        w1b_ref[...] = w1_ref[0].astype(BF16)
        w2b_ref[...] = w2_ref[0].astype(BF16)

    @pl.when(active)
    def _():
        hid = jnp.dot(x_ref[...].astype(BF16), w1b_ref[...], preferred_element_type=F32) + b1_ref[0]
        glu = jnp.minimum(hid[:, :d_expert], SWIGLU_LIMIT)
        lin = jnp.clip(hid[:, d_expert:], -SWIGLU_LIMIT, SWIGLU_LIMIT)
        act = glu * _sigmoid(SWIGLU_ALPHA * glu) * (lin + 1.0)
        o_ref[...] = jnp.dot(act.astype(BF16), w2b_ref[...], preferred_element_type=F32) + b2_ref[0]

    @pl.when(jnp.logical_not(active))
    def _():
        o_ref[...] = jnp.zeros_like(o_ref)


def _experts(block_expert, n_active, xs, w1_all, b1, w2_all, b2, layer, mb):
    n_slots, d = xs.shape
    depth, n_exp, _, f2 = w1_all.shape
    f = f2 // 2
    nb = n_slots // mb
    w1 = w1_all.reshape(depth * n_exp, d, f2)
    w2 = w2_all.reshape(depth * n_exp, f, d)
    block_expert = block_expert + layer * n_exp
    return pl.pallas_call(
        functools.partial(_expert_kernel, d_expert=f),
        out_shape=jax.ShapeDtypeStruct((n_slots, d), F32),
        grid_spec=pltpu.PrefetchScalarGridSpec(
            num_scalar_prefetch=2,
            grid=(nb,),
            in_specs=[pl.BlockSpec((mb, d), lambda i, be, na: (jnp.minimum(i, na[0] - 1), 0)),
                      pl.BlockSpec((1, d, f2), lambda i, be, na: (be[i], 0, 0)),
                      pl.BlockSpec((1, 1, f2), lambda i, be, na: (be[i], 0, 0)),
                      pl.BlockSpec((1, f, d), lambda i, be, na: (be[i], 0, 0)),
                      pl.BlockSpec((1, 1, d), lambda i, be, na: (be[i], 0, 0))],
            out_specs=pl.BlockSpec((mb, d), lambda i, be, na: (i, 0)),
            scratch_shapes=[pltpu.VMEM((d, f2), BF16), pltpu.VMEM((f, d), BF16)]),
        compiler_params=_cparams("arbitrary"),
        name="moe_experts",
    )(block_expert, n_active, xs, w1, b1.reshape(depth * n_exp, 1, f2), w2, b2.reshape(depth * n_exp, 1, d))


def _combine_kernel(gstart_ref, gcnt_ref, npass_ref, x1_ref, lr_ref, ge_ref, g_ref, b_ref, ys_ref, o_ref,
                    ybuf, acc_ref, sem, *, tm, nb, alpha):
    b = pl.program_id(0)
    slot = b % 2

    def slab_copy(bb, c, e, s):
        start = pl.multiple_of(gstart_ref[bb * N_EXPERTS + e] + c * SLAB, GROUP_ALIGN)
        return pltpu.make_async_copy(ys_ref.at[pl.ds(start, SLAB)], ybuf.at[s, pl.ds(e * SLAB, SLAB)], sem.at[s])

    @pl.when(b == 0)
    def _():
        for e in range(N_EXPERTS):
            slab_copy(0, 0, e, 0).start()

    @pl.when(b + 1 < nb)
    def _():
        for e in range(N_EXPERTS):
            slab_copy(b + 1, 0, e, 1 - slot).start()

    for e in range(N_EXPERTS):
        slab_copy(b, 0, e, slot).wait()

    lane = lax.broadcasted_iota(I32, (1, LANES), 1)
    low = lane < SLAB
    lr = lr_ref[...]
    ge = ge_ref[...]

    def weighted_sum(rank0):
        want = ((lane & (SLAB - 1)) + rank0).astype(F32)
        his, los = [], []
        for e in range(0, N_EXPERTS, 2):
            rank = jnp.where(low, lr[:, e:e + 1], lr[:, e + 1:e + 2])
            gate = jnp.where(low, ge[:, e:e + 1], ge[:, e + 1:e + 2])
            w = jnp.where(rank == want, gate, 0.0)
            hi = w.astype(BF16)
            his.append(hi)
            los.append((w - hi.astype(F32)).astype(BF16))
        sel = jnp.concatenate([jnp.concatenate(his, axis=1), jnp.concatenate(los, axis=1)], axis=0)
        r = jnp.dot(sel, ybuf[slot].astype(BF16), preferred_element_type=F32)
        return r[:tm] + r[tm:]

    acc_ref[...] = alpha * x1_ref[...] + weighted_sum(0)

    @pl.when(npass_ref[b] > 1)
    def _():
        def extra_pass(c, carry):
            for action in ("start", "wait"):
                for e in range(N_EXPERTS):
                    @pl.when(gcnt_ref[b * N_EXPERTS + e] > c * SLAB)
                    def _():
                        getattr(slab_copy(b, c, e, slot), action)()
            acc_ref[...] += weighted_sum(c * SLAB)
            return carry

        lax.fori_loop(1, npass_ref[b], extra_pass, 0)

    o_ref[...] = _layer_norm(acc_ref[...], g_ref[...], b_ref[...])


def _combine(tables, x1, lr, ge, ln_g, ln_b, ys, alpha, tm):
    t_all, d = x1.shape
    nb = t_all // tm
    ntab = len(tables)
    row = lambda i, *_: (i, 0)
    const = lambda i, *_: (0, 0)
    return pl.pallas_call(
        functools.partial(_combine_kernel, tm=tm, nb=nb, alpha=alpha),
        out_shape=jax.ShapeDtypeStruct((t_all, d), F32),
        grid_spec=pltpu.PrefetchScalarGridSpec(
            num_scalar_prefetch=ntab,
            grid=(nb,),
            in_specs=[pl.BlockSpec((tm, d), row),
                      pl.BlockSpec((tm, LANES), row),
                      pl.BlockSpec((tm, LANES), row),
                      pl.BlockSpec((1, d), const),
                      pl.BlockSpec((1, d), const),
                      pl.BlockSpec(memory_space=pl.ANY)],
            out_specs=pl.BlockSpec((tm, d), row),
            scratch_shapes=[pltpu.VMEM((2, N_EXPERTS * SLAB, d), F32),
                            pltpu.VMEM((tm, d), F32),
                            pltpu.SemaphoreType.DMA((2,))]),
        compiler_params=_cparams("arbitrary"),
        name="moe_combine",
    )(*tables, x1, lr, ge, ln_g.reshape(1, d), ln_b.reshape(1, d), ys)


def _lanes(values, offset):
    return jnp.zeros((1, LANES), F32).at[0, offset:offset + values.shape[0]].set(values.astype(F32))


def _split_in_proj(w_in_layer):
    offs = np.concatenate([[0], np.cumsum(IN_PROJ_SIZES)])
    seg = lambda n: w_in_layer[:, offs[n]:offs[n + 1]]
    qk_scale = DIFF_QK_DIM ** -0.5 * LOG2_E
    fox_scale = FOX_HEAD_DIM ** -0.5 * LOG2_E
    wa = jnp.concatenate([seg(0) * qk_scale, seg(1), seg(2), seg(3) * fox_scale, seg(4), seg(5)], axis=1)
    wg = jnp.concatenate([seg(7), seg(10)], axis=1)
    d = w_in_layer.shape[0]
    ws = jnp.zeros((d, LANES), F32)
    ws = ws.at[:, GATE_FF:GATE_FF + FOX_HEADS].set(seg(6))
    ws = ws.at[:, GATE_BETA:GATE_BETA + GDN_HEADS].set(seg(8))
    ws = ws.at[:, GATE_DECAY:GATE_DECAY + GDN_HEADS].set(seg(9))
    return wa.astype(BF16), wg.astype(BF16), ws


def _diff_lambda_init(layer):
    return 0.8 - 0.6 * math.exp(-0.3 * layer)


def _pick_block(seq_len, candidates):
    for c in candidates:
        if seq_len % c == 0:
            return c
    raise ValueError(f"unsupported sequence length {seq_len}")


MOE_ROWS = 256
TOKEN_BLOCK = 256


def _layer(stream, layer, seq_len, n_lead, alpha, p):
    t_all, d = stream.shape
    attn_tb = _pick_block(seq_len, (640, 512, 256, 128))
    wa, wg, ws = _split_in_proj(p["w_in"])
    attn, gdn_slab, gates = _in_proj(stream, wa, wg, ws, seq_len, n_lead, attn_tb)

    lv = p["diff_lambda"].astype(F32)
    lambda_init = _diff_lambda_init(layer)
    lam = (jnp.exp(jnp.sum(lv[0] * lv[1])) - jnp.exp(jnp.sum(lv[2] * lv[3])) + lambda_init).reshape(1)
    y_diff = _diff_attention(attn, lam, p["diff_subln_g"], seq_len, n_lead, lambda_init, attn_tb)

    key_bias = _forget_cumsum(gates, _lanes(p["fox_forget_b"], GATE_FF), seq_len, n_lead)
    y_fox = _fox_attention(attn, key_bias, seq_len, attn_tb)

    conv_w = jnp.zeros((SUBLANES, GDN_CONV_CH), F32).at[:CONV_K].set(p["gdn_conv_w"])
    y_gdn = _gated_deltanet(gdn_slab, gates, conv_w, _lanes(p["gdn_a_log"], GATE_DECAY),
                            _lanes(p["gdn_dt_bias"], GATE_DECAY), p["gdn_norm_g"], seq_len, attn_tb)

    rw = jnp.zeros((d, LANES), F32).at[:, :N_EXPERTS].set(p["router_w"])
    rb = _lanes(p["router_b"], 0)
    tm = TOKEN_BLOCK
    x1, lr, lrt, ge, counts = _mix_router(stream, y_diff, y_fox, y_gdn, p["w_out"].astype(BF16),
                                          p["ln1_g"], p["ln1_b"], rw, rb, alpha, tm)

    mb = MOE_ROWS
    nb = t_all // tm
    n_slots = -(-(t_all * TOP_K + (GROUP_ALIGN - 1) * nb * N_EXPERTS
                  + N_EXPERTS * (SLAB + mb - GROUP_ALIGN)) // mb) * mb
    n_blocks = n_slots // mb
    cnt = counts[:, 0, :N_EXPERTS].astype(I32)
    rows = (cnt + GROUP_ALIGN - 1) // GROUP_ALIGN * GROUP_ALIGN
    total = jnp.sum(rows, axis=0)
    region = (total + SLAB + mb - 1) // mb * mb
    pad_end = jnp.cumsum(region)
    pad_start = pad_end - region
    gstart = (pad_start[None, :] + jnp.cumsum(rows, axis=0) - rows).reshape(-1)
    npass = jnp.maximum((jnp.max(cnt, axis=1) + SLAB - 1) // SLAB, 1)
    block_first = jnp.arange(n_blocks, dtype=I32) * mb
    block_expert = jnp.minimum(jnp.sum((pad_end[None, :] <= block_first[:, None]).astype(I32), axis=1),
                               N_EXPERTS - 1)
    n_active = (pad_end[-1] // mb).astype(I32).reshape(1)
    group_tables = (gstart.astype(I32), cnt.reshape(-1), npass.astype(I32))

    xs = _dispatch(group_tables + ((pad_start + total).astype(I32), (region - total).astype(I32)),
                   x1, lrt, n_slots, tm)
    ys = _experts(block_expert, n_active, xs, p["expert_w1"], p["expert_b1"], p["expert_w2"], p["expert_b2"],
                  layer, mb)
    return _combine(group_tables, x1, lr, ge, p["ln2_g"], p["ln2_b"], ys, alpha, tm)


def kernel(x, meta_tokens, ln_in_g, ln_in_b, w_in, diff_lambda, diff_subln_g, fox_forget_b, gdn_conv_w,
           gdn_a_log, gdn_dt_bias, gdn_norm_g, w_out, ln1_g, ln1_b, router_w, router_b, expert_w1,
           expert_b1, expert_w2, expert_b2, ln2_g, ln2_b):
    bsz, s, d = x.shape
    depth = w_in.shape[0]
    n_lead = (-(s + N_META)) % Q_BLOCK
    if n_lead + N_META != Q_BLOCK:
        raise ValueError("sequence length must be a multiple of the query block")
    seq_len = n_lead + N_META + s
    alpha = (2 * depth) ** 0.25
    stream = _ln_in(x, meta_tokens.astype(x.dtype), ln_in_g, ln_in_b, n_lead).reshape(bsz * seq_len, d)
    per_layer = dict(w_in=w_in, diff_lambda=diff_lambda, diff_subln_g=diff_subln_g, fox_forget_b=fox_forget_b,
                     gdn_conv_w=gdn_conv_w, gdn_a_log=gdn_a_log, gdn_dt_bias=gdn_dt_bias,
                     gdn_norm_g=gdn_norm_g, w_out=w_out, ln1_g=ln1_g, ln1_b=ln1_b, router_w=router_w,
                     router_b=router_b, expert_w1=expert_w1, expert_b1=expert_b1, expert_w2=expert_w2,
                     expert_b2=expert_b2, ln2_g=ln2_g, ln2_b=ln2_b)
    stacked = ("expert_w1", "expert_b1", "expert_w2", "expert_b2")
    for layer in range(depth):
        p = {k: (v if k in stacked else v[layer]) for k, v in per_layer.items()}
        stream = _layer(stream, layer, seq_len, n_lead, alpha, p)
    return stream.reshape(bsz, seq_len, d)[:, n_lead + N_META:]
```

```python
import functools
import math

import numpy as np
import jax
import jax.numpy as jnp
from jax import lax
from jax.experimental import pallas as pl
from jax.experimental.pallas import tpu as pltpu

F32 = jnp.float32
BF16 = jnp.bfloat16
I32 = jnp.int32

CHUNK = 64
CHUNK_SHIFT = 6
Q_BLOCK = 128
N_META = 16
DIFF_HEADS = 4
DIFF_QK_DIM = 64
DIFF_V_DIM = 128
FOX_HEADS = 4
FOX_HEAD_DIM = 64
GDN_HEADS = 4
GDN_HEAD_DIM = 64
CONV_K = 4
N_EXPERTS = 32
TOP_K = 4
SWIGLU_LIMIT = 7.0
SWIGLU_ALPHA = 1.702
MASK_VALUE = -1e30
LOG2_E = 1.4426950408889634
FOX_BIAS_PIECES = 3

DIFF_QK_WIDTH = DIFF_HEADS * 2 * DIFF_QK_DIM
DIFF_V_WIDTH = DIFF_HEADS * DIFF_V_DIM
FOX_WIDTH = FOX_HEADS * FOX_HEAD_DIM
GDN_WIDTH = GDN_HEADS * GDN_HEAD_DIM
GDN_CONV_CH = 3 * GDN_WIDTH
IN_PROJ_SIZES = (DIFF_QK_WIDTH, DIFF_QK_WIDTH, DIFF_V_WIDTH, FOX_WIDTH, FOX_WIDTH, FOX_WIDTH, FOX_HEADS,
                 GDN_CONV_CH, GDN_HEADS, GDN_HEADS, GDN_WIDTH)

LANES = 128
SUBLANES = 8
VMEM_LIMIT = 56 * 1024 * 1024

GATE_FF = 0
GATE_BETA = 4
GATE_DECAY = 8

ATTN_WIDTH = 2 * DIFF_QK_WIDTH + DIFF_V_WIDTH + 3 * FOX_WIDTH
GDN_SLAB = GDN_CONV_CH + GDN_WIDTH


def _cparams(*sem):
    return pltpu.CompilerParams(dimension_semantics=tuple(sem), vmem_limit_bytes=VMEM_LIMIT)


def _dot(a, b):
    return jnp.dot(a.astype(BF16), b.astype(BF16), preferred_element_type=F32)


def _dot_nt(a, b):
    return lax.dot_general(a.astype(BF16), b.astype(BF16), (((1,), (1,)), ((), ())),
                           preferred_element_type=F32)


def _split2(x):
    hi = x.astype(BF16)
    lo = (x - hi.astype(F32)).astype(BF16)
    return hi, lo


def _split3(x):
    hi = x.astype(BF16)
    r = x - hi.astype(F32)
    mid = r.astype(BF16)
    lo = (r - mid.astype(F32)).astype(BF16)
    return hi, mid, lo


def _dot3(a, b):
    ah, al = _split2(a)
    bh, bl = _split2(b)
    return (jnp.dot(ah, bh, preferred_element_type=F32) + jnp.dot(ah, bl, preferred_element_type=F32)
            + jnp.dot(al, bh, preferred_element_type=F32))


def _dot_exact_lhs(a01, b):
    a = a01.astype(BF16)
    hi, mid, lo = _split3(b)
    return (jnp.dot(a, hi, preferred_element_type=F32) + jnp.dot(a, mid, preferred_element_type=F32)
            + jnp.dot(a, lo, preferred_element_type=F32))


_BDIMS = (((2,), (1,)), ((0,), (0,)))
_BDIMS_NT = (((2,), (2,)), ((0,), (0,)))


def _bdot(a, b):
    return lax.dot_general(a.astype(BF16), b.astype(BF16), _BDIMS, preferred_element_type=F32)


def _bdot_nt(a, b):
    return lax.dot_general(a.astype(BF16), b.astype(BF16), _BDIMS_NT, preferred_element_type=F32)


def _bdot3(a, b):
    ah, al = _split2(a)
    bh, bl = _split2(b)
    return (lax.dot_general(ah, bh, _BDIMS, preferred_element_type=F32)
            + lax.dot_general(ah, bl, _BDIMS, preferred_element_type=F32)
            + lax.dot_general(al, bh, _BDIMS, preferred_element_type=F32))


def _bdot_exact_lhs(a01, b):
    a = a01.astype(BF16)
    hi, mid, lo = _split3(b)
    return (lax.dot_general(a, hi, _BDIMS, preferred_element_type=F32)
            + lax.dot_general(a, mid, _BDIMS, preferred_element_type=F32)
            + lax.dot_general(a, lo, _BDIMS, preferred_element_type=F32))


def _layer_norm(x, g, b, eps=1e-5):
    mu = jnp.mean(x, axis=-1, keepdims=True)
    xc = x - mu
    var = jnp.mean(xc * xc, axis=-1, keepdims=True)
    return xc * lax.rsqrt(var + eps) * g + b


def _sigmoid(x):
    return 1.0 / (1.0 + jnp.exp(-x))


def _softplus(x):
    return jnp.maximum(x, 0.0) + jnp.log1p(jnp.exp(-jnp.abs(x)))


def _log_sigmoid(x):
    return -_softplus(-x)


def _ln_in_kernel(x_ref, meta_ref, g_ref, b_ref, o_ref, *, n_lead):
    i = pl.program_id(1)
    g = g_ref[...]
    b = b_ref[...]

    @pl.when(i == 0)
    def _():
        o_ref[0, :n_lead, :] = jnp.broadcast_to(b, (n_lead, b.shape[-1]))
        o_ref[0, n_lead:, :] = _layer_norm(meta_ref[...], g, b)

    @pl.when(i > 0)
    def _():
        o_ref[0] = _layer_norm(x_ref[0], g, b)


def _ln_in(x, meta, g, b, n_lead):
    bsz, s, d = x.shape
    l = n_lead + N_META + s
    blk = n_lead + N_META
    return pl.pallas_call(
        functools.partial(_ln_in_kernel, n_lead=n_lead),
        out_shape=jax.ShapeDtypeStruct((bsz, l, d), F32),
        grid=(bsz, l // blk),
        in_specs=[pl.BlockSpec((1, blk, d), lambda bb, i: (bb, jnp.maximum(i - 1, 0), 0)),
                  pl.BlockSpec((N_META, d), lambda bb, i: (0, 0)),
                  pl.BlockSpec((1, d), lambda bb, i: (0, 0)),
                  pl.BlockSpec((1, d), lambda bb, i: (0, 0))],
        out_specs=pl.BlockSpec((1, blk, d), lambda bb, i: (bb, i, 0)),
        compiler_params=_cparams("arbitrary", "arbitrary"),
        name="ln_in",
    )(x, meta, g.reshape(1, d), b.reshape(1, d))


PROJ_COL_CHUNK = 768


def _proj_kernel(x_ref, wa_ref, wg_ref, ws_ref, fb_ref, oa_ref, og_ref, os_ref, kb_ref, carry_ref, *, tm, n_lead):
    i = pl.program_id(1)
    pos = i * tm + lax.broadcasted_iota(I32, (tm, 1), 0)
    x = jnp.where(pos >= n_lead, x_ref[...], 0.0)
    xb = x.astype(BF16)
    for c in range(0, ATTN_WIDTH, PROJ_COL_CHUNK):
        oa_ref[:, c:c + PROJ_COL_CHUNK] = jnp.dot(
            xb, wa_ref[:, c:c + PROJ_COL_CHUNK], preferred_element_type=F32).astype(BF16)
    og_ref[...] = jnp.dot(xb, wg_ref[...], preferred_element_type=F32)
    gates = _dot3(x, ws_ref[...])
    os_ref[...] = gates

    @pl.when(i == 0)
    def _():
        carry_ref[...] = jnp.zeros_like(carry_ref)

    log_f = _log_sigmoid(gates + fb_ref[...])
    r = lax.broadcasted_iota(I32, (LANES, LANES), 0)
    c = lax.broadcasted_iota(I32, (LANES, LANES), 1)
    tri = jnp.where(c <= r, 1.0, 0.0).astype(BF16)
    carry = carry_ref[...]
    cums = []
    for r0 in range(0, tm, LANES):
        cums.append(_dot_exact_lhs(tri, log_f[r0:r0 + LANES, :]) + carry)
        carry = cums[-1][LANES - 1:LANES, :]
    carry_ref[...] = carry
    cum = jnp.concatenate(cums, axis=0)
    lane = lax.broadcasted_iota(I32, (1, LANES), 1)
    out = jnp.zeros((tm, LANES), F32)
    for hh in range(FOX_HEADS):
        bias = jnp.where(pos >= n_lead, -LOG2_E * cum[:, GATE_FF + hh:GATE_FF + hh + 1], MASK_VALUE)
        for n, piece in enumerate(_split3(bias)):
            out = jnp.where(lane == _fox_bias_lane(hh) + n, piece.astype(F32), out)
    kb_ref[...] = out.astype(BF16)


def _in_proj(stream, wa, wg, ws, forget_bias_lanes, seq_len, n_lead, tm):
    t_all, d = stream.shape
    nb = seq_len // tm
    row = lambda bb, i: (bb * nb + i, 0)
    const = lambda bb, i: (0, 0)
    return pl.pallas_call(
        functools.partial(_proj_kernel, tm=tm, n_lead=n_lead),
        out_shape=(jax.ShapeDtypeStruct((t_all, ATTN_WIDTH), BF16),
                   jax.ShapeDtypeStruct((t_all, GDN_SLAB), F32),
                   jax.ShapeDtypeStruct((t_all, LANES), F32),
                   jax.ShapeDtypeStruct((t_all, LANES), BF16)),
        grid=(t_all // seq_len, nb),
        in_specs=[pl.BlockSpec((tm, d), row),
                  pl.BlockSpec((d, ATTN_WIDTH), const),
                  pl.BlockSpec((d, GDN_SLAB), const),
                  pl.BlockSpec((d, LANES), const),
                  pl.BlockSpec((1, LANES), const)],
        out_specs=(pl.BlockSpec((tm, ATTN_WIDTH), row),
                   pl.BlockSpec((tm, GDN_SLAB), row),
                   pl.BlockSpec((tm, LANES), row),
                   pl.BlockSpec((tm, LANES), row)),
        scratch_shapes=[pltpu.VMEM((1, LANES), F32)],
        compiler_params=_cparams("arbitrary", "arbitrary"),
        name="in_proj",
    )(stream, wa, wg, ws, forget_bias_lanes)


def _online_softmax_step(s, vt, m_ref, l_ref, a_ref):
    m_old = m_ref[...]
    m_new = jnp.maximum(m_old, jnp.max(s, axis=0, keepdims=True))
    alpha = jnp.exp2(m_old - m_new)
    p = jnp.exp2(s - m_new)
    l_ref[...] = alpha * l_ref[...] + jnp.sum(p, axis=0, keepdims=True)
    a_ref[...] = alpha * a_ref[...] + jnp.dot(vt, p.astype(BF16), preferred_element_type=F32)
    m_ref[...] = m_new


def _pipelined_blocks(i, buf_a, buf_b, produce_first, produce, consume):
    produce_first(0, buf_a)

    def pair(t, carry):
        j = 2 * t
        produce(j + 1, buf_b)
        consume(j, buf_a)
        produce(j + 2, buf_a)
        consume(j + 1, buf_b)
        return carry

    lax.fori_loop(0, i // 2, pair, 0)

    @pl.when(i % 2 == 0)
    def _():
        consume(i, buf_a, diagonal=True)

    @pl.when(i % 2 == 1)
    def _():
        produce(i, buf_b)
        consume(i - 1, buf_a)
        consume(i, buf_b, diagonal=True)


def _transpose_to(dst_ref, src_ref, n_rows):
    for c in range(src_ref.shape[1] // LANES):
        def body(t, carry):
            r0 = pl.multiple_of(t * LANES, LANES)
            tile = src_ref[pl.ds(r0, LANES), c * LANES:(c + 1) * LANES].astype(F32)
            dst_ref[c * LANES:(c + 1) * LANES, pl.ds(r0, LANES)] = tile.T.astype(dst_ref.dtype)
            return carry
        lax.fori_loop(0, n_rows // LANES, body, 0)


def _init_softmax_state(m_ref, l_ref, a_ref):
    m_ref[...] = jnp.full(m_ref.shape, MASK_VALUE, F32)
    l_ref[...] = jnp.zeros(l_ref.shape, F32)
    a_ref[...] = jnp.zeros(a_ref.shape, F32)


def _diff_attn_kernel(lam_ref, q_ref, k_ref, v_ref, g_ref, o_ref, vt_ref, sa_ref, sb_ref,
                      m0, l0, a0, m1, l1, a1, *, tb, n_lead, seq_len, out_scale):
    i = pl.program_id(2)

    @pl.when(i == 0)
    def _():
        _transpose_to(vt_ref, v_ref, seq_len)

    qt = q_ref[...].astype(F32).T
    row = lax.broadcasted_iota(I32, (LANES, 1), 0)
    qt0 = jnp.where(row < DIFF_QK_DIM, qt, 0.0).astype(BF16)
    qt1 = jnp.where(row >= DIFF_QK_DIM, qt, 0.0).astype(BF16)
    _init_softmax_state(m0, l0, a0)
    _init_softmax_state(m1, l1, a1)

    def produce(j, buf, pad_bias=False):
        start = pl.multiple_of(j * tb, tb)
        k = k_ref[pl.ds(start, tb), :]
        for half, qth in enumerate((qt0, qt1)):
            s = jnp.dot(k, qth, preferred_element_type=F32)
            if pad_bias:
                kpos = lax.broadcasted_iota(I32, (tb, 1), 0)
                s = s + jnp.where(kpos >= n_lead, 0.0, MASK_VALUE)
            buf[half] = s

    def consume(j, buf, diagonal=False):
        start = pl.multiple_of(j * tb, tb)
        vt = vt_ref[:, pl.ds(start, tb)]
        for half, state in enumerate(((m0, l0, a0), (m1, l1, a1))):
            s = buf[half]
            if diagonal:
                kpos = j * tb + lax.broadcasted_iota(I32, (tb, 1), 0)
                qpos = i * tb + lax.broadcasted_iota(I32, (1, tb), 1)
                s = jnp.where(kpos >> CHUNK_SHIFT <= qpos >> CHUNK_SHIFT, s, MASK_VALUE)
            _online_softmax_step(s, vt, *state)

    _pipelined_blocks(i, sa_ref, sb_ref, functools.partial(produce, pad_bias=True), produce, consume)

    lam = lam_ref[0]
    o = a0[...] / l0[...] - lam * (a1[...] / l1[...])
    o = o * lax.rsqrt(jnp.mean(o * o, axis=0, keepdims=True) + 1e-5) * g_ref[...] * out_scale
    o_ref[...] = o.T.astype(o_ref.dtype)


def _diff_attention(attn, lam, subln_g, seq_len, n_lead, lambda_init, tb):
    t_all = attn.shape[0]
    bsz = t_all // seq_len
    nq = seq_len // tb
    kcol = DIFF_QK_WIDTH // LANES
    vcol = 2 * DIFF_QK_WIDTH // LANES
    scratch = [pltpu.VMEM((DIFF_V_DIM, seq_len), BF16),
               pltpu.VMEM((2, tb, tb), F32), pltpu.VMEM((2, tb, tb), F32)]
    for _ in range(2):
        scratch += [pltpu.VMEM((1, tb), F32), pltpu.VMEM((1, tb), F32), pltpu.VMEM((DIFF_V_DIM, tb), F32)]
    return pl.pallas_call(
        functools.partial(_diff_attn_kernel, tb=tb, n_lead=n_lead, seq_len=seq_len,
                          out_scale=1.0 - lambda_init),
        out_shape=jax.ShapeDtypeStruct((t_all, DIFF_V_WIDTH), BF16),
        grid_spec=pltpu.PrefetchScalarGridSpec(
            num_scalar_prefetch=1,
            grid=(bsz, DIFF_HEADS, nq),
            in_specs=[pl.BlockSpec((tb, LANES), lambda b, h, i, lam: (b * nq + i, h)),
                      pl.BlockSpec((seq_len, LANES), lambda b, h, i, lam: (b, kcol + h)),
                      pl.BlockSpec((seq_len, LANES), lambda b, h, i, lam: (b, vcol + h)),
                      pl.BlockSpec((DIFF_V_DIM, 1), lambda b, h, i, lam: (0, 0))],
            out_specs=pl.BlockSpec((tb, DIFF_V_DIM), lambda b, h, i, lam: (b * nq + i, h)),
            scratch_shapes=scratch),
        compiler_params=_cparams("arbitrary", "arbitrary", "arbitrary"),
        name="diff_attention",
    )(lam, attn, attn, attn, subln_g.reshape(DIFF_V_DIM, 1))


def _fox_bias_lane(hh):
    return (1 - hh % 2) * FOX_HEAD_DIM + FOX_BIAS_PIECES * (hh // 2)


def _fox_attn_kernel(q_ref, k_ref, v_ref, kb_ref, o_ref, vt_ref, sa_ref, sb_ref, *scratch, tb, seq_len):
    i = pl.program_id(1)
    pairs = FOX_HEADS // 2

    @pl.when(i == 0)
    def _():
        _transpose_to(vt_ref, v_ref, seq_len)

    row = lax.broadcasted_iota(I32, (LANES, 1), 0)
    lane = lax.broadcasted_iota(I32, (1, LANES), 1)
    qts = []
    for p in range(pairs):
        qt = q_ref[:, p * LANES:(p + 1) * LANES].astype(F32).T
        for r in range(2):
            own = jnp.logical_and(row >= r * FOX_HEAD_DIM, row < (r + 1) * FOX_HEAD_DIM)
            b0 = _fox_bias_lane(2 * p + r)
            ones = jnp.logical_and(row >= b0, row < b0 + FOX_BIAS_PIECES)
            qts.append(jnp.where(own, qt, jnp.where(ones, 1.0, 0.0)).astype(BF16))
    for hh in range(FOX_HEADS):
        _init_softmax_state(*scratch[3 * hh:3 * hh + 3])

    def produce(j, buf):
        start = pl.multiple_of(j * tb, tb)
        kb = kb_ref[pl.ds(start, tb), :]
        for p in range(pairs):
            k = k_ref[pl.ds(start, tb), p * LANES:(p + 1) * LANES]
            for r in range(2):
                own = jnp.logical_and(lane >= r * FOX_HEAD_DIM, lane < (r + 1) * FOX_HEAD_DIM)
                buf[2 * p + r] = jnp.dot(jnp.where(own, k, kb), qts[2 * p + r],
                                         preferred_element_type=F32)

    def consume(j, buf, diagonal=False):
        start = pl.multiple_of(j * tb, tb)
        for hh in range(FOX_HEADS):
            vt = vt_ref[hh * FOX_HEAD_DIM:(hh + 1) * FOX_HEAD_DIM, pl.ds(start, tb)]
            s = buf[hh]
            if diagonal:
                kpos = j * tb + lax.broadcasted_iota(I32, (tb, 1), 0)
                qpos = i * tb + lax.broadcasted_iota(I32, (1, tb), 1)
                s = jnp.where(kpos <= qpos, s, MASK_VALUE)
            _online_softmax_step(s, vt, *scratch[3 * hh:3 * hh + 3])

    _pipelined_blocks(i, sa_ref, sb_ref, produce, produce, consume)

    for p in range(pairs):
        _, l_lo, a_lo = scratch[6 * p:6 * p + 3]
        _, l_hi, a_hi = scratch[6 * p + 3:6 * p + 6]
        o = jnp.concatenate([a_lo[...] / l_lo[...], a_hi[...] / l_hi[...]], axis=0)
        o_ref[:, p * LANES:(p + 1) * LANES] = o.T.astype(o_ref.dtype)


def _fox_attention(attn, key_bias, seq_len, tb):
    t_all = attn.shape[0]
    bsz = t_all // seq_len
    nq = seq_len // tb
    base = (2 * DIFF_QK_WIDTH + DIFF_V_WIDTH) // FOX_WIDTH
    scratch = [pltpu.VMEM((FOX_WIDTH, seq_len), BF16),
               pltpu.VMEM((FOX_HEADS, tb, tb), F32), pltpu.VMEM((FOX_HEADS, tb, tb), F32)]
    for _ in range(FOX_HEADS):
        scratch += [pltpu.VMEM((1, tb), F32), pltpu.VMEM((1, tb), F32), pltpu.VMEM((FOX_HEAD_DIM, tb), F32)]
    return pl.pallas_call(
        functools.partial(_fox_attn_kernel, tb=tb, seq_len=seq_len),
        out_shape=jax.ShapeDtypeStruct((t_all, FOX_WIDTH), BF16),
        grid=(bsz, nq),
        in_specs=[pl.BlockSpec((tb, FOX_WIDTH), lambda b, i: (b * nq + i, base)),
                  pl.BlockSpec((seq_len, FOX_WIDTH), lambda b, i: (b, base + 1)),
                  pl.BlockSpec((seq_len, FOX_WIDTH), lambda b, i: (b, base + 2)),
                  pl.BlockSpec((seq_len, LANES), lambda b, i: (b, 0))],
        out_specs=pl.BlockSpec((tb, FOX_WIDTH), lambda b, i: (b * nq + i, 0)),
        scratch_shapes=scratch,
        compiler_params=_cparams("arbitrary", "arbitrary"),
        name="fox_attention",
    )(attn, attn, attn, key_bias)


def _gdn_kernel(x_ref, gate_ref, cw_ref, alog_ref, dtb_ref, ng_ref, o_ref,
                halo_ref, xs_ref, state_ref, *, rows):
    cg = pl.program_id(1)
    g_chunks = rows // CHUNK
    hd = GDN_HEAD_DIM

    @pl.when(cg == 0)
    def _():
        halo_ref[...] = jnp.zeros_like(halo_ref)
        state_ref[...] = jnp.zeros_like(state_ref)

    xin = x_ref[:, :GDN_CONV_CH]
    xs_ref[0:SUBLANES, :] = halo_ref[...]
    xs_ref[SUBLANES:SUBLANES + rows, :] = xin
    halo_ref[...] = xin[rows - SUBLANES:rows, :]
    y = jnp.zeros((rows, GDN_CONV_CH), F32)
    for tap in range(CONV_K):
        back = CONV_K - 1 - tap
        y = y + cw_ref[tap:tap + 1, :] * xs_ref[SUBLANES - back:SUBLANES - back + rows, :]
    y = y * _sigmoid(y)

    gates = gate_ref[...]
    beta_all = _sigmoid(gates)
    g_all = -jnp.exp(alog_ref[...]) * _softplus(gates + dtb_ref[...])
    rowi = lax.broadcasted_iota(I32, (rows, 1), 0) & (CHUNK - 1)
    gc_all = g_all
    shift = 1
    while shift < CHUNK:
        gc_all = gc_all + jnp.where(rowi >= shift, pltpu.roll(gc_all, shift, 0), 0.0)
        shift *= 2

    ri = lax.broadcasted_iota(I32, (CHUNK, CHUNK), 0)
    ci = lax.broadcasted_iota(I32, (CHUNK, CHUNK), 1)
    eye = jnp.where(ri == ci, 1.0, 0.0).astype(F32)
    causal = (ci <= ri)[None]
    strict = (ci < ri)[None]
    eye_b = jnp.broadcast_to(eye[None], (g_chunks, CHUNK, CHUNK))

    prep = []
    for h in range(GDN_HEADS):
        def head3(base):
            return y[:, base + h * hd:base + (h + 1) * hd]
        q = head3(0)
        k = head3(GDN_WIDTH)
        v = head3(2 * GDN_WIDTH)
        q = q * lax.rsqrt(jnp.sum(q * q, axis=-1, keepdims=True) + 1e-6) * (hd ** -0.5)
        k = k * lax.rsqrt(jnp.sum(k * k, axis=-1, keepdims=True) + 1e-6)
        beta = beta_all[:, GATE_BETA + h:GATE_BETA + h + 1]
        gcol = jnp.broadcast_to(gc_all[:, GATE_DECAY + h:GATE_DECAY + h + 1], (rows, hd))
        egc = jnp.exp(gcol)

        q3 = q.reshape(g_chunks, CHUNK, hd)
        k3 = k.reshape(g_chunks, CHUNK, hd)
        gcol3 = gcol.reshape(g_chunks, CHUNK, CHUNK)
        beta3 = jnp.broadcast_to(beta, (rows, hd)).reshape(g_chunks, CHUNK, hd)
        grow3 = jnp.swapaxes(gcol3, 1, 2)
        decay = jnp.exp(jnp.where(causal, gcol3 - grow3, MASK_VALUE))
        a = jnp.where(strict, _bdot_nt(k3, k3) * beta3 * decay, 0.0)

        x = -a
        tinv = eye_b + x
        span = 2
        while span < CHUNK:
            x = _bdot(x, x)
            tinv = tinv + _bdot(tinv, x)
            span *= 2

        vb = (v * beta).reshape(g_chunks, CHUNK, hd)
        kb = (k * beta * egc).reshape(g_chunks, CHUNK, hd)
        u3 = _bdot(tinv, vb)
        w3 = _bdot(tinv, kb)
        qk3 = _bdot_nt(q3, k3) * decay
        qd3 = (q * egc).reshape(g_chunks, CHUNK, hd)
        glast3 = gcol3[:, CHUNK - 1:CHUNK, :]
        kdt3 = jnp.swapaxes(k3 * jnp.exp(glast3 - gcol3), 1, 2)
        cd3 = jnp.exp(glast3)
        prep.append((_bdot(kdt3, w3), _bdot(kdt3, u3), qd3 - _bdot(qk3, w3), _bdot(qk3, u3), cd3))

    states = [state_ref[h] for h in range(GDN_HEADS)]
    outs = [[] for _ in range(GDN_HEADS)]
    for c in range(g_chunks):
        for h in range(GDN_HEADS):
            kw3, ku3, qp3, o03, cd3 = prep[h]
            state = states[h]
            outs[h].append(_dot(qp3[c], state) + o03[c])
            states[h] = state * cd3[c] + (ku3[c] - _dot(kw3[c], state))

    for h in range(GDN_HEADS):
        state_ref[h] = states[h]
        o = jnp.concatenate(outs[h], axis=0)
        o = o * lax.rsqrt(jnp.mean(o * o, axis=-1, keepdims=True) + 1e-6) * ng_ref[...]
        z = x_ref[:, GDN_CONV_CH + h * hd:GDN_CONV_CH + (h + 1) * hd]
        o_ref[:, h * hd:(h + 1) * hd] = (o * (z * _sigmoid(z))).astype(o_ref.dtype)


def _gated_deltanet(gdn_slab, gates, conv_w, a_log_lanes, dt_bias_lanes, norm_g, seq_len, rows):
    t_all = gdn_slab.shape[0]
    bsz = t_all // seq_len
    nb = seq_len // rows
    row = lambda b, i: (b * nb + i, 0)
    const = lambda b, i: (0, 0)
    return pl.pallas_call(
        functools.partial(_gdn_kernel, rows=rows),
        out_shape=jax.ShapeDtypeStruct((t_all, GDN_WIDTH), BF16),
        grid=(bsz, nb),
        in_specs=[pl.BlockSpec((rows, GDN_SLAB), row),
                  pl.BlockSpec((rows, LANES), row),
                  pl.BlockSpec((SUBLANES, GDN_CONV_CH), const),
                  pl.BlockSpec((1, LANES), const),
                  pl.BlockSpec((1, LANES), const),
                  pl.BlockSpec((1, GDN_HEAD_DIM), const)],
        out_specs=pl.BlockSpec((rows, GDN_WIDTH), row),
        scratch_shapes=[pltpu.VMEM((SUBLANES, GDN_CONV_CH), F32),
                        pltpu.VMEM((rows + SUBLANES, GDN_CONV_CH), F32),
                        pltpu.VMEM((GDN_HEADS, GDN_HEAD_DIM, GDN_HEAD_DIM), F32)],
        compiler_params=_cparams("arbitrary", "arbitrary"),
        name="gated_deltanet",
    )(gdn_slab, gates, conv_w, a_log_lanes, dt_bias_lanes, norm_g.reshape(1, GDN_HEAD_DIM))


def _mix_router_kernel(x_ref, yd_ref, yf_ref, yg_ref, wd_ref, wf_ref, wg_ref, g_ref, b_ref,
                       rw_ref, rb_ref, x1_ref, lr_ref, lrt_ref, ge_ref, cnt_ref, *, tm, alpha):
    mix = (jnp.dot(yd_ref[...], wd_ref[...], preferred_element_type=F32)
           + jnp.dot(yf_ref[...], wf_ref[...], preferred_element_type=F32)
           + jnp.dot(yg_ref[...], wg_ref[...], preferred_element_type=F32))
    x1 = _layer_norm(alpha * x_ref[...] + mix, g_ref[...], b_ref[...])
    x1_ref[...] = x1

    lane = lax.broadcasted_iota(I32, (tm, LANES), 1)
    lane_f = lane.astype(F32)
    logits = _dot3(x1, rw_ref[...]) + rb_ref[...]
    vals = jnp.where(lane < N_EXPERTS, logits, -jnp.inf)
    hits, tops = [], []
    for _ in range(TOP_K):
        top = jnp.max(vals, axis=-1, keepdims=True)
        idx = jnp.min(jnp.where(vals == top, lane_f, float(LANES)), axis=-1, keepdims=True)
        hit = lane_f == idx
        vals = jnp.where(hit, -jnp.inf, vals)
        hits.append(hit)
        tops.append(top)
    exps = [jnp.exp(t - tops[0]) for t in tops]
    denom = exps[0] + exps[1] + exps[2] + exps[3]
    onehot = jnp.zeros((tm, LANES), F32)
    gate_by_expert = jnp.zeros((tm, LANES), F32)
    for kk in range(TOP_K):
        onehot = jnp.where(hits[kk], 1.0, onehot)
        gate_by_expert = jnp.where(hits[kk], exps[kk] / denom, gate_by_expert)

    r = lax.broadcasted_iota(I32, (tm, tm), 0)
    c = lax.broadcasted_iota(I32, (tm, tm), 1)
    tri = jnp.where(c < r, 1.0, 0.0).astype(BF16)
    rank = jnp.dot(tri, onehot.astype(BF16), preferred_element_type=F32)
    local = jnp.where(onehot > 0.0, rank, -1.0)
    lr_ref[...] = local
    lrt_ref[...] = local.T[:N_EXPERTS, :]
    ge_ref[...] = gate_by_expert
    cnt_ref[0] = jnp.sum(onehot, axis=0, keepdims=True)


def _mix_router(stream, yd, yf, yg, w_out_b, ln_g, ln_b, rw, rb, alpha, tm):
    t_all, d = stream.shape
    row = lambda i: (i, 0)
    const = lambda i: (0, 0)
    wd = w_out_b[:DIFF_V_WIDTH]
    wf = w_out_b[DIFF_V_WIDTH:DIFF_V_WIDTH + FOX_WIDTH]
    wg = w_out_b[DIFF_V_WIDTH + FOX_WIDTH:]
    return pl.pallas_call(
        functools.partial(_mix_router_kernel, tm=tm, alpha=alpha),
        out_shape=(jax.ShapeDtypeStruct((t_all, d), F32),
                   jax.ShapeDtypeStruct((t_all, LANES), F32),
                   jax.ShapeDtypeStruct((t_all // tm * N_EXPERTS, tm), F32),
                   jax.ShapeDtypeStruct((t_all, LANES), F32),
                   jax.ShapeDtypeStruct((t_all // tm, 1, LANES), F32)),
        grid=(t_all // tm,),
        in_specs=[pl.BlockSpec((tm, d), row),
                  pl.BlockSpec((tm, DIFF_V_WIDTH), row),
                  pl.BlockSpec((tm, FOX_WIDTH), row),
                  pl.BlockSpec((tm, GDN_WIDTH), row),
                  pl.BlockSpec((DIFF_V_WIDTH, d), const),
                  pl.BlockSpec((FOX_WIDTH, d), const),
                  pl.BlockSpec((GDN_WIDTH, d), const),
                  pl.BlockSpec((1, d), const),
                  pl.BlockSpec((1, d), const),
                  pl.BlockSpec((d, LANES), const),
                  pl.BlockSpec((1, LANES), const)],
        out_specs=(pl.BlockSpec((tm, d), row),
                   pl.BlockSpec((tm, LANES), row),
                   pl.BlockSpec((N_EXPERTS, tm), row),
                   pl.BlockSpec((tm, LANES), row),
                   pl.BlockSpec((1, 1, LANES), lambda i: (i, 0, 0))),
        compiler_params=_cparams("arbitrary"),
        name="mix_router",
    )(stream, yd, yf, yg, wd, wf, wg, ln_g.reshape(1, d), ln_b.reshape(1, d), rw, rb)


SLAB = 48
GROUP_ALIGN = 8
ZERO_FILL_SIZES = (256, 128, 64, 32, 16, 8)


def _slab_rows(src_ref, rank_row0):
    want = (rank_row0 + lax.broadcasted_iota(I32, (SLAB, 1), 0)).astype(F32)
    return jnp.concatenate([jnp.where(src_ref[e:e + 1, :] == want, 1.0, 0.0).astype(BF16)
                            for e in range(N_EXPERTS)], axis=0)


def _dispatch_kernel(gstart_ref, gcnt_ref, npass_ref, zstart_ref, zrows_ref, x_ref, lrt_ref, xs_ref,
                     zbuf, zero_ref, sem, zsem, *, nb, n_slots):
    b = pl.program_id(0)
    slot = b % 2

    def slab_copy(bb, c, e, s):
        start = pl.multiple_of(gstart_ref[bb * N_EXPERTS + e] + c * SLAB, GROUP_ALIGN)
        return pltpu.make_async_copy(zbuf.at[s, pl.ds(e * SLAB, SLAB)], xs_ref.at[pl.ds(start, SLAB)], sem.at[s])

    def zero_fill(act):
        for e in range(N_EXPERTS):
            rows = zrows_ref[e]
            for size in ZERO_FILL_SIZES:
                @pl.when((rows & size) != 0)
                def _():
                    start = pl.multiple_of(zstart_ref[e] + (rows & ~(2 * size - 1)), GROUP_ALIGN)
                    act(pltpu.make_async_copy(zero_ref.at[pl.ds(0, size)], xs_ref.at[pl.ds(start, size)], zsem))

    def zero_tail(act):
        first = zstart_ref[N_EXPERTS - 1] + zrows_ref[N_EXPERTS - 1]
        size = ZERO_FILL_SIZES[0]

        def body(t, carry):
            start = pl.multiple_of(first + t * size, GROUP_ALIGN)
            act(pltpu.make_async_copy(zero_ref, xs_ref.at[pl.ds(start, size)], zsem))
            return carry

        lax.fori_loop(0, (n_slots - first) // size, body, 0)

    @pl.when(b == 0)
    def _():
        zero_ref[...] = jnp.zeros_like(zero_ref)
        for fill in (zero_fill, zero_tail):
            fill(lambda cp: cp.start())
        for fill in (zero_fill, zero_tail):
            fill(lambda cp: cp.wait())

    xb = x_ref[...].astype(BF16)
    zbuf[slot] = jnp.dot(_slab_rows(lrt_ref, 0), xb, preferred_element_type=F32)

    @pl.when(b > 0)
    def _():
        @pl.when(npass_ref[b - 1] <= 1)
        def _():
            for e in range(N_EXPERTS):
                slab_copy(b - 1, 0, e, 1 - slot).wait()

    for e in range(N_EXPERTS):
        slab_copy(b, 0, e, slot).start()

    @pl.when(npass_ref[b] > 1)
    def _():
        for e in range(N_EXPERTS):
            slab_copy(b, 0, e, slot).wait()

        def extra_pass(c, carry):
            zbuf[slot] = jnp.dot(_slab_rows(lrt_ref, c * SLAB), xb, preferred_element_type=F32)
            for action in ("start", "wait"):
                for e in range(N_EXPERTS):
                    @pl.when(gcnt_ref[b * N_EXPERTS + e] > c * SLAB)
                    def _():
                        getattr(slab_copy(b, c, e, slot), action)()
            return carry

        lax.fori_loop(1, npass_ref[b], extra_pass, 0)

    @pl.when(b == nb - 1)
    def _():
        @pl.when(npass_ref[b] <= 1)
        def _():
            for e in range(N_EXPERTS):
                slab_copy(b, 0, e, slot).wait()


def _dispatch(tables, x1, lrt, n_slots, tm):
    t_all, d = x1.shape
    nb = t_all // tm
    ntab = len(tables)
    return pl.pallas_call(
        functools.partial(_dispatch_kernel, nb=nb, n_slots=n_slots),
        out_shape=jax.ShapeDtypeStruct((n_slots, d), F32),
        grid_spec=pltpu.PrefetchScalarGridSpec(
            num_scalar_prefetch=ntab,
            grid=(nb,),
            in_specs=[pl.BlockSpec((tm, d), lambda i, *_: (i, 0)),
                      pl.BlockSpec((N_EXPERTS, tm), lambda i, *_: (i, 0))],
            out_specs=pl.BlockSpec(memory_space=pl.ANY),
            scratch_shapes=[pltpu.VMEM((2, N_EXPERTS * SLAB, d), F32),
                            pltpu.VMEM((ZERO_FILL_SIZES[0], d), F32),
                            pltpu.SemaphoreType.DMA((2,)),
                            pltpu.SemaphoreType.DMA]),
        compiler_params=_cparams("arbitrary"),
        name="moe_dispatch",
    )(*tables, x1, lrt)


def _expert_kernel(be_ref, na_ref, x_ref, w1_ref, b1_ref, w2_ref, b2_ref, o_ref, w1b_ref, w2b_ref, *, d_expert):
    i = pl.program_id(0)
    active = i < na_ref[0]
    new_expert = jnp.logical_or(i == 0, be_ref[i] != be_ref[jnp.maximum(i - 1, 0)])

    @pl.when(jnp.logical_and(active, new_expert))
    def _():
        w1b_ref[...] = w1_ref[0].astype(BF16)
        w2b_ref[...] = w2_ref[0].astype(BF16)

    @pl.when(active)
    def _():
        hid = jnp.dot(x_ref[...].astype(BF16), w1b_ref[...], preferred_element_type=F32) + b1_ref[0]
        glu = jnp.minimum(hid[:, :d_expert], SWIGLU_LIMIT)
        lin = jnp.clip(hid[:, d_expert:], -SWIGLU_LIMIT, SWIGLU_LIMIT)
        act = glu * _sigmoid(SWIGLU_ALPHA * glu) * (lin + 1.0)
        o_ref[...] = jnp.dot(act.astype(BF16), w2b_ref[...], preferred_element_type=F32) + b2_ref[0]

    @pl.when(jnp.logical_not(active))
    def _():
        o_ref[...] = jnp.zeros_like(o_ref)


def _experts(block_expert, n_active, xs, w1_all, b1, w2_all, b2, layer, mb):
    n_slots, d = xs.shape
    depth, n_exp, _, f2 = w1_all.shape
    f = f2 // 2
    nb = n_slots // mb
    w1 = w1_all.reshape(depth * n_exp, d, f2)
    w2 = w2_all.reshape(depth * n_exp, f, d)
    block_expert = block_expert + layer * n_exp
    return pl.pallas_call(
        functools.partial(_expert_kernel, d_expert=f),
        out_shape=jax.ShapeDtypeStruct((n_slots, d), F32),
        grid_spec=pltpu.PrefetchScalarGridSpec(
            num_scalar_prefetch=2,
            grid=(nb,),
            in_specs=[pl.BlockSpec((mb, d), lambda i, be, na: (jnp.minimum(i, na[0] - 1), 0)),
                      pl.BlockSpec((1, d, f2), lambda i, be, na: (be[i], 0, 0)),
                      pl.BlockSpec((1, 1, f2), lambda i, be, na: (be[i], 0, 0)),
                      pl.BlockSpec((1, f, d), lambda i, be, na: (be[i], 0, 0)),
                      pl.BlockSpec((1, 1, d), lambda i, be, na: (be[i], 0, 0))],
            out_specs=pl.BlockSpec((mb, d), lambda i, be, na: (i, 0)),
            scratch_shapes=[pltpu.VMEM((d, f2), BF16), pltpu.VMEM((f, d), BF16)]),
        compiler_params=_cparams("arbitrary"),
        name="moe_experts",
    )(block_expert, n_active, xs, w1, b1.reshape(depth * n_exp, 1, f2), w2, b2.reshape(depth * n_exp, 1, d))


def _combine_kernel(gstart_ref, gcnt_ref, npass_ref, x1_ref, lr_ref, ge_ref, g_ref, b_ref, ys_ref, o_ref,
                    ybuf, acc_ref, sem, *, tm, nb, alpha):
    b = pl.program_id(0)
    slot = b % 2

    def slab_copy(bb, c, e, s):
        start = pl.multiple_of(gstart_ref[bb * N_EXPERTS + e] + c * SLAB, GROUP_ALIGN)
        return pltpu.make_async_copy(ys_ref.at[pl.ds(start, SLAB)], ybuf.at[s, pl.ds(e * SLAB, SLAB)], sem.at[s])

    @pl.when(b == 0)
    def _():
        for e in range(N_EXPERTS):
            slab_copy(0, 0, e, 0).start()

    @pl.when(b + 1 < nb)
    def _():
        for e in range(N_EXPERTS):
            slab_copy(b + 1, 0, e, 1 - slot).start()

    for e in range(N_EXPERTS):
        slab_copy(b, 0, e, slot).wait()

    lane = lax.broadcasted_iota(I32, (1, LANES), 1)
    lr = lr_ref[...]
    ge = ge_ref[...]

    def weighted_sum(rank0):
        his, los = [], []
        for col0 in range(0, N_EXPERTS * SLAB, LANES):
            col = col0 + lane
            rank = jnp.full((tm, LANES), -2.0, F32)
            gate = jnp.zeros((tm, LANES), F32)
            want = jnp.zeros((1, LANES), I32)
            for e in range(col0 // SLAB, min((col0 + LANES - 1) // SLAB, N_EXPERTS - 1) + 1):
                inside = jnp.logical_and(col >= e * SLAB, col < (e + 1) * SLAB)
                rank = jnp.where(inside, lr[:, e:e + 1], rank)
                gate = jnp.where(inside, ge[:, e:e + 1], gate)
                want = jnp.where(inside, col - e * SLAB + rank0, want)
            w = jnp.where(rank == want.astype(F32), gate, 0.0)
            hi = w.astype(BF16)
            his.append(hi)
            los.append((w - hi.astype(F32)).astype(BF16))
        sel = jnp.concatenate([jnp.concatenate(his, axis=1), jnp.concatenate(los, axis=1)], axis=0)
        r = jnp.dot(sel, ybuf[slot].astype(BF16), preferred_element_type=F32)
        return r[:tm] + r[tm:]

    acc_ref[...] = alpha * x1_ref[...] + weighted_sum(0)

    @pl.when(npass_ref[b] > 1)
    def _():
        def extra_pass(c, carry):
            for action in ("start", "wait"):
                for e in range(N_EXPERTS):
                    @pl.when(gcnt_ref[b * N_EXPERTS + e] > c * SLAB)
                    def _():
                        getattr(slab_copy(b, c, e, slot), action)()
            acc_ref[...] += weighted_sum(c * SLAB)
            return carry

        lax.fori_loop(1, npass_ref[b], extra_pass, 0)

    o_ref[...] = _layer_norm(acc_ref[...], g_ref[...], b_ref[...])


def _combine(tables, x1, lr, ge, ln_g, ln_b, ys, alpha, tm):
    t_all, d = x1.shape
    nb = t_all // tm
    ntab = len(tables)
    row = lambda i, *_: (i, 0)
    const = lambda i, *_: (0, 0)
    return pl.pallas_call(
        functools.partial(_combine_kernel, tm=tm, nb=nb, alpha=alpha),
        out_shape=jax.ShapeDtypeStruct((t_all, d), F32),
        grid_spec=pltpu.PrefetchScalarGridSpec(
            num_scalar_prefetch=ntab,
            grid=(nb,),
            in_specs=[pl.BlockSpec((tm, d), row),
                      pl.BlockSpec((tm, LANES), row),
                      pl.BlockSpec((tm, LANES), row),
                      pl.BlockSpec((1, d), const),
                      pl.BlockSpec((1, d), const),
                      pl.BlockSpec(memory_space=pl.ANY)],
            out_specs=pl.BlockSpec((tm, d), row),
            scratch_shapes=[pltpu.VMEM((2, N_EXPERTS * SLAB, d), F32),
                            pltpu.VMEM((tm, d), F32),
                            pltpu.SemaphoreType.DMA((2,))]),
        compiler_params=_cparams("arbitrary"),
        name="moe_combine",
    )(*tables, x1, lr, ge, ln_g.reshape(1, d), ln_b.reshape(1, d), ys)


def _lanes(values, offset):
    return jnp.zeros((1, LANES), F32).at[0, offset:offset + values.shape[0]].set(values.astype(F32))


def _split_in_proj(w_in_layer):
    offs = np.concatenate([[0], np.cumsum(IN_PROJ_SIZES)])
    seg = lambda n: w_in_layer[:, offs[n]:offs[n + 1]]
    qk_scale = DIFF_QK_DIM ** -0.5 * LOG2_E
    fox_scale = FOX_HEAD_DIM ** -0.5 * LOG2_E
    wa = jnp.concatenate([seg(0) * qk_scale, seg(1), seg(2), seg(3) * fox_scale, seg(4), seg(5)], axis=1)
    wg = jnp.concatenate([seg(7), seg(10)], axis=1)
    d = w_in_layer.shape[0]
    ws = jnp.zeros((d, LANES), F32)
    ws = ws.at[:, GATE_FF:GATE_FF + FOX_HEADS].set(seg(6))
    ws = ws.at[:, GATE_BETA:GATE_BETA + GDN_HEADS].set(seg(8))
    ws = ws.at[:, GATE_DECAY:GATE_DECAY + GDN_HEADS].set(seg(9))
    return wa.astype(BF16), wg.astype(BF16), ws


def _diff_lambda_init(layer):
    return 0.8 - 0.6 * math.exp(-0.3 * layer)


def _pick_block(seq_len, candidates):
    for c in candidates:
        if seq_len % c == 0:
            return c
    raise ValueError(f"unsupported sequence length {seq_len}")


MOE_ROWS = 256
TOKEN_BLOCK = 256


def _layer(stream, layer, seq_len, n_lead, alpha, p):
    t_all, d = stream.shape
    attn_tb = _pick_block(seq_len, (640, 512, 256, 128))
    wa, wg, ws = _split_in_proj(p["w_in"])
    attn, gdn_slab, gates, key_bias = _in_proj(stream, wa, wg, ws, _lanes(p["fox_forget_b"], GATE_FF),
                                               seq_len, n_lead, attn_tb)

    lv = p["diff_lambda"].astype(F32)
    lambda_init = _diff_lambda_init(layer)
    lam = (jnp.exp(jnp.sum(lv[0] * lv[1])) - jnp.exp(jnp.sum(lv[2] * lv[3])) + lambda_init).reshape(1)
    y_diff = _diff_attention(attn, lam, p["diff_subln_g"], seq_len, n_lead, lambda_init, attn_tb)

    y_fox = _fox_attention(attn, key_bias, seq_len, attn_tb)

    conv_w = jnp.zeros((SUBLANES, GDN_CONV_CH), F32).at[:CONV_K].set(p["gdn_conv_w"])
    y_gdn = _gated_deltanet(gdn_slab, gates, conv_w, _lanes(p["gdn_a_log"], GATE_DECAY),
                            _lanes(p["gdn_dt_bias"], GATE_DECAY), p["gdn_norm_g"], seq_len, attn_tb)

    rw = jnp.zeros((d, LANES), F32).at[:, :N_EXPERTS].set(p["router_w"])
    rb = _lanes(p["router_b"], 0)
    tm = TOKEN_BLOCK
    x1, lr, lrt, ge, counts = _mix_router(stream, y_diff, y_fox, y_gdn, p["w_out"].astype(BF16),
                                          p["ln1_g"], p["ln1_b"], rw, rb, alpha, tm)

    mb = MOE_ROWS
    nb = t_all // tm
    n_slots = -(-(t_all * TOP_K + (GROUP_ALIGN - 1) * nb * N_EXPERTS
                  + N_EXPERTS * (SLAB + mb - GROUP_ALIGN)) // mb) * mb
    n_blocks = n_slots // mb
    cnt = counts[:, 0, :N_EXPERTS].astype(I32)
    rows = (cnt + GROUP_ALIGN - 1) // GROUP_ALIGN * GROUP_ALIGN
    total = jnp.sum(rows, axis=0)
    region = (total + SLAB + mb - 1) // mb * mb
    pad_end = jnp.cumsum(region)
    pad_start = pad_end - region
    gstart = (pad_start[None, :] + jnp.cumsum(rows, axis=0) - rows).reshape(-1)
    npass = jnp.maximum((jnp.max(cnt, axis=1) + SLAB - 1) // SLAB, 1)
    block_first = jnp.arange(n_blocks, dtype=I32) * mb
    block_expert = jnp.minimum(jnp.sum((pad_end[None, :] <= block_first[:, None]).astype(I32), axis=1),
                               N_EXPERTS - 1)
    n_active = (pad_end[-1] // mb).astype(I32).reshape(1)
    group_tables = (gstart.astype(I32), cnt.reshape(-1), npass.astype(I32))

    xs = _dispatch(group_tables + ((pad_start + total).astype(I32), (region - total).astype(I32)),
                   x1, lrt, n_slots, tm)
    ys = _experts(block_expert, n_active, xs, p["expert_w1"], p["expert_b1"], p["expert_w2"], p["expert_b2"],
                  layer, mb)
    return _combine(group_tables, x1, lr, ge, p["ln2_g"], p["ln2_b"], ys, alpha, tm)


def kernel(x, meta_tokens, ln_in_g, ln_in_b, w_in, diff_lambda, diff_subln_g, fox_forget_b, gdn_conv_w,
           gdn_a_log, gdn_dt_bias, gdn_norm_g, w_out, ln1_g, ln1_b, router_w, router_b, expert_w1,
           expert_b1, expert_w2, expert_b2, ln2_g, ln2_b):
    bsz, s, d = x.shape
    depth = w_in.shape[0]
    n_lead = (-(s + N_META)) % Q_BLOCK
    if n_lead + N_META != Q_BLOCK:
        raise ValueError("sequence length must be a multiple of the query block")
    seq_len = n_lead + N_META + s
    alpha = (2 * depth) ** 0.25
    stream = _ln_in(x, meta_tokens.astype(x.dtype), ln_in_g, ln_in_b, n_lead).reshape(bsz * seq_len, d)
    per_layer = dict(w_in=w_in, diff_lambda=diff_lambda, diff_subln_g=diff_subln_g, fox_forget_b=fox_forget_b,
                     gdn_conv_w=gdn_conv_w, gdn_a_log=gdn_a_log, gdn_dt_bias=gdn_dt_bias,
                     gdn_norm_g=gdn_norm_g, w_out=w_out, ln1_g=ln1_g, ln1_b=ln1_b, router_w=router_w,
                     router_b=router_b, expert_w1=expert_w1, expert_b1=expert_b1, expert_w2=expert_w2,
                     expert_b2=expert_b2, ln2_g=ln2_g, ln2_b=ln2_b)
    stacked = ("expert_w1", "expert_b1", "expert_w2", "expert_b2")
    for layer in range(depth):
        p = {k: (v if k in stacked else v[layer]) for k, v in per_layer.items()}
        stream = _layer(stream, layer, seq_len, n_lead, alpha, p)
    return stream.reshape(bsz, seq_len, d)[:, n_lead + N_META:]
```

```python
import functools
import math

import numpy as np
import jax
import jax.numpy as jnp
from jax import lax
from jax.experimental import pallas as pl
from jax.experimental.pallas import tpu as pltpu

F32 = jnp.float32
BF16 = jnp.bfloat16
I32 = jnp.int32

CHUNK = 64
CHUNK_SHIFT = 6
Q_BLOCK = 128
N_META = 16
DIFF_HEADS = 4
DIFF_QK_DIM = 64
DIFF_V_DIM = 128
FOX_HEADS = 4
FOX_HEAD_DIM = 64
GDN_HEADS = 4
GDN_HEAD_DIM = 64
CONV_K = 4
N_EXPERTS = 32
TOP_K = 4
SWIGLU_LIMIT = 7.0
SWIGLU_ALPHA = 1.702
MASK_VALUE = -1e30
LOG2_E = 1.4426950408889634
FOX_BIAS_PIECES = 3

DIFF_QK_WIDTH = DIFF_HEADS * 2 * DIFF_QK_DIM
DIFF_V_WIDTH = DIFF_HEADS * DIFF_V_DIM
FOX_WIDTH = FOX_HEADS * FOX_HEAD_DIM
GDN_WIDTH = GDN_HEADS * GDN_HEAD_DIM
GDN_CONV_CH = 3 * GDN_WIDTH
IN_PROJ_SIZES = (DIFF_QK_WIDTH, DIFF_QK_WIDTH, DIFF_V_WIDTH, FOX_WIDTH, FOX_WIDTH, FOX_WIDTH, FOX_HEADS,
                 GDN_CONV_CH, GDN_HEADS, GDN_HEADS, GDN_WIDTH)

LANES = 128
SUBLANES = 8
VMEM_LIMIT = 56 * 1024 * 1024

GATE_FF = 0
GATE_BETA = 4
GATE_DECAY = 8

ATTN_WIDTH = 2 * DIFF_QK_WIDTH + DIFF_V_WIDTH + 3 * FOX_WIDTH
GDN_SLAB = GDN_CONV_CH + GDN_WIDTH


def _cparams(*sem):
    return pltpu.CompilerParams(dimension_semantics=tuple(sem), vmem_limit_bytes=VMEM_LIMIT)


def _dot(a, b):
    return jnp.dot(a.astype(BF16), b.astype(BF16), preferred_element_type=F32)


def _dot_nt(a, b):
    return lax.dot_general(a.astype(BF16), b.astype(BF16), (((1,), (1,)), ((), ())),
                           preferred_element_type=F32)


def _split2(x):
    hi = x.astype(BF16)
    lo = (x - hi.astype(F32)).astype(BF16)
    return hi, lo


def _split3(x):
    hi = x.astype(BF16)
    r = x - hi.astype(F32)
    mid = r.astype(BF16)
    lo = (r - mid.astype(F32)).astype(BF16)
    return hi, mid, lo


def _dot3(a, b):
    ah, al = _split2(a)
    bh, bl = _split2(b)
    return (jnp.dot(ah, bh, preferred_element_type=F32) + jnp.dot(ah, bl, preferred_element_type=F32)
            + jnp.dot(al, bh, preferred_element_type=F32))


def _dot_exact_lhs(a01, b):
    a = a01.astype(BF16)
    hi, mid, lo = _split3(b)
    return (jnp.dot(a, hi, preferred_element_type=F32) + jnp.dot(a, mid, preferred_element_type=F32)
            + jnp.dot(a, lo, preferred_element_type=F32))


_BDIMS = (((2,), (1,)), ((0,), (0,)))
_BDIMS_NT = (((2,), (2,)), ((0,), (0,)))


def _bdot(a, b):
    return lax.dot_general(a.astype(BF16), b.astype(BF16), _BDIMS, preferred_element_type=F32)


def _bdot_nt(a, b):
    return lax.dot_general(a.astype(BF16), b.astype(BF16), _BDIMS_NT, preferred_element_type=F32)


def _bdot3(a, b):
    ah, al = _split2(a)
    bh, bl = _split2(b)
    return (lax.dot_general(ah, bh, _BDIMS, preferred_element_type=F32)
            + lax.dot_general(ah, bl, _BDIMS, preferred_element_type=F32)
            + lax.dot_general(al, bh, _BDIMS, preferred_element_type=F32))


def _bdot_exact_lhs(a01, b):
    a = a01.astype(BF16)
    hi, mid, lo = _split3(b)
    return (lax.dot_general(a, hi, _BDIMS, preferred_element_type=F32)
            + lax.dot_general(a, mid, _BDIMS, preferred_element_type=F32)
            + lax.dot_general(a, lo, _BDIMS, preferred_element_type=F32))


def _layer_norm(x, g, b, eps=1e-5):
    mu = jnp.mean(x, axis=-1, keepdims=True)
    xc = x - mu
    var = jnp.mean(xc * xc, axis=-1, keepdims=True)
    return xc * lax.rsqrt(var + eps) * g + b


def _sigmoid(x):
    return 1.0 / (1.0 + jnp.exp(-x))


def _softplus(x):
    return jnp.maximum(x, 0.0) + jnp.log1p(jnp.exp(-jnp.abs(x)))


def _log_sigmoid(x):
    return -_softplus(-x)


def _ln_in_kernel(x_ref, meta_ref, g_ref, b_ref, o_ref, *, n_lead):
    i = pl.program_id(1)
    g = g_ref[...]
    b = b_ref[...]

    @pl.when(i == 0)
    def _():
        o_ref[0, :n_lead, :] = jnp.broadcast_to(b, (n_lead, b.shape[-1]))
        o_ref[0, n_lead:, :] = _layer_norm(meta_ref[...], g, b)

    @pl.when(i > 0)
    def _():
        o_ref[0] = _layer_norm(x_ref[0], g, b)


def _ln_in(x, meta, g, b, n_lead):
    bsz, s, d = x.shape
    l = n_lead + N_META + s
    blk = n_lead + N_META
    return pl.pallas_call(
        functools.partial(_ln_in_kernel, n_lead=n_lead),
        out_shape=jax.ShapeDtypeStruct((bsz, l, d), F32),
        grid=(bsz, l // blk),
        in_specs=[pl.BlockSpec((1, blk, d), lambda bb, i: (bb, jnp.maximum(i - 1, 0), 0)),
                  pl.BlockSpec((N_META, d), lambda bb, i: (0, 0)),
                  pl.BlockSpec((1, d), lambda bb, i: (0, 0)),
                  pl.BlockSpec((1, d), lambda bb, i: (0, 0))],
        out_specs=pl.BlockSpec((1, blk, d), lambda bb, i: (bb, i, 0)),
        compiler_params=_cparams("arbitrary", "arbitrary"),
        name="ln_in",
    )(x, meta, g.reshape(1, d), b.reshape(1, d))


PROJ_COL_CHUNK = 768


def _proj_kernel(x_ref, wa_ref, wg_ref, ws_ref, fb_ref, oa_ref, og_ref, os_ref, kb_ref, carry_ref, *, tm, n_lead):
    i = pl.program_id(1)
    pos = i * tm + lax.broadcasted_iota(I32, (tm, 1), 0)
    x = jnp.where(pos >= n_lead, x_ref[...], 0.0)
    xb = x.astype(BF16)
    for c in range(0, ATTN_WIDTH, PROJ_COL_CHUNK):
        oa_ref[:, c:c + PROJ_COL_CHUNK] = jnp.dot(
            xb, wa_ref[:, c:c + PROJ_COL_CHUNK], preferred_element_type=F32).astype(BF16)
    og_ref[...] = jnp.dot(xb, wg_ref[...], preferred_element_type=F32)
    gates = _dot3(x, ws_ref[...])
    os_ref[...] = gates

    @pl.when(i == 0)
    def _():
        carry_ref[...] = jnp.zeros_like(carry_ref)

    log_f = _log_sigmoid(gates + fb_ref[...])
    r = lax.broadcasted_iota(I32, (LANES, LANES), 0)
    c = lax.broadcasted_iota(I32, (LANES, LANES), 1)
    tri = jnp.where(c <= r, 1.0, 0.0).astype(BF16)
    carry = carry_ref[...]
    cums = []
    for r0 in range(0, tm, LANES):
        cums.append(_dot_exact_lhs(tri, log_f[r0:r0 + LANES, :]) + carry)
        carry = cums[-1][LANES - 1:LANES, :]
    carry_ref[...] = carry
    cum = jnp.concatenate(cums, axis=0)
    lane = lax.broadcasted_iota(I32, (1, LANES), 1)
    out = jnp.zeros((tm, LANES), F32)
    for hh in range(FOX_HEADS):
        bias = jnp.where(pos >= n_lead, -LOG2_E * cum[:, GATE_FF + hh:GATE_FF + hh + 1], MASK_VALUE)
        for n, piece in enumerate(_split3(bias)):
            out = jnp.where(lane == _fox_bias_lane(hh) + n, piece.astype(F32), out)
    kb_ref[...] = out.astype(BF16)


def _in_proj(stream, wa, wg, ws, forget_bias_lanes, seq_len, n_lead, tm):
    t_all, d = stream.shape
    nb = seq_len // tm
    row = lambda bb, i: (bb * nb + i, 0)
    const = lambda bb, i: (0, 0)
    return pl.pallas_call(
        functools.partial(_proj_kernel, tm=tm, n_lead=n_lead),
        out_shape=(jax.ShapeDtypeStruct((t_all, ATTN_WIDTH), BF16),
                   jax.ShapeDtypeStruct((t_all, GDN_SLAB), F32),
                   jax.ShapeDtypeStruct((t_all, LANES), F32),
                   jax.ShapeDtypeStruct((t_all, LANES), BF16)),
        grid=(t_all // seq_len, nb),
        in_specs=[pl.BlockSpec((tm, d), row),
                  pl.BlockSpec((d, ATTN_WIDTH), const),
                  pl.BlockSpec((d, GDN_SLAB), const),
                  pl.BlockSpec((d, LANES), const),
                  pl.BlockSpec((1, LANES), const)],
        out_specs=(pl.BlockSpec((tm, ATTN_WIDTH), row),
                   pl.BlockSpec((tm, GDN_SLAB), row),
                   pl.BlockSpec((tm, LANES), row),
                   pl.BlockSpec((tm, LANES), row)),
        scratch_shapes=[pltpu.VMEM((1, LANES), F32)],
        compiler_params=_cparams("arbitrary", "arbitrary"),
        name="in_proj",
    )(stream, wa, wg, ws, forget_bias_lanes)


def _online_softmax_step(s, vt, m_ref, l_ref, a_ref):
    m_old = m_ref[...]
    m_new = jnp.maximum(m_old, jnp.max(s, axis=0, keepdims=True))
    alpha = jnp.exp2(m_old - m_new)
    p = jnp.exp2(s - m_new)
    l_ref[...] = alpha * l_ref[...] + jnp.sum(p, axis=0, keepdims=True)
    a_ref[...] = alpha * a_ref[...] + jnp.dot(vt, p.astype(BF16), preferred_element_type=F32)
    m_ref[...] = m_new


def _pipelined_blocks(i, buf_a, buf_b, produce_first, produce, consume):
    produce_first(0, buf_a)

    def pair(t, carry):
        j = 2 * t
        produce(j + 1, buf_b)
        consume(j, buf_a)
        produce(j + 2, buf_a)
        consume(j + 1, buf_b)
        return carry

    lax.fori_loop(0, i // 2, pair, 0)

    @pl.when(i % 2 == 0)
    def _():
        consume(i, buf_a, diagonal=True)

    @pl.when(i % 2 == 1)
    def _():
        produce(i, buf_b)
        consume(i - 1, buf_a)
        consume(i, buf_b, diagonal=True)


def _transpose_to(dst_ref, src_ref, n_rows):
    for c in range(src_ref.shape[1] // LANES):
        def body(t, carry):
            r0 = pl.multiple_of(t * LANES, LANES)
            tile = src_ref[pl.ds(r0, LANES), c * LANES:(c + 1) * LANES].astype(F32)
            dst_ref[c * LANES:(c + 1) * LANES, pl.ds(r0, LANES)] = tile.T.astype(dst_ref.dtype)
            return carry
        lax.fori_loop(0, n_rows // LANES, body, 0)


def _init_softmax_state(m_ref, l_ref, a_ref):
    m_ref[...] = jnp.full(m_ref.shape, MASK_VALUE, F32)
    l_ref[...] = jnp.zeros(l_ref.shape, F32)
    a_ref[...] = jnp.zeros(a_ref.shape, F32)


def _diff_attn_kernel(lam_ref, q_ref, k_ref, v_ref, g_ref, o_ref, vt_ref, sa_ref, sb_ref,
                      m0, l0, a0, m1, l1, a1, *, tb, n_lead, seq_len, out_scale):
    i = pl.program_id(2)

    @pl.when(i == 0)
    def _():
        _transpose_to(vt_ref, v_ref, seq_len)

    qt = q_ref[...].astype(F32).T
    row = lax.broadcasted_iota(I32, (LANES, 1), 0)
    qt0 = jnp.where(row < DIFF_QK_DIM, qt, 0.0).astype(BF16)
    qt1 = jnp.where(row >= DIFF_QK_DIM, qt, 0.0).astype(BF16)
    _init_softmax_state(m0, l0, a0)
    _init_softmax_state(m1, l1, a1)

    def produce(j, buf, pad_bias=False):
        start = pl.multiple_of(j * tb, tb)
        k = k_ref[pl.ds(start, tb), :]
        for half, qth in enumerate((qt0, qt1)):
            s = jnp.dot(k, qth, preferred_element_type=F32)
            if pad_bias:
                kpos = lax.broadcasted_iota(I32, (tb, 1), 0)
                s = s + jnp.where(kpos >= n_lead, 0.0, MASK_VALUE)
            buf[half] = s

    def consume(j, buf, diagonal=False):
        start = pl.multiple_of(j * tb, tb)
        vt = vt_ref[:, pl.ds(start, tb)]
        for half, state in enumerate(((m0, l0, a0), (m1, l1, a1))):
            s = buf[half]
            if diagonal:
                kpos = j * tb + lax.broadcasted_iota(I32, (tb, 1), 0)
                qpos = i * tb + lax.broadcasted_iota(I32, (1, tb), 1)
                s = jnp.where(kpos >> CHUNK_SHIFT <= qpos >> CHUNK_SHIFT, s, MASK_VALUE)
            _online_softmax_step(s, vt, *state)

    _pipelined_blocks(i, sa_ref, sb_ref, functools.partial(produce, pad_bias=True), produce, consume)

    lam = lam_ref[0]
    o = a0[...] / l0[...] - lam * (a1[...] / l1[...])
    o = o * lax.rsqrt(jnp.mean(o * o, axis=0, keepdims=True) + 1e-5) * g_ref[...] * out_scale
    o_ref[...] = o.T.astype(o_ref.dtype)


def _diff_attention(attn, lam, subln_g, seq_len, n_lead, lambda_init, tb):
    t_all = attn.shape[0]
    bsz = t_all // seq_len
    nq = seq_len // tb
    kcol = DIFF_QK_WIDTH // LANES
    vcol = 2 * DIFF_QK_WIDTH // LANES
    scratch = [pltpu.VMEM((DIFF_V_DIM, seq_len), BF16),
               pltpu.VMEM((2, tb, tb), F32), pltpu.VMEM((2, tb, tb), F32)]
    for _ in range(2):
        scratch += [pltpu.VMEM((1, tb), F32), pltpu.VMEM((1, tb), F32), pltpu.VMEM((DIFF_V_DIM, tb), F32)]
    return pl.pallas_call(
        functools.partial(_diff_attn_kernel, tb=tb, n_lead=n_lead, seq_len=seq_len,
                          out_scale=1.0 - lambda_init),
        out_shape=jax.ShapeDtypeStruct((t_all, DIFF_V_WIDTH), BF16),
        grid_spec=pltpu.PrefetchScalarGridSpec(
            num_scalar_prefetch=1,
            grid=(bsz, DIFF_HEADS, nq),
            in_specs=[pl.BlockSpec((tb, LANES), lambda b, h, i, lam: (b * nq + i, h)),
                      pl.BlockSpec((seq_len, LANES), lambda b, h, i, lam: (b, kcol + h)),
                      pl.BlockSpec((seq_len, LANES), lambda b, h, i, lam: (b, vcol + h)),
                      pl.BlockSpec((DIFF_V_DIM, 1), lambda b, h, i, lam: (0, 0))],
            out_specs=pl.BlockSpec((tb, DIFF_V_DIM), lambda b, h, i, lam: (b * nq + i, h)),
            scratch_shapes=scratch),
        compiler_params=_cparams("arbitrary", "arbitrary", "arbitrary"),
        name="diff_attention",
    )(lam, attn, attn, attn, subln_g.reshape(DIFF_V_DIM, 1))


def _fox_bias_lane(hh):
    return (1 - hh % 2) * FOX_HEAD_DIM + FOX_BIAS_PIECES * (hh // 2)


def _fox_attn_kernel(q_ref, k_ref, v_ref, kb_ref, o_ref, vt_ref, sa_ref, sb_ref, *scratch, tb, seq_len):
    i = pl.program_id(1)
    pairs = FOX_HEADS // 2

    @pl.when(i == 0)
    def _():
        _transpose_to(vt_ref, v_ref, seq_len)

    row = lax.broadcasted_iota(I32, (LANES, 1), 0)
    lane = lax.broadcasted_iota(I32, (1, LANES), 1)
    qts = []
    for p in range(pairs):
        qt = q_ref[:, p * LANES:(p + 1) * LANES].astype(F32).T
        for r in range(2):
            own = jnp.logical_and(row >= r * FOX_HEAD_DIM, row < (r + 1) * FOX_HEAD_DIM)
            b0 = _fox_bias_lane(2 * p + r)
            ones = jnp.logical_and(row >= b0, row < b0 + FOX_BIAS_PIECES)
            qts.append(jnp.where(own, qt, jnp.where(ones, 1.0, 0.0)).astype(BF16))
    for hh in range(FOX_HEADS):
        _init_softmax_state(*scratch[3 * hh:3 * hh + 3])

    def produce(j, buf):
        start = pl.multiple_of(j * tb, tb)
        kb = kb_ref[pl.ds(start, tb), :]
        for p in range(pairs):
            k = k_ref[pl.ds(start, tb), p * LANES:(p + 1) * LANES]
            for r in range(2):
                own = jnp.logical_and(lane >= r * FOX_HEAD_DIM, lane < (r + 1) * FOX_HEAD_DIM)
                buf[2 * p + r] = jnp.dot(jnp.where(own, k, kb), qts[2 * p + r],
                                         preferred_element_type=F32)

    def consume(j, buf, diagonal=False):
        start = pl.multiple_of(j * tb, tb)
        for hh in range(FOX_HEADS):
            vt = vt_ref[hh * FOX_HEAD_DIM:(hh + 1) * FOX_HEAD_DIM, pl.ds(start, tb)]
            s = buf[hh]
            if diagonal:
                kpos = j * tb + lax.broadcasted_iota(I32, (tb, 1), 0)
                qpos = i * tb + lax.broadcasted_iota(I32, (1, tb), 1)
                s = jnp.where(kpos <= qpos, s, MASK_VALUE)
            _online_softmax_step(s, vt, *scratch[3 * hh:3 * hh + 3])

    _pipelined_blocks(i, sa_ref, sb_ref, produce, produce, consume)

    for p in range(pairs):
        _, l_lo, a_lo = scratch[6 * p:6 * p + 3]
        _, l_hi, a_hi = scratch[6 * p + 3:6 * p + 6]
        o = jnp.concatenate([a_lo[...] / l_lo[...], a_hi[...] / l_hi[...]], axis=0)
        o_ref[:, p * LANES:(p + 1) * LANES] = o.T.astype(o_ref.dtype)


def _fox_attention(attn, key_bias, seq_len, tb):
    t_all = attn.shape[0]
    bsz = t_all // seq_len
    nq = seq_len // tb
    base = (2 * DIFF_QK_WIDTH + DIFF_V_WIDTH) // FOX_WIDTH
    scratch = [pltpu.VMEM((FOX_WIDTH, seq_len), BF16),
               pltpu.VMEM((FOX_HEADS, tb, tb), F32), pltpu.VMEM((FOX_HEADS, tb, tb), F32)]
    for _ in range(FOX_HEADS):
        scratch += [pltpu.VMEM((1, tb), F32), pltpu.VMEM((1, tb), F32), pltpu.VMEM((FOX_HEAD_DIM, tb), F32)]
    return pl.pallas_call(
        functools.partial(_fox_attn_kernel, tb=tb, seq_len=seq_len),
        out_shape=jax.ShapeDtypeStruct((t_all, FOX_WIDTH), BF16),
        grid=(bsz, nq),
        in_specs=[pl.BlockSpec((tb, FOX_WIDTH), lambda b, i: (b * nq + i, base)),
                  pl.BlockSpec((seq_len, FOX_WIDTH), lambda b, i: (b, base + 1)),
                  pl.BlockSpec((seq_len, FOX_WIDTH), lambda b, i: (b, base + 2)),
                  pl.BlockSpec((seq_len, LANES), lambda b, i: (b, 0))],
        out_specs=pl.BlockSpec((tb, FOX_WIDTH), lambda b, i: (b * nq + i, 0)),
        scratch_shapes=scratch,
        compiler_params=_cparams("arbitrary", "arbitrary"),
        name="fox_attention",
    )(attn, attn, attn, key_bias)


def _gdn_kernel(x_ref, gate_ref, cw_ref, alog_ref, dtb_ref, ng_ref, o_ref,
                halo_ref, xs_ref, state_ref, *, rows):
    cg = pl.program_id(1)
    g_chunks = rows // CHUNK
    hd = GDN_HEAD_DIM

    @pl.when(cg == 0)
    def _():
        halo_ref[...] = jnp.zeros_like(halo_ref)
        state_ref[...] = jnp.zeros_like(state_ref)

    xin = x_ref[:, :GDN_CONV_CH]
    xs_ref[0:SUBLANES, :] = halo_ref[...]
    xs_ref[SUBLANES:SUBLANES + rows, :] = xin
    halo_ref[...] = xin[rows - SUBLANES:rows, :]
    y = jnp.zeros((rows, GDN_CONV_CH), F32)
    for tap in range(CONV_K):
        back = CONV_K - 1 - tap
        y = y + cw_ref[tap:tap + 1, :] * xs_ref[SUBLANES - back:SUBLANES - back + rows, :]
    y = y * _sigmoid(y)

    gates = gate_ref[...]
    beta_all = _sigmoid(gates)
    g_all = -jnp.exp(alog_ref[...]) * _softplus(gates + dtb_ref[...])
    rowi = lax.broadcasted_iota(I32, (rows, 1), 0) & (CHUNK - 1)
    gc_all = g_all
    shift = 1
    while shift < CHUNK:
        gc_all = gc_all + jnp.where(rowi >= shift, pltpu.roll(gc_all, shift, 0), 0.0)
        shift *= 2

    ri = lax.broadcasted_iota(I32, (CHUNK, CHUNK), 0)
    ci = lax.broadcasted_iota(I32, (CHUNK, CHUNK), 1)
    eye = jnp.where(ri == ci, 1.0, 0.0).astype(F32)
    causal = (ci <= ri)[None]
    strict = (ci < ri)[None]
    eye_b = jnp.broadcast_to(eye[None], (g_chunks, CHUNK, CHUNK))

    prep = []
    for h in range(GDN_HEADS):
        def head3(base):
            return y[:, base + h * hd:base + (h + 1) * hd]
        q = head3(0)
        k = head3(GDN_WIDTH)
        v = head3(2 * GDN_WIDTH)
        q = q * lax.rsqrt(jnp.sum(q * q, axis=-1, keepdims=True) + 1e-6) * (hd ** -0.5)
        k = k * lax.rsqrt(jnp.sum(k * k, axis=-1, keepdims=True) + 1e-6)
        beta = beta_all[:, GATE_BETA + h:GATE_BETA + h + 1]
        gcol = jnp.broadcast_to(gc_all[:, GATE_DECAY + h:GATE_DECAY + h + 1], (rows, hd))
        egc = jnp.exp(gcol)

        q3 = q.reshape(g_chunks, CHUNK, hd)
        k3 = k.reshape(g_chunks, CHUNK, hd)
        gcol3 = gcol.reshape(g_chunks, CHUNK, CHUNK)
        beta3 = jnp.broadcast_to(beta, (rows, hd)).reshape(g_chunks, CHUNK, hd)
        grow3 = jnp.swapaxes(gcol3, 1, 2)
        decay = jnp.exp(jnp.where(causal, gcol3 - grow3, MASK_VALUE))
        a = jnp.where(strict, _bdot_nt(k3, k3) * beta3 * decay, 0.0)

        x = -a
        tinv = eye_b + x
        span = 2
        while span < CHUNK:
            x = _bdot(x, x)
            tinv = tinv + _bdot(tinv, x)
            span *= 2

        vb = (v * beta).reshape(g_chunks, CHUNK, hd)
        kb = (k * beta * egc).reshape(g_chunks, CHUNK, hd)
        u3 = _bdot(tinv, vb)
        w3 = _bdot(tinv, kb)
        qk3 = _bdot_nt(q3, k3) * decay
        qd3 = (q * egc).reshape(g_chunks, CHUNK, hd)
        glast3 = gcol3[:, CHUNK - 1:CHUNK, :]
        kdt3 = jnp.swapaxes(k3 * jnp.exp(glast3 - gcol3), 1, 2)
        cd3 = jnp.exp(glast3)
        prep.append((_bdot(kdt3, w3), _bdot(kdt3, u3), qd3 - _bdot(qk3, w3), _bdot(qk3, u3), cd3))

    states = [state_ref[h] for h in range(GDN_HEADS)]
    outs = [[] for _ in range(GDN_HEADS)]
    for c in range(g_chunks):
        for h in range(GDN_HEADS):
            kw3, ku3, qp3, o03, cd3 = prep[h]
            state = states[h]
            outs[h].append(_dot(qp3[c], state) + o03[c])
            states[h] = state * cd3[c] + (ku3[c] - _dot(kw3[c], state))

    for h in range(GDN_HEADS):
        state_ref[h] = states[h]
        o = jnp.concatenate(outs[h], axis=0)
        o = o * lax.rsqrt(jnp.mean(o * o, axis=-1, keepdims=True) + 1e-6) * ng_ref[...]
        z = x_ref[:, GDN_CONV_CH + h * hd:GDN_CONV_CH + (h + 1) * hd]
        o_ref[:, h * hd:(h + 1) * hd] = (o * (z * _sigmoid(z))).astype(o_ref.dtype)


def _gated_deltanet(gdn_slab, gates, conv_w, a_log_lanes, dt_bias_lanes, norm_g, seq_len, rows):
    t_all = gdn_slab.shape[0]
    bsz = t_all // seq_len
    nb = seq_len // rows
    row = lambda b, i: (b * nb + i, 0)
    const = lambda b, i: (0, 0)
    return pl.pallas_call(
        functools.partial(_gdn_kernel, rows=rows),
        out_shape=jax.ShapeDtypeStruct((t_all, GDN_WIDTH), BF16),
        grid=(bsz, nb),
        in_specs=[pl.BlockSpec((rows, GDN_SLAB), row),
                  pl.BlockSpec((rows, LANES), row),
                  pl.BlockSpec((SUBLANES, GDN_CONV_CH), const),
                  pl.BlockSpec((1, LANES), const),
                  pl.BlockSpec((1, LANES), const),
                  pl.BlockSpec((1, GDN_HEAD_DIM), const)],
        out_specs=pl.BlockSpec((rows, GDN_WIDTH), row),
        scratch_shapes=[pltpu.VMEM((SUBLANES, GDN_CONV_CH), F32),
                        pltpu.VMEM((rows + SUBLANES, GDN_CONV_CH), F32),
                        pltpu.VMEM((GDN_HEADS, GDN_HEAD_DIM, GDN_HEAD_DIM), F32)],
        compiler_params=_cparams("arbitrary", "arbitrary"),
        name="gated_deltanet",
    )(gdn_slab, gates, conv_w, a_log_lanes, dt_bias_lanes, norm_g.reshape(1, GDN_HEAD_DIM))


def _mix_router_kernel(x_ref, yd_ref, yf_ref, yg_ref, wd_ref, wf_ref, wg_ref, g_ref, b_ref,
                       rw_ref, rb_ref, x1_ref, lr_ref, lrt_ref, ge_ref, cnt_ref, *,
                       tm, rb_rows, alpha, seq_len, n_lead, n_seq):
    mix = (jnp.dot(yd_ref[...], wd_ref[...], preferred_element_type=F32)
           + jnp.dot(yf_ref[...], wf_ref[...], preferred_element_type=F32)
           + jnp.dot(yg_ref[...], wg_ref[...], preferred_element_type=F32))
    x1 = _layer_norm(alpha * x_ref[...] + mix, g_ref[...], b_ref[...])
    x1_ref[...] = x1

    lane = lax.broadcasted_iota(I32, (tm, LANES), 1)
    lane_f = lane.astype(F32)
    logits = _dot3(x1, rw_ref[...]) + rb_ref[...]
    vals = jnp.where(lane < N_EXPERTS, logits, -jnp.inf)
    hits, tops = [], []
    for _ in range(TOP_K):
        top = jnp.max(vals, axis=-1, keepdims=True)
        idx = jnp.min(jnp.where(vals == top, lane_f, float(LANES)), axis=-1, keepdims=True)
        hit = lane_f == idx
        vals = jnp.where(hit, -jnp.inf, vals)
        hits.append(hit)
        tops.append(top)
    exps = [jnp.exp(t - tops[0]) for t in tops]
    denom = exps[0] + exps[1] + exps[2] + exps[3]
    row = pl.program_id(0) * tm + lax.broadcasted_iota(I32, (tm, 1), 0)
    is_pad = row < 0
    for sq in range(n_seq):
        is_pad = jnp.logical_or(is_pad, jnp.logical_and(row >= sq * seq_len, row < sq * seq_len + n_lead))
    onehot = jnp.zeros((tm, LANES), F32)
    gate_by_expert = jnp.zeros((tm, LANES), F32)
    for kk in range(TOP_K):
        hit = jnp.logical_and(hits[kk], jnp.logical_not(is_pad))
        onehot = jnp.where(hit, 1.0, onehot)
        gate_by_expert = jnp.where(hit, exps[kk] / denom, gate_by_expert)
    ge_ref[...] = gate_by_expert

    r = lax.broadcasted_iota(I32, (rb_rows, rb_rows), 0)
    c = lax.broadcasted_iota(I32, (rb_rows, rb_rows), 1)
    tri = jnp.where(c < r, 1.0, 0.0).astype(BF16)
    for blk in range(tm // rb_rows):
        oh = onehot[blk * rb_rows:(blk + 1) * rb_rows, :]
        rank = jnp.dot(tri, oh.astype(BF16), preferred_element_type=F32)
        local = jnp.where(oh > 0.0, rank, -1.0)
        lr_ref[blk * rb_rows:(blk + 1) * rb_rows, :] = local
        lrt_ref[blk * N_EXPERTS:(blk + 1) * N_EXPERTS, :] = local.T[:N_EXPERTS, :]
        cnt_ref[blk] = jnp.sum(oh, axis=0, keepdims=True)


def _mix_router(stream, yd, yf, yg, w_out_b, ln_g, ln_b, rw, rb, alpha, rb_rows, seq_len, n_lead):
    t_all, d = stream.shape
    n_rb = t_all // rb_rows
    per_step = max(k for k in (5, 4, 3, 2, 1) if n_rb % k == 0)
    tm = per_step * rb_rows
    row = lambda i: (i, 0)
    const = lambda i: (0, 0)
    wd = w_out_b[:DIFF_V_WIDTH]
    wf = w_out_b[DIFF_V_WIDTH:DIFF_V_WIDTH + FOX_WIDTH]
    wg = w_out_b[DIFF_V_WIDTH + FOX_WIDTH:]
    return pl.pallas_call(
        functools.partial(_mix_router_kernel, tm=tm, rb_rows=rb_rows, alpha=alpha, seq_len=seq_len,
                          n_lead=n_lead, n_seq=t_all // seq_len),
        out_shape=(jax.ShapeDtypeStruct((t_all, d), F32),
                   jax.ShapeDtypeStruct((t_all, LANES), F32),
                   jax.ShapeDtypeStruct((n_rb * N_EXPERTS, rb_rows), F32),
                   jax.ShapeDtypeStruct((t_all, LANES), F32),
                   jax.ShapeDtypeStruct((n_rb, 1, LANES), F32)),
        grid=(t_all // tm,),
        in_specs=[pl.BlockSpec((tm, d), row),
                  pl.BlockSpec((tm, DIFF_V_WIDTH), row),
                  pl.BlockSpec((tm, FOX_WIDTH), row),
                  pl.BlockSpec((tm, GDN_WIDTH), row),
                  pl.BlockSpec((DIFF_V_WIDTH, d), const),
                  pl.BlockSpec((FOX_WIDTH, d), const),
                  pl.BlockSpec((GDN_WIDTH, d), const),
                  pl.BlockSpec((1, d), const),
                  pl.BlockSpec((1, d), const),
                  pl.BlockSpec((d, LANES), const),
                  pl.BlockSpec((1, LANES), const)],
        out_specs=(pl.BlockSpec((tm, d), row),
                   pl.BlockSpec((tm, LANES), row),
                   pl.BlockSpec((per_step * N_EXPERTS, rb_rows), row),
                   pl.BlockSpec((tm, LANES), row),
                   pl.BlockSpec((per_step, 1, LANES), lambda i: (i, 0, 0))),
        compiler_params=_cparams("arbitrary"),
        name="mix_router",
    )(stream, yd, yf, yg, wd, wf, wg, ln_g.reshape(1, d), ln_b.reshape(1, d), rw, rb)


SLAB = 64
GROUP_ALIGN = 8
ZERO_FILL_SIZES = (512, 256, 128, 64, 32, 16, 8)


def _slab_rows(src_ref, rank_row0):
    want = (rank_row0 + lax.broadcasted_iota(I32, (SLAB, 1), 0)).astype(F32)
    return jnp.concatenate([jnp.where(src_ref[e:e + 1, :] == want, 1.0, 0.0).astype(BF16)
                            for e in range(N_EXPERTS)], axis=0)


def _dispatch_kernel(gstart_ref, gcnt_ref, npass_ref, zstart_ref, zrows_ref, x_ref, lrt_ref, xs_ref,
                     zbuf, zero_ref, sem, zsem, *, nb, n_slots):
    b = pl.program_id(0)
    slot = b % 2

    def slab_copy(bb, c, e, s):
        start = pl.multiple_of(gstart_ref[bb * N_EXPERTS + e] + c * SLAB, GROUP_ALIGN)
        return pltpu.make_async_copy(zbuf.at[s, pl.ds(e * SLAB, SLAB)], xs_ref.at[pl.ds(start, SLAB)], sem.at[s])

    def zero_fill(act):
        for e in range(N_EXPERTS):
            rows = zrows_ref[e]
            for size in ZERO_FILL_SIZES:
                @pl.when((rows & size) != 0)
                def _():
                    start = pl.multiple_of(zstart_ref[e] + (rows & ~(2 * size - 1)), GROUP_ALIGN)
                    act(pltpu.make_async_copy(zero_ref.at[pl.ds(0, size)], xs_ref.at[pl.ds(start, size)], zsem))

    def zero_tail(act):
        first = zstart_ref[N_EXPERTS - 1] + zrows_ref[N_EXPERTS - 1]
        size = ZERO_FILL_SIZES[0]

        def body(t, carry):
            start = pl.multiple_of(first + t * size, GROUP_ALIGN)
            act(pltpu.make_async_copy(zero_ref, xs_ref.at[pl.ds(start, size)], zsem))
            return carry

        lax.fori_loop(0, (n_slots - first) // size, body, 0)

    @pl.when(b == 0)
    def _():
        zero_ref[...] = jnp.zeros_like(zero_ref)
        for fill in (zero_fill, zero_tail):
            fill(lambda cp: cp.start())
        for fill in (zero_fill, zero_tail):
            fill(lambda cp: cp.wait())

    xb = x_ref[...].astype(BF16)
    zbuf[slot] = jnp.dot(_slab_rows(lrt_ref, 0), xb, preferred_element_type=F32)

    @pl.when(b > 0)
    def _():
        @pl.when(npass_ref[b - 1] <= 1)
        def _():
            for e in range(N_EXPERTS):
                slab_copy(b - 1, 0, e, 1 - slot).wait()

    for e in range(N_EXPERTS):
        slab_copy(b, 0, e, slot).start()

    @pl.when(npass_ref[b] > 1)
    def _():
        for e in range(N_EXPERTS):
            slab_copy(b, 0, e, slot).wait()

        def extra_pass(c, carry):
            zbuf[slot] = jnp.dot(_slab_rows(lrt_ref, c * SLAB), xb, preferred_element_type=F32)
            for action in ("start", "wait"):
                for e in range(N_EXPERTS):
                    @pl.when(gcnt_ref[b * N_EXPERTS + e] > c * SLAB)
                    def _():
                        getattr(slab_copy(b, c, e, slot), action)()
            return carry

        lax.fori_loop(1, npass_ref[b], extra_pass, 0)

    @pl.when(b == nb - 1)
    def _():
        @pl.when(npass_ref[b] <= 1)
        def _():
            for e in range(N_EXPERTS):
                slab_copy(b, 0, e, slot).wait()


def _dispatch(tables, x1, lrt, n_slots, tm):
    t_all, d = x1.shape
    nb = t_all // tm
    ntab = len(tables)
    return pl.pallas_call(
        functools.partial(_dispatch_kernel, nb=nb, n_slots=n_slots),
        out_shape=jax.ShapeDtypeStruct((n_slots, d), F32),
        grid_spec=pltpu.PrefetchScalarGridSpec(
            num_scalar_prefetch=ntab,
            grid=(nb,),
            in_specs=[pl.BlockSpec((tm, d), lambda i, *_: (i, 0)),
                      pl.BlockSpec((N_EXPERTS, tm), lambda i, *_: (i, 0))],
            out_specs=pl.BlockSpec(memory_space=pl.ANY),
            scratch_shapes=[pltpu.VMEM((2, N_EXPERTS * SLAB, d), F32),
                            pltpu.VMEM((ZERO_FILL_SIZES[0], d), F32),
                            pltpu.SemaphoreType.DMA((2,)),
                            pltpu.SemaphoreType.DMA]),
        compiler_params=_cparams("arbitrary"),
        name="moe_dispatch",
    )(*tables, x1, lrt)


def _expert_kernel(be_ref, na_ref, x_ref, w1_ref, b1_ref, w2_ref, b2_ref, o_ref, w1b_ref, w2b_ref, *, d_expert):
    i = pl.program_id(0)
    active = i < na_ref[0]
    new_expert = jnp.logical_or(i == 0, be_ref[i] != be_ref[jnp.maximum(i - 1, 0)])

    @pl.when(jnp.logical_and(active, new_expert))
    def _():
        w1b_ref[...] = w1_ref[0].astype(BF16)
        w2b_ref[...] = w2_ref[0].astype(BF16)

    @pl.when(active)
    def _():
        hid = jnp.dot(x_ref[...].astype(BF16), w1b_ref[...], preferred_element_type=F32) + b1_ref[0]
        glu = jnp.minimum(hid[:, :d_expert], SWIGLU_LIMIT)
        lin = jnp.clip(hid[:, d_expert:], -SWIGLU_LIMIT, SWIGLU_LIMIT)
        act = glu * _sigmoid(SWIGLU_ALPHA * glu) * (lin + 1.0)
        o_ref[...] = jnp.dot(act.astype(BF16), w2b_ref[...], preferred_element_type=F32) + b2_ref[0]

    @pl.when(jnp.logical_not(active))
    def _():
        o_ref[...] = jnp.zeros_like(o_ref)


def _experts(block_expert, n_active, xs, w1_all, b1, w2_all, b2, layer, mb):
    n_slots, d = xs.shape
    depth, n_exp, _, f2 = w1_all.shape
    f = f2 // 2
    nb = n_slots // mb
    w1 = w1_all.reshape(depth * n_exp, d, f2)
    w2 = w2_all.reshape(depth * n_exp, f, d)
    block_expert = block_expert + layer * n_exp
    return pl.pallas_call(
        functools.partial(_expert_kernel, d_expert=f),
        out_shape=jax.ShapeDtypeStruct((n_slots, d), F32),
        grid_spec=pltpu.PrefetchScalarGridSpec(
            num_scalar_prefetch=2,
            grid=(nb,),
            in_specs=[pl.BlockSpec((mb, d), lambda i, be, na: (jnp.minimum(i, na[0] - 1), 0)),
                      pl.BlockSpec((1, d, f2), lambda i, be, na: (be[i], 0, 0)),
                      pl.BlockSpec((1, 1, f2), lambda i, be, na: (be[i], 0, 0)),
                      pl.BlockSpec((1, f, d), lambda i, be, na: (be[i], 0, 0)),
                      pl.BlockSpec((1, 1, d), lambda i, be, na: (be[i], 0, 0))],
            out_specs=pl.BlockSpec((mb, d), lambda i, be, na: (i, 0)),
            scratch_shapes=[pltpu.VMEM((d, f2), BF16), pltpu.VMEM((f, d), BF16)]),
        compiler_params=_cparams("arbitrary"),
        name="moe_experts",
    )(block_expert, n_active, xs, w1, b1.reshape(depth * n_exp, 1, f2), w2, b2.reshape(depth * n_exp, 1, d))


def _combine_kernel(gstart_ref, gcnt_ref, npass_ref, x1_ref, lr_ref, ge_ref, g_ref, b_ref, ys_ref, o_ref,
                    ybuf, acc_ref, sem, *, tm, nb, alpha):
    b = pl.program_id(0)
    slot = b % 2

    def slab_copy(bb, c, e, s):
        start = pl.multiple_of(gstart_ref[bb * N_EXPERTS + e] + c * SLAB, GROUP_ALIGN)
        return pltpu.make_async_copy(ys_ref.at[pl.ds(start, SLAB)], ybuf.at[s, pl.ds(e * SLAB, SLAB)], sem.at[s])

    @pl.when(b == 0)
    def _():
        for e in range(N_EXPERTS):
            slab_copy(0, 0, e, 0).start()

    @pl.when(b + 1 < nb)
    def _():
        for e in range(N_EXPERTS):
            slab_copy(b + 1, 0, e, 1 - slot).start()

    for e in range(N_EXPERTS):
        slab_copy(b, 0, e, slot).wait()

    lane = lax.broadcasted_iota(I32, (1, LANES), 1)
    lr = lr_ref[...]
    ge = ge_ref[...]

    def weighted_sum(rank0):
        his, los = [], []
        for col0 in range(0, N_EXPERTS * SLAB, LANES):
            col = col0 + lane
            rank = jnp.full((tm, LANES), -2.0, F32)
            gate = jnp.zeros((tm, LANES), F32)
            want = jnp.zeros((1, LANES), I32)
            for e in range(col0 // SLAB, min((col0 + LANES - 1) // SLAB, N_EXPERTS - 1) + 1):
                inside = jnp.logical_and(col >= e * SLAB, col < (e + 1) * SLAB)
                rank = jnp.where(inside, lr[:, e:e + 1], rank)
                gate = jnp.where(inside, ge[:, e:e + 1], gate)
                want = jnp.where(inside, col - e * SLAB + rank0, want)
            w = jnp.where(rank == want.astype(F32), gate, 0.0)
            hi = w.astype(BF16)
            his.append(hi)
            los.append((w - hi.astype(F32)).astype(BF16))
        sel = jnp.concatenate([jnp.concatenate(his, axis=1), jnp.concatenate(los, axis=1)], axis=0)
        r = jnp.dot(sel, ybuf[slot].astype(BF16), preferred_element_type=F32)
        return r[:tm] + r[tm:]

    acc_ref[...] = alpha * x1_ref[...] + weighted_sum(0)

    @pl.when(npass_ref[b] > 1)
    def _():
        def extra_pass(c, carry):
            for action in ("start", "wait"):
                for e in range(N_EXPERTS):
                    @pl.when(gcnt_ref[b * N_EXPERTS + e] > c * SLAB)
                    def _():
                        getattr(slab_copy(b, c, e, slot), action)()
            acc_ref[...] += weighted_sum(c * SLAB)
            return carry

        lax.fori_loop(1, npass_ref[b], extra_pass, 0)

    o_ref[...] = _layer_norm(acc_ref[...], g_ref[...], b_ref[...])


def _combine(tables, x1, lr, ge, ln_g, ln_b, ys, alpha, tm):
    t_all, d = x1.shape
    nb = t_all // tm
    ntab = len(tables)
    row = lambda i, *_: (i, 0)
    const = lambda i, *_: (0, 0)
    return pl.pallas_call(
        functools.partial(_combine_kernel, tm=tm, nb=nb, alpha=alpha),
        out_shape=jax.ShapeDtypeStruct((t_all, d), F32),
        grid_spec=pltpu.PrefetchScalarGridSpec(
            num_scalar_prefetch=ntab,
            grid=(nb,),
            in_specs=[pl.BlockSpec((tm, d), row),
                      pl.BlockSpec((tm, LANES), row),
                      pl.BlockSpec((tm, LANES), row),
                      pl.BlockSpec((1, d), const),
                      pl.BlockSpec((1, d), const),
                      pl.BlockSpec(memory_space=pl.ANY)],
            out_specs=pl.BlockSpec((tm, d), row),
            scratch_shapes=[pltpu.VMEM((2, N_EXPERTS * SLAB, d), F32),
                            pltpu.VMEM((tm, d), F32),
                            pltpu.SemaphoreType.DMA((2,))]),
        compiler_params=_cparams("arbitrary"),
        name="moe_combine",
    )(*tables, x1, lr, ge, ln_g.reshape(1, d), ln_b.reshape(1, d), ys)


def _lanes(values, offset):
    return jnp.zeros((1, LANES), F32).at[0, offset:offset + values.shape[0]].set(values.astype(F32))


def _split_in_proj(w_in_layer):
    offs = np.concatenate([[0], np.cumsum(IN_PROJ_SIZES)])
    seg = lambda n: w_in_layer[:, offs[n]:offs[n + 1]]
    qk_scale = DIFF_QK_DIM ** -0.5 * LOG2_E
    fox_scale = FOX_HEAD_DIM ** -0.5 * LOG2_E
    wa = jnp.concatenate([seg(0) * qk_scale, seg(1), seg(2), seg(3) * fox_scale, seg(4), seg(5)], axis=1)
    wg = jnp.concatenate([seg(7), seg(10)], axis=1)
    d = w_in_layer.shape[0]
    ws = jnp.zeros((d, LANES), F32)
    ws = ws.at[:, GATE_FF:GATE_FF + FOX_HEADS].set(seg(6))
    ws = ws.at[:, GATE_BETA:GATE_BETA + GDN_HEADS].set(seg(8))
    ws = ws.at[:, GATE_DECAY:GATE_DECAY + GDN_HEADS].set(seg(9))
    return wa.astype(BF16), wg.astype(BF16), ws


def _diff_lambda_init(layer):
    return 0.8 - 0.6 * math.exp(-0.3 * layer)


def _pick_block(seq_len, candidates):
    for c in candidates:
        if seq_len % c == 0:
            return c
    raise ValueError(f"unsupported sequence length {seq_len}")


MOE_ROWS = 512
TOKEN_BLOCK = 256
assert MOE_ROWS % ZERO_FILL_SIZES[0] == 0 and SLAB + MOE_ROWS - GROUP_ALIGN < 2 * ZERO_FILL_SIZES[0]


def _layer(stream, layer, seq_len, n_lead, alpha, p):
    t_all, d = stream.shape
    attn_tb = _pick_block(seq_len, (640, 512, 256, 128))
    wa, wg, ws = _split_in_proj(p["w_in"])
    attn, gdn_slab, gates, key_bias = _in_proj(stream, wa, wg, ws, _lanes(p["fox_forget_b"], GATE_FF),
                                               seq_len, n_lead, attn_tb)

    lv = p["diff_lambda"].astype(F32)
    lambda_init = _diff_lambda_init(layer)
    lam = (jnp.exp(jnp.sum(lv[0] * lv[1])) - jnp.exp(jnp.sum(lv[2] * lv[3])) + lambda_init).reshape(1)
    y_diff = _diff_attention(attn, lam, p["diff_subln_g"], seq_len, n_lead, lambda_init, attn_tb)

    y_fox = _fox_attention(attn, key_bias, seq_len, attn_tb)

    conv_w = jnp.zeros((SUBLANES, GDN_CONV_CH), F32).at[:CONV_K].set(p["gdn_conv_w"])
    y_gdn = _gated_deltanet(gdn_slab, gates, conv_w, _lanes(p["gdn_a_log"], GATE_DECAY),
                            _lanes(p["gdn_dt_bias"], GATE_DECAY), p["gdn_norm_g"], seq_len, attn_tb)

    rw = jnp.zeros((d, LANES), F32).at[:, :N_EXPERTS].set(p["router_w"])
    rb = _lanes(p["router_b"], 0)
    tm = TOKEN_BLOCK
    x1, lr, lrt, ge, counts = _mix_router(stream, y_diff, y_fox, y_gdn, p["w_out"].astype(BF16),
                                          p["ln1_g"], p["ln1_b"], rw, rb, alpha, tm, seq_len, n_lead)

    mb = MOE_ROWS
    nb = t_all // tm
    n_slots = -(-(t_all * TOP_K + (GROUP_ALIGN - 1) * nb * N_EXPERTS
                  + N_EXPERTS * (SLAB + mb - GROUP_ALIGN)) // mb) * mb
    n_blocks = n_slots // mb
    cnt = counts[:, 0, :N_EXPERTS].astype(I32)
    rows = (cnt + GROUP_ALIGN - 1) // GROUP_ALIGN * GROUP_ALIGN
    total = jnp.sum(rows, axis=0)
    region = (total + SLAB + mb - 1) // mb * mb
    pad_end = jnp.cumsum(region)
    pad_start = pad_end - region
    gstart = (pad_start[None, :] + jnp.cumsum(rows, axis=0) - rows).reshape(-1)
    npass = jnp.maximum((jnp.max(cnt, axis=1) + SLAB - 1) // SLAB, 1)
    block_first = jnp.arange(n_blocks, dtype=I32) * mb
    block_expert = jnp.minimum(jnp.sum((pad_end[None, :] <= block_first[:, None]).astype(I32), axis=1),
                               N_EXPERTS - 1)
    n_active = (pad_end[-1] // mb).astype(I32).reshape(1)
    group_tables = (gstart.astype(I32), cnt.reshape(-1), npass.astype(I32))

    xs = _dispatch(group_tables + ((pad_start + total).astype(I32), (region - total).astype(I32)),
                   x1, lrt, n_slots, tm)
    ys = _experts(block_expert, n_active, xs, p["expert_w1"], p["expert_b1"], p["expert_w2"], p["expert_b2"],
                  layer, mb)
    return _combine(group_tables, x1, lr, ge, p["ln2_g"], p["ln2_b"], ys, alpha, tm)


def kernel(x, meta_tokens, ln_in_g, ln_in_b, w_in, diff_lambda, diff_subln_g, fox_forget_b, gdn_conv_w,
           gdn_a_log, gdn_dt_bias, gdn_norm_g, w_out, ln1_g, ln1_b, router_w, router_b, expert_w1,
           expert_b1, expert_w2, expert_b2, ln2_g, ln2_b):
    bsz, s, d = x.shape
    depth = w_in.shape[0]
    n_lead = (-(s + N_META)) % Q_BLOCK
    if n_lead + N_META != Q_BLOCK:
        raise ValueError("sequence length must be a multiple of the query block")
    seq_len = n_lead + N_META + s
    alpha = (2 * depth) ** 0.25
    stream = _ln_in(x, meta_tokens.astype(x.dtype), ln_in_g, ln_in_b, n_lead).reshape(bsz * seq_len, d)
    per_layer = dict(w_in=w_in, diff_lambda=diff_lambda, diff_subln_g=diff_subln_g, fox_forget_b=fox_forget_b,
                     gdn_conv_w=gdn_conv_w, gdn_a_log=gdn_a_log, gdn_dt_bias=gdn_dt_bias,
                     gdn_norm_g=gdn_norm_g, w_out=w_out, ln1_g=ln1_g, ln1_b=ln1_b, router_w=router_w,
                     router_b=router_b, expert_w1=expert_w1, expert_b1=expert_b1, expert_w2=expert_w2,
                     expert_b2=expert_b2, ln2_g=ln2_g, ln2_b=ln2_b)
    stacked = ("expert_w1", "expert_b1", "expert_w2", "expert_b2")
    for layer in range(depth):
        p = {k: (v if k in stacked else v[layer]) for k, v in per_layer.items()}
        stream = _layer(stream, layer, seq_len, n_lead, alpha, p)
    return stream.reshape(bsz, seq_len, d)[:, n_lead + N_META:]
```

```python
import functools
import math

import numpy as np
import jax
import jax.numpy as jnp
from jax import lax
from jax.experimental import pallas as pl
from jax.experimental.pallas import tpu as pltpu

F32 = jnp.float32
BF16 = jnp.bfloat16
I32 = jnp.int32

CHUNK = 64
CHUNK_SHIFT = 6
Q_BLOCK = 128
N_META = 16
DIFF_HEADS = 4
DIFF_QK_DIM = 64
DIFF_V_DIM = 128
FOX_HEADS = 4
FOX_HEAD_DIM = 64
GDN_HEADS = 4
GDN_HEAD_DIM = 64
CONV_K = 4
N_EXPERTS = 32
TOP_K = 4
SWIGLU_LIMIT = 7.0
SWIGLU_ALPHA = 1.702
MASK_VALUE = -1e30
LOG2_E = 1.4426950408889634
FOX_BIAS_PIECES = 3

DIFF_QK_WIDTH = DIFF_HEADS * 2 * DIFF_QK_DIM
DIFF_V_WIDTH = DIFF_HEADS * DIFF_V_DIM
FOX_WIDTH = FOX_HEADS * FOX_HEAD_DIM
GDN_WIDTH = GDN_HEADS * GDN_HEAD_DIM
GDN_CONV_CH = 3 * GDN_WIDTH
IN_PROJ_SIZES = (DIFF_QK_WIDTH, DIFF_QK_WIDTH, DIFF_V_WIDTH, FOX_WIDTH, FOX_WIDTH, FOX_WIDTH, FOX_HEADS,
                 GDN_CONV_CH, GDN_HEADS, GDN_HEADS, GDN_WIDTH)

LANES = 128
SUBLANES = 8
VMEM_LIMIT = 56 * 1024 * 1024

GATE_FF = 0
GATE_BETA = 4
GATE_DECAY = 8

ATTN_WIDTH = 2 * DIFF_QK_WIDTH + DIFF_V_WIDTH + 3 * FOX_WIDTH
GDN_SLAB = GDN_CONV_CH + GDN_WIDTH


def _cparams(*sem):
    return pltpu.CompilerParams(dimension_semantics=tuple(sem), vmem_limit_bytes=VMEM_LIMIT)


def _dot(a, b):
    return jnp.dot(a.astype(BF16), b.astype(BF16), preferred_element_type=F32)


def _dot_nt(a, b):
    return lax.dot_general(a.astype(BF16), b.astype(BF16), (((1,), (1,)), ((), ())),
                           preferred_element_type=F32)


def _split2(x):
    hi = x.astype(BF16)
    lo = (x - hi.astype(F32)).astype(BF16)
    return hi, lo


def _split3(x):
    hi = x.astype(BF16)
    r = x - hi.astype(F32)
    mid = r.astype(BF16)
    lo = (r - mid.astype(F32)).astype(BF16)
    return hi, mid, lo


def _dot3(a, b):
    ah, al = _split2(a)
    bh, bl = _split2(b)
    return (jnp.dot(ah, bh, preferred_element_type=F32) + jnp.dot(ah, bl, preferred_element_type=F32)
            + jnp.dot(al, bh, preferred_element_type=F32))


def _dot_exact_lhs(a01, b):
    a = a01.astype(BF16)
    hi, mid, lo = _split3(b)
    return (jnp.dot(a, hi, preferred_element_type=F32) + jnp.dot(a, mid, preferred_element_type=F32)
            + jnp.dot(a, lo, preferred_element_type=F32))


_BDIMS = (((2,), (1,)), ((0,), (0,)))
_BDIMS_NT = (((2,), (2,)), ((0,), (0,)))


def _bdot(a, b):
    return lax.dot_general(a.astype(BF16), b.astype(BF16), _BDIMS, preferred_element_type=F32)


def _bdot_nt(a, b):
    return lax.dot_general(a.astype(BF16), b.astype(BF16), _BDIMS_NT, preferred_element_type=F32)


def _bdot3(a, b):
    ah, al = _split2(a)
    bh, bl = _split2(b)
    return (lax.dot_general(ah, bh, _BDIMS, preferred_element_type=F32)
            + lax.dot_general(ah, bl, _BDIMS, preferred_element_type=F32)
            + lax.dot_general(al, bh, _BDIMS, preferred_element_type=F32))


def _bdot_exact_lhs(a01, b):
    a = a01.astype(BF16)
    hi, mid, lo = _split3(b)
    return (lax.dot_general(a, hi, _BDIMS, preferred_element_type=F32)
            + lax.dot_general(a, mid, _BDIMS, preferred_element_type=F32)
            + lax.dot_general(a, lo, _BDIMS, preferred_element_type=F32))


def _layer_norm(x, g, b, eps=1e-5):
    mu = jnp.mean(x, axis=-1, keepdims=True)
    xc = x - mu
    var = jnp.mean(xc * xc, axis=-1, keepdims=True)
    return xc * lax.rsqrt(var + eps) * g + b


def _sigmoid(x):
    return 1.0 / (1.0 + jnp.exp(-x))


def _softplus(x):
    return jnp.maximum(x, 0.0) + jnp.log1p(jnp.exp(-jnp.abs(x)))


def _log_sigmoid(x):
    return -_softplus(-x)


def _ln_in_kernel(x_ref, meta_ref, g_ref, b_ref, o_ref, *, n_lead):
    i = pl.program_id(1)
    g = g_ref[...]
    b = b_ref[...]

    @pl.when(i == 0)
    def _():
        o_ref[0, :n_lead, :] = jnp.broadcast_to(b, (n_lead, b.shape[-1]))
        o_ref[0, n_lead:, :] = _layer_norm(meta_ref[...], g, b)

    @pl.when(i > 0)
    def _():
        o_ref[0] = _layer_norm(x_ref[0], g, b)


def _ln_in(x, meta, g, b, n_lead):
    bsz, s, d = x.shape
    l = n_lead + N_META + s
    blk = n_lead + N_META
    return pl.pallas_call(
        functools.partial(_ln_in_kernel, n_lead=n_lead),
        out_shape=jax.ShapeDtypeStruct((bsz, l, d), F32),
        grid=(bsz, l // blk),
        in_specs=[pl.BlockSpec((1, blk, d), lambda bb, i: (bb, jnp.maximum(i - 1, 0), 0)),
                  pl.BlockSpec((N_META, d), lambda bb, i: (0, 0)),
                  pl.BlockSpec((1, d), lambda bb, i: (0, 0)),
                  pl.BlockSpec((1, d), lambda bb, i: (0, 0))],
        out_specs=pl.BlockSpec((1, blk, d), lambda bb, i: (bb, i, 0)),
        compiler_params=_cparams("arbitrary", "arbitrary"),
        name="ln_in",
    )(x, meta, g.reshape(1, d), b.reshape(1, d))


PROJ_COL_CHUNK = 768


def _proj_kernel(x_ref, wa_ref, wg_ref, ws_ref, fb_ref, oa_ref, og_ref, os_ref, kb_ref, carry_ref, *, tm, n_lead):
    i = pl.program_id(1)
    pos = i * tm + lax.broadcasted_iota(I32, (tm, 1), 0)
    x = jnp.where(pos >= n_lead, x_ref[...], 0.0)
    xb = x.astype(BF16)
    for c in range(0, ATTN_WIDTH, PROJ_COL_CHUNK):
        oa_ref[:, c:c + PROJ_COL_CHUNK] = jnp.dot(
            xb, wa_ref[:, c:c + PROJ_COL_CHUNK], preferred_element_type=F32).astype(BF16)
    og_ref[...] = jnp.dot(xb, wg_ref[...], preferred_element_type=F32)
    gates = _dot3(x, ws_ref[...])
    os_ref[...] = gates

    @pl.when(i == 0)
    def _():
        carry_ref[...] = jnp.zeros_like(carry_ref)

    log_f = _log_sigmoid(gates + fb_ref[...])
    r = lax.broadcasted_iota(I32, (LANES, LANES), 0)
    c = lax.broadcasted_iota(I32, (LANES, LANES), 1)
    tri = jnp.where(c <= r, 1.0, 0.0).astype(BF16)
    carry = carry_ref[...]
    cums = []
    for r0 in range(0, tm, LANES):
        cums.append(_dot_exact_lhs(tri, log_f[r0:r0 + LANES, :]) + carry)
        carry = cums[-1][LANES - 1:LANES, :]
    carry_ref[...] = carry
    cum = jnp.concatenate(cums, axis=0)
    lane = lax.broadcasted_iota(I32, (1, LANES), 1)
    out = jnp.zeros((tm, LANES), F32)
    for hh in range(FOX_HEADS):
        bias = jnp.where(pos >= n_lead, -LOG2_E * cum[:, GATE_FF + hh:GATE_FF + hh + 1], MASK_VALUE)
        for n, piece in enumerate(_split3(bias)):
            out = jnp.where(lane == _fox_bias_lane(hh) + n, piece.astype(F32), out)
    kb_ref[...] = out.astype(BF16)


def _in_proj(stream, wa, wg, ws, forget_bias_lanes, seq_len, n_lead, tm):
    t_all, d = stream.shape
    nb = seq_len // tm
    row = lambda bb, i: (bb * nb + i, 0)
    const = lambda bb, i: (0, 0)
    return pl.pallas_call(
        functools.partial(_proj_kernel, tm=tm, n_lead=n_lead),
        out_shape=(jax.ShapeDtypeStruct((t_all, ATTN_WIDTH), BF16),
                   jax.ShapeDtypeStruct((t_all, GDN_SLAB), F32),
                   jax.ShapeDtypeStruct((t_all, LANES), F32),
                   jax.ShapeDtypeStruct((t_all, LANES), BF16)),
        grid=(t_all // seq_len, nb),
        in_specs=[pl.BlockSpec((tm, d), row),
                  pl.BlockSpec((d, ATTN_WIDTH), const),
                  pl.BlockSpec((d, GDN_SLAB), const),
                  pl.BlockSpec((d, LANES), const),
                  pl.BlockSpec((1, LANES), const)],
        out_specs=(pl.BlockSpec((tm, ATTN_WIDTH), row),
                   pl.BlockSpec((tm, GDN_SLAB), row),
                   pl.BlockSpec((tm, LANES), row),
                   pl.BlockSpec((tm, LANES), row)),
        scratch_shapes=[pltpu.VMEM((1, LANES), F32)],
        compiler_params=_cparams("arbitrary", "arbitrary"),
        name="in_proj",
    )(stream, wa, wg, ws, forget_bias_lanes)


def _online_softmax_step(s, vt, m_ref, l_ref, a_ref):
    m_old = m_ref[...]
    m_new = jnp.maximum(m_old, jnp.max(s, axis=0, keepdims=True))
    alpha = jnp.exp2(m_old - m_new)
    p = jnp.exp2(s - m_new)
    l_ref[...] = alpha * l_ref[...] + jnp.sum(p, axis=0, keepdims=True)
    a_ref[...] = alpha * a_ref[...] + jnp.dot(vt, p.astype(BF16), preferred_element_type=F32)
    m_ref[...] = m_new


def _pipelined_blocks(i, buf_a, buf_b, produce_first, produce, consume):
    produce_first(0, buf_a)

    def pair(t, carry):
        j = 2 * t
        produce(j + 1, buf_b)
        consume(j, buf_a)
        produce(j + 2, buf_a)
        consume(j + 1, buf_b)
        return carry

    lax.fori_loop(0, i // 2, pair, 0)

    @pl.when(i % 2 == 0)
    def _():
        consume(i, buf_a, diagonal=True)

    @pl.when(i % 2 == 1)
    def _():
        produce(i, buf_b)
        consume(i - 1, buf_a)
        consume(i, buf_b, diagonal=True)


def _transpose_to(dst_ref, src_ref, n_rows):
    for c in range(src_ref.shape[1] // LANES):
        def body(t, carry):
            r0 = pl.multiple_of(t * LANES, LANES)
            tile = src_ref[pl.ds(r0, LANES), c * LANES:(c + 1) * LANES].astype(F32)
            dst_ref[c * LANES:(c + 1) * LANES, pl.ds(r0, LANES)] = tile.T.astype(dst_ref.dtype)
            return carry
        lax.fori_loop(0, n_rows // LANES, body, 0)


def _init_softmax_state(m_ref, l_ref, a_ref):
    m_ref[...] = jnp.full(m_ref.shape, MASK_VALUE, F32)
    l_ref[...] = jnp.zeros(l_ref.shape, F32)
    a_ref[...] = jnp.zeros(a_ref.shape, F32)


def _diff_attn_kernel(lam_ref, q_ref, k_ref, v_ref, g_ref, o_ref, vt_ref, sa_ref, sb_ref,
                      m0, l0, a0, m1, l1, a1, *, tb, n_lead, seq_len, out_scale):
    i = pl.program_id(2)

    @pl.when(i == 0)
    def _():
        _transpose_to(vt_ref, v_ref, seq_len)

    qt = q_ref[...].astype(F32).T
    row = lax.broadcasted_iota(I32, (LANES, 1), 0)
    qt0 = jnp.where(row < DIFF_QK_DIM, qt, 0.0).astype(BF16)
    qt1 = jnp.where(row >= DIFF_QK_DIM, qt, 0.0).astype(BF16)
    _init_softmax_state(m0, l0, a0)
    _init_softmax_state(m1, l1, a1)

    def produce(j, buf, pad_bias=False):
        start = pl.multiple_of(j * tb, tb)
        k = k_ref[pl.ds(start, tb), :]
        for half, qth in enumerate((qt0, qt1)):
            s = jnp.dot(k, qth, preferred_element_type=F32)
            if pad_bias:
                kpos = lax.broadcasted_iota(I32, (tb, 1), 0)
                s = s + jnp.where(kpos >= n_lead, 0.0, MASK_VALUE)
            buf[half] = s

    def consume(j, buf, diagonal=False):
        start = pl.multiple_of(j * tb, tb)
        vt = vt_ref[:, pl.ds(start, tb)]
        for half, state in enumerate(((m0, l0, a0), (m1, l1, a1))):
            s = buf[half]
            if diagonal:
                kpos = j * tb + lax.broadcasted_iota(I32, (tb, 1), 0)
                qpos = i * tb + lax.broadcasted_iota(I32, (1, tb), 1)
                s = jnp.where(kpos >> CHUNK_SHIFT <= qpos >> CHUNK_SHIFT, s, MASK_VALUE)
            _online_softmax_step(s, vt, *state)

    _pipelined_blocks(i, sa_ref, sb_ref, functools.partial(produce, pad_bias=True), produce, consume)

    lam = lam_ref[0]
    o = a0[...] / l0[...] - lam * (a1[...] / l1[...])
    o = o * lax.rsqrt(jnp.mean(o * o, axis=0, keepdims=True) + 1e-5) * g_ref[...] * out_scale
    o_ref[...] = o.T.astype(o_ref.dtype)


def _diff_attention(attn, lam, subln_g, seq_len, n_lead, lambda_init, tb):
    t_all = attn.shape[0]
    bsz = t_all // seq_len
    nq = seq_len // tb
    kcol = DIFF_QK_WIDTH // LANES
    vcol = 2 * DIFF_QK_WIDTH // LANES
    scratch = [pltpu.VMEM((DIFF_V_DIM, seq_len), BF16),
               pltpu.VMEM((2, tb, tb), F32), pltpu.VMEM((2, tb, tb), F32)]
    for _ in range(2):
        scratch += [pltpu.VMEM((1, tb), F32), pltpu.VMEM((1, tb), F32), pltpu.VMEM((DIFF_V_DIM, tb), F32)]
    return pl.pallas_call(
        functools.partial(_diff_attn_kernel, tb=tb, n_lead=n_lead, seq_len=seq_len,
                          out_scale=1.0 - lambda_init),
        out_shape=jax.ShapeDtypeStruct((t_all, DIFF_V_WIDTH), BF16),
        grid_spec=pltpu.PrefetchScalarGridSpec(
            num_scalar_prefetch=1,
            grid=(bsz, DIFF_HEADS, nq),
            in_specs=[pl.BlockSpec((tb, LANES), lambda b, h, i, lam: (b * nq + i, h)),
                      pl.BlockSpec((seq_len, LANES), lambda b, h, i, lam: (b, kcol + h)),
                      pl.BlockSpec((seq_len, LANES), lambda b, h, i, lam: (b, vcol + h)),
                      pl.BlockSpec((DIFF_V_DIM, 1), lambda b, h, i, lam: (0, 0))],
            out_specs=pl.BlockSpec((tb, DIFF_V_DIM), lambda b, h, i, lam: (b * nq + i, h)),
            scratch_shapes=scratch),
        compiler_params=_cparams("arbitrary", "arbitrary", "arbitrary"),
        name="diff_attention",
    )(lam, attn, attn, attn, subln_g.reshape(DIFF_V_DIM, 1))


def _fox_bias_lane(hh):
    return (1 - hh % 2) * FOX_HEAD_DIM + FOX_BIAS_PIECES * (hh // 2)


def _fox_attn_kernel(q_ref, k_ref, v_ref, kb_ref, o_ref, vt_ref, sa_ref, sb_ref, *scratch, tb, seq_len):
    i = pl.program_id(1)
    pairs = FOX_HEADS // 2

    @pl.when(i == 0)
    def _():
        _transpose_to(vt_ref, v_ref, seq_len)

    row = lax.broadcasted_iota(I32, (LANES, 1), 0)
    lane = lax.broadcasted_iota(I32, (1, LANES), 1)
    qts = []
    for p in range(pairs):
        qt = q_ref[:, p * LANES:(p + 1) * LANES].astype(F32).T
        for r in range(2):
            own = jnp.logical_and(row >= r * FOX_HEAD_DIM, row < (r + 1) * FOX_HEAD_DIM)
            b0 = _fox_bias_lane(2 * p + r)
            ones = jnp.logical_and(row >= b0, row < b0 + FOX_BIAS_PIECES)
            qts.append(jnp.where(own, qt, jnp.where(ones, 1.0, 0.0)).astype(BF16))
    for hh in range(FOX_HEADS):
        _init_softmax_state(*scratch[3 * hh:3 * hh + 3])

    def produce(j, buf):
        start = pl.multiple_of(j * tb, tb)
        kb = kb_ref[pl.ds(start, tb), :]
        for p in range(pairs):
            k = k_ref[pl.ds(start, tb), p * LANES:(p + 1) * LANES]
            for r in range(2):
                own = jnp.logical_and(lane >= r * FOX_HEAD_DIM, lane < (r + 1) * FOX_HEAD_DIM)
                buf[2 * p + r] = jnp.dot(jnp.where(own, k, kb), qts[2 * p + r],
                                         preferred_element_type=F32)

    def consume(j, buf, diagonal=False):
        start = pl.multiple_of(j * tb, tb)
        for hh in range(FOX_HEADS):
            vt = vt_ref[hh * FOX_HEAD_DIM:(hh + 1) * FOX_HEAD_DIM, pl.ds(start, tb)]
            s = buf[hh]
            if diagonal:
                kpos = j * tb + lax.broadcasted_iota(I32, (tb, 1), 0)
                qpos = i * tb + lax.broadcasted_iota(I32, (1, tb), 1)
                s = jnp.where(kpos <= qpos, s, MASK_VALUE)
            _online_softmax_step(s, vt, *scratch[3 * hh:3 * hh + 3])

    _pipelined_blocks(i, sa_ref, sb_ref, produce, produce, consume)

    for p in range(pairs):
        _, l_lo, a_lo = scratch[6 * p:6 * p + 3]
        _, l_hi, a_hi = scratch[6 * p + 3:6 * p + 6]
        o = jnp.concatenate([a_lo[...] / l_lo[...], a_hi[...] / l_hi[...]], axis=0)
        o_ref[:, p * LANES:(p + 1) * LANES] = o.T.astype(o_ref.dtype)


def _fox_attention(attn, key_bias, seq_len, tb):
    t_all = attn.shape[0]
    bsz = t_all // seq_len
    nq = seq_len // tb
    base = (2 * DIFF_QK_WIDTH + DIFF_V_WIDTH) // FOX_WIDTH
    scratch = [pltpu.VMEM((FOX_WIDTH, seq_len), BF16),
               pltpu.VMEM((FOX_HEADS, tb, tb), F32), pltpu.VMEM((FOX_HEADS, tb, tb), F32)]
    for _ in range(FOX_HEADS):
        scratch += [pltpu.VMEM((1, tb), F32), pltpu.VMEM((1, tb), F32), pltpu.VMEM((FOX_HEAD_DIM, tb), F32)]
    return pl.pallas_call(
        functools.partial(_fox_attn_kernel, tb=tb, seq_len=seq_len),
        out_shape=jax.ShapeDtypeStruct((t_all, FOX_WIDTH), BF16),
        grid=(bsz, nq),
        in_specs=[pl.BlockSpec((tb, FOX_WIDTH), lambda b, i: (b * nq + i, base)),
                  pl.BlockSpec((seq_len, FOX_WIDTH), lambda b, i: (b, base + 1)),
                  pl.BlockSpec((seq_len, FOX_WIDTH), lambda b, i: (b, base + 2)),
                  pl.BlockSpec((seq_len, LANES), lambda b, i: (b, 0))],
        out_specs=pl.BlockSpec((tb, FOX_WIDTH), lambda b, i: (b * nq + i, 0)),
        scratch_shapes=scratch,
        compiler_params=_cparams("arbitrary", "arbitrary"),
        name="fox_attention",
    )(attn, attn, attn, key_bias)


def _gdn_kernel(x_ref, gate_ref, cw_ref, alog_ref, dtb_ref, ng_ref, o_ref,
                halo_ref, xs_ref, state_ref, *, rows):
    cg = pl.program_id(1)
    g_chunks = rows // CHUNK
    hd = GDN_HEAD_DIM

    @pl.when(cg == 0)
    def _():
        halo_ref[...] = jnp.zeros_like(halo_ref)
        state_ref[...] = jnp.zeros_like(state_ref)

    xin = x_ref[:, :GDN_CONV_CH]
    xs_ref[0:SUBLANES, :] = halo_ref[...]
    xs_ref[SUBLANES:SUBLANES + rows, :] = xin
    halo_ref[...] = xin[rows - SUBLANES:rows, :]
    y = jnp.zeros((rows, GDN_CONV_CH), F32)
    for tap in range(CONV_K):
        back = CONV_K - 1 - tap
        y = y + cw_ref[tap:tap + 1, :] * xs_ref[SUBLANES - back:SUBLANES - back + rows, :]
    y = y * _sigmoid(y)

    gates = gate_ref[...]
    beta_all = _sigmoid(gates)
    g_all = -jnp.exp(alog_ref[...]) * _softplus(gates + dtb_ref[...])
    rowi = lax.broadcasted_iota(I32, (rows, 1), 0) & (CHUNK - 1)
    gc_all = g_all
    shift = 1
    while shift < CHUNK:
        gc_all = gc_all + jnp.where(rowi >= shift, pltpu.roll(gc_all, shift, 0), 0.0)
        shift *= 2

    ri = lax.broadcasted_iota(I32, (CHUNK, CHUNK), 0)
    ci = lax.broadcasted_iota(I32, (CHUNK, CHUNK), 1)
    eye = jnp.where(ri == ci, 1.0, 0.0).astype(F32)
    causal = (ci <= ri)[None]
    strict = (ci < ri)[None]
    eye_b = jnp.broadcast_to(eye[None], (g_chunks, CHUNK, CHUNK))

    prep = []
    for h in range(GDN_HEADS):
        def head3(base):
            return y[:, base + h * hd:base + (h + 1) * hd]
        q = head3(0)
        k = head3(GDN_WIDTH)
        v = head3(2 * GDN_WIDTH)
        q = q * lax.rsqrt(jnp.sum(q * q, axis=-1, keepdims=True) + 1e-6) * (hd ** -0.5)
        k = k * lax.rsqrt(jnp.sum(k * k, axis=-1, keepdims=True) + 1e-6)
        beta = beta_all[:, GATE_BETA + h:GATE_BETA + h + 1]
        gcol = jnp.broadcast_to(gc_all[:, GATE_DECAY + h:GATE_DECAY + h + 1], (rows, hd))
        egc = jnp.exp(gcol)

        q3 = q.reshape(g_chunks, CHUNK, hd)
        k3 = k.reshape(g_chunks, CHUNK, hd)
        gcol3 = gcol.reshape(g_chunks, CHUNK, CHUNK)
        beta3 = jnp.broadcast_to(beta, (rows, hd)).reshape(g_chunks, CHUNK, hd)
        grow3 = jnp.swapaxes(gcol3, 1, 2)
        decay = jnp.exp(jnp.where(causal, gcol3 - grow3, MASK_VALUE))
        a = jnp.where(strict, _bdot_nt(k3, k3) * beta3 * decay, 0.0)

        x = -a
        tinv = eye_b + x
        span = 2
        while span < CHUNK:
            x = _bdot(x, x)
            tinv = tinv + _bdot(tinv, x)
            span *= 2

        vb = (v * beta).reshape(g_chunks, CHUNK, hd)
        kb = (k * beta * egc).reshape(g_chunks, CHUNK, hd)
        uw3 = _bdot(tinv, jnp.concatenate([vb, kb], axis=-1))
        qk3 = _bdot_nt(q3, k3) * decay
        qd3 = (q * egc).reshape(g_chunks, CHUNK, hd)
        glast3 = gcol3[:, CHUNK - 1:CHUNK, :]
        kdt3 = jnp.swapaxes(k3 * jnp.exp(glast3 - gcol3), 1, 2)
        cd3 = jnp.exp(glast3)
        k_uw = _bdot(kdt3, uw3)
        q_uw = _bdot(qk3, uw3)
        prep.append((k_uw[..., hd:], k_uw[..., :hd], qd3 - q_uw[..., hd:], q_uw[..., :hd], cd3))

    states = [state_ref[h] for h in range(GDN_HEADS)]
    outs = [[] for _ in range(GDN_HEADS)]
    for c in range(g_chunks):
        for h in range(GDN_HEADS):
            kw3, ku3, qp3, o03, cd3 = prep[h]
            state = states[h]
            outs[h].append(_dot(qp3[c], state) + o03[c])
            states[h] = state * cd3[c] + (ku3[c] - _dot(kw3[c], state))

    for h in range(GDN_HEADS):
        state_ref[h] = states[h]
        o = jnp.concatenate(outs[h], axis=0)
        o = o * lax.rsqrt(jnp.mean(o * o, axis=-1, keepdims=True) + 1e-6) * ng_ref[...]
        z = x_ref[:, GDN_CONV_CH + h * hd:GDN_CONV_CH + (h + 1) * hd]
        o_ref[:, h * hd:(h + 1) * hd] = (o * (z * _sigmoid(z))).astype(o_ref.dtype)


def _gated_deltanet(gdn_slab, gates, conv_w, a_log_lanes, dt_bias_lanes, norm_g, seq_len, rows):
    t_all = gdn_slab.shape[0]
    bsz = t_all // seq_len
    nb = seq_len // rows
    row = lambda b, i: (b * nb + i, 0)
    const = lambda b, i: (0, 0)
    return pl.pallas_call(
        functools.partial(_gdn_kernel, rows=rows),
        out_shape=jax.ShapeDtypeStruct((t_all, GDN_WIDTH), BF16),
        grid=(bsz, nb),
        in_specs=[pl.BlockSpec((rows, GDN_SLAB), row),
                  pl.BlockSpec((rows, LANES), row),
                  pl.BlockSpec((SUBLANES, GDN_CONV_CH), const),
                  pl.BlockSpec((1, LANES), const),
                  pl.BlockSpec((1, LANES), const),
                  pl.BlockSpec((1, GDN_HEAD_DIM), const)],
        out_specs=pl.BlockSpec((rows, GDN_WIDTH), row),
        scratch_shapes=[pltpu.VMEM((SUBLANES, GDN_CONV_CH), F32),
                        pltpu.VMEM((rows + SUBLANES, GDN_CONV_CH), F32),
                        pltpu.VMEM((GDN_HEADS, GDN_HEAD_DIM, GDN_HEAD_DIM), F32)],
        compiler_params=_cparams("arbitrary", "arbitrary"),
        name="gated_deltanet",
    )(gdn_slab, gates, conv_w, a_log_lanes, dt_bias_lanes, norm_g.reshape(1, GDN_HEAD_DIM))


def _mix_router_kernel(x_ref, yd_ref, yf_ref, yg_ref, wd_ref, wf_ref, wg_ref, g_ref, b_ref,
                       rw_ref, rb_ref, x1_ref, lr_ref, lrt_ref, ge_ref, cnt_ref, *,
                       tm, rb_rows, alpha, seq_len, n_lead, n_seq):
    mix = (jnp.dot(yd_ref[...], wd_ref[...], preferred_element_type=F32)
           + jnp.dot(yf_ref[...], wf_ref[...], preferred_element_type=F32)
           + jnp.dot(yg_ref[...], wg_ref[...], preferred_element_type=F32))
    x1 = _layer_norm(alpha * x_ref[...] + mix, g_ref[...], b_ref[...])
    x1_ref[...] = x1

    lane = lax.broadcasted_iota(I32, (tm, LANES), 1)
    lane_f = lane.astype(F32)
    logits = _dot3(x1, rw_ref[...]) + rb_ref[...]
    vals = jnp.where(lane < N_EXPERTS, logits, -jnp.inf)
    hits, tops = [], []
    for _ in range(TOP_K):
        top = jnp.max(vals, axis=-1, keepdims=True)
        idx = jnp.min(jnp.where(vals == top, lane_f, float(LANES)), axis=-1, keepdims=True)
        hit = lane_f == idx
        vals = jnp.where(hit, -jnp.inf, vals)
        hits.append(hit)
        tops.append(top)
    exps = [jnp.exp(t - tops[0]) for t in tops]
    denom = exps[0] + exps[1] + exps[2] + exps[3]
    row = pl.program_id(0) * tm + lax.broadcasted_iota(I32, (tm, 1), 0)
    is_pad = row < 0
    for sq in range(n_seq):
        is_pad = jnp.logical_or(is_pad, jnp.logical_and(row >= sq * seq_len, row < sq * seq_len + n_lead))
    onehot = jnp.zeros((tm, LANES), F32)
    gate_by_expert = jnp.zeros((tm, LANES), F32)
    for kk in range(TOP_K):
        hit = jnp.logical_and(hits[kk], jnp.logical_not(is_pad))
        onehot = jnp.where(hit, 1.0, onehot)
        gate_by_expert = jnp.where(hit, exps[kk] / denom, gate_by_expert)
    ge_ref[...] = gate_by_expert

    r = lax.broadcasted_iota(I32, (rb_rows, rb_rows), 0)
    c = lax.broadcasted_iota(I32, (rb_rows, rb_rows), 1)
    tri = jnp.where(c < r, 1.0, 0.0).astype(BF16)
    for blk in range(tm // rb_rows):
        oh = onehot[blk * rb_rows:(blk + 1) * rb_rows, :]
        rank = jnp.dot(tri, oh.astype(BF16), preferred_element_type=F32)
        local = jnp.where(oh > 0.0, rank, -1.0)
        lr_ref[blk * rb_rows:(blk + 1) * rb_rows, :] = local
        lrt_ref[blk * N_EXPERTS:(blk + 1) * N_EXPERTS, :] = local.T[:N_EXPERTS, :]
        cnt_ref[blk] = jnp.sum(oh, axis=0, keepdims=True)


def _mix_router(stream, yd, yf, yg, w_out_b, ln_g, ln_b, rw, rb, alpha, rb_rows, seq_len, n_lead):
    t_all, d = stream.shape
    n_rb = t_all // rb_rows
    per_step = max(k for k in (5, 4, 3, 2, 1) if n_rb % k == 0)
    tm = per_step * rb_rows
    row = lambda i: (i, 0)
    const = lambda i: (0, 0)
    wd = w_out_b[:DIFF_V_WIDTH]
    wf = w_out_b[DIFF_V_WIDTH:DIFF_V_WIDTH + FOX_WIDTH]
    wg = w_out_b[DIFF_V_WIDTH + FOX_WIDTH:]
    return pl.pallas_call(
        functools.partial(_mix_router_kernel, tm=tm, rb_rows=rb_rows, alpha=alpha, seq_len=seq_len,
                          n_lead=n_lead, n_seq=t_all // seq_len),
        out_shape=(jax.ShapeDtypeStruct((t_all, d), F32),
                   jax.ShapeDtypeStruct((t_all, LANES), F32),
                   jax.ShapeDtypeStruct((n_rb * N_EXPERTS, rb_rows), F32),
                   jax.ShapeDtypeStruct((t_all, LANES), F32),
                   jax.ShapeDtypeStruct((n_rb, 1, LANES), F32)),
        grid=(t_all // tm,),
        in_specs=[pl.BlockSpec((tm, d), row),
                  pl.BlockSpec((tm, DIFF_V_WIDTH), row),
                  pl.BlockSpec((tm, FOX_WIDTH), row),
                  pl.BlockSpec((tm, GDN_WIDTH), row),
                  pl.BlockSpec((DIFF_V_WIDTH, d), const),
                  pl.BlockSpec((FOX_WIDTH, d), const),
                  pl.BlockSpec((GDN_WIDTH, d), const),
                  pl.BlockSpec((1, d), const),
                  pl.BlockSpec((1, d), const),
                  pl.BlockSpec((d, LANES), const),
                  pl.BlockSpec((1, LANES), const)],
        out_specs=(pl.BlockSpec((tm, d), row),
                   pl.BlockSpec((tm, LANES), row),
                   pl.BlockSpec((per_step * N_EXPERTS, rb_rows), row),
                   pl.BlockSpec((tm, LANES), row),
                   pl.BlockSpec((per_step, 1, LANES), lambda i: (i, 0, 0))),
        compiler_params=_cparams("arbitrary"),
        name="mix_router",
    )(stream, yd, yf, yg, wd, wf, wg, ln_g.reshape(1, d), ln_b.reshape(1, d), rw, rb)


SLAB = 64
GROUP_ALIGN = 8
ZERO_FILL_SIZES = (512, 256, 128, 64, 32, 16, 8)


def _slab_dma(gstart_ref, gcnt_ref, bb, c, e, action, make_copy):
    idx = bb * N_EXPERTS + e
    start = pl.multiple_of(gstart_ref[idx] + c * SLAB, GROUP_ALIGN)
    small = gcnt_ref[idx] - c * SLAB <= SLAB // 2
    for rows, cond in ((SLAB // 2, small), (SLAB, jnp.logical_not(small))):
        @pl.when(cond)
        def _():
            getattr(make_copy(start, rows), action)()


def _slab_rows(src_ref, rank_row0):
    want = (rank_row0 + lax.broadcasted_iota(I32, (SLAB, 1), 0)).astype(F32)
    return jnp.concatenate([jnp.where(src_ref[e:e + 1, :] == want, 1.0, 0.0).astype(BF16)
                            for e in range(N_EXPERTS)], axis=0)


def _dispatch_kernel(gstart_ref, gcnt_ref, npass_ref, zstart_ref, zrows_ref, x_ref, lrt_ref, xs_ref,
                     zbuf, zero_ref, sem, zsem, *, nb, n_slots):
    b = pl.program_id(0)
    slot = b % 2

    def slab_copy(bb, c, e, s, action):
        _slab_dma(gstart_ref, gcnt_ref, bb, c, e, action, lambda start, rows: pltpu.make_async_copy(
            zbuf.at[s, pl.ds(e * SLAB, rows)], xs_ref.at[pl.ds(start, rows)], sem.at[s]))

    def zero_fill(act):
        for e in range(N_EXPERTS):
            rows = zrows_ref[e]
            for size in ZERO_FILL_SIZES:
                @pl.when((rows & size) != 0)
                def _():
                    start = pl.multiple_of(zstart_ref[e] + (rows & ~(2 * size - 1)), GROUP_ALIGN)
                    act(pltpu.make_async_copy(zero_ref.at[pl.ds(0, size)], xs_ref.at[pl.ds(start, size)], zsem))

    def zero_tail(act):
        first = zstart_ref[N_EXPERTS - 1] + zrows_ref[N_EXPERTS - 1]
        size = ZERO_FILL_SIZES[0]

        def body(t, carry):
            start = pl.multiple_of(first + t * size, GROUP_ALIGN)
            act(pltpu.make_async_copy(zero_ref, xs_ref.at[pl.ds(start, size)], zsem))
            return carry

        lax.fori_loop(0, (n_slots - first) // size, body, 0)

    @pl.when(b == 0)
    def _():
        zero_ref[...] = jnp.zeros_like(zero_ref)
        for fill in (zero_fill, zero_tail):
            fill(lambda cp: cp.start())
        for fill in (zero_fill, zero_tail):
            fill(lambda cp: cp.wait())

    xb = x_ref[...].astype(BF16)
    zbuf[slot] = jnp.dot(_slab_rows(lrt_ref, 0), xb, preferred_element_type=F32)

    @pl.when(b > 0)
    def _():
        @pl.when(npass_ref[b - 1] <= 1)
        def _():
            for e in range(N_EXPERTS):
                slab_copy(b - 1, 0, e, 1 - slot, "wait")

    for e in range(N_EXPERTS):
        slab_copy(b, 0, e, slot, "start")

    @pl.when(npass_ref[b] > 1)
    def _():
        for e in range(N_EXPERTS):
            slab_copy(b, 0, e, slot, "wait")

        def extra_pass(c, carry):
            zbuf[slot] = jnp.dot(_slab_rows(lrt_ref, c * SLAB), xb, preferred_element_type=F32)
            for action in ("start", "wait"):
                for e in range(N_EXPERTS):
                    @pl.when(gcnt_ref[b * N_EXPERTS + e] > c * SLAB)
                    def _():
                        slab_copy(b, c, e, slot, action)
            return carry

        lax.fori_loop(1, npass_ref[b], extra_pass, 0)

    @pl.when(b == nb - 1)
    def _():
        @pl.when(npass_ref[b] <= 1)
        def _():
            for e in range(N_EXPERTS):
                slab_copy(b, 0, e, slot, "wait")


def _dispatch(tables, x1, lrt, n_slots, tm):
    t_all, d = x1.shape
    nb = t_all // tm
    ntab = len(tables)
    return pl.pallas_call(
        functools.partial(_dispatch_kernel, nb=nb, n_slots=n_slots),
        out_shape=jax.ShapeDtypeStruct((n_slots, d), F32),
        grid_spec=pltpu.PrefetchScalarGridSpec(
            num_scalar_prefetch=ntab,
            grid=(nb,),
            in_specs=[pl.BlockSpec((tm, d), lambda i, *_: (i, 0)),
                      pl.BlockSpec((N_EXPERTS, tm), lambda i, *_: (i, 0))],
            out_specs=pl.BlockSpec(memory_space=pl.ANY),
            scratch_shapes=[pltpu.VMEM((2, N_EXPERTS * SLAB, d), F32),
                            pltpu.VMEM((ZERO_FILL_SIZES[0], d), F32),
                            pltpu.SemaphoreType.DMA((2,)),
                            pltpu.SemaphoreType.DMA]),
        compiler_params=_cparams("arbitrary"),
        name="moe_dispatch",
    )(*tables, x1, lrt)


def _expert_kernel(be_ref, na_ref, x_ref, w1_ref, b1_ref, w2_ref, b2_ref, o_ref, w1b_ref, w2b_ref, *, d_expert):
    i = pl.program_id(0)
    active = i < na_ref[0]
    new_expert = jnp.logical_or(i == 0, be_ref[i] != be_ref[jnp.maximum(i - 1, 0)])

    @pl.when(jnp.logical_and(active, new_expert))
    def _():
        w1b_ref[...] = w1_ref[0].astype(BF16)
        w2b_ref[...] = w2_ref[0].astype(BF16)

    @pl.when(active)
    def _():
        hid = jnp.dot(x_ref[...].astype(BF16), w1b_ref[...], preferred_element_type=F32) + b1_ref[0]
        glu = jnp.minimum(hid[:, :d_expert], SWIGLU_LIMIT)
        lin = jnp.clip(hid[:, d_expert:], -SWIGLU_LIMIT, SWIGLU_LIMIT)
        act = glu * _sigmoid(SWIGLU_ALPHA * glu) * (lin + 1.0)
        o_ref[...] = jnp.dot(act.astype(BF16), w2b_ref[...], preferred_element_type=F32) + b2_ref[0]

    @pl.when(jnp.logical_not(active))
    def _():
        o_ref[...] = jnp.zeros_like(o_ref)


def _experts(block_expert, n_active, xs, w1_all, b1, w2_all, b2, layer, mb):
    n_slots, d = xs.shape
    depth, n_exp, _, f2 = w1_all.shape
    f = f2 // 2
    nb = n_slots // mb
    w1 = w1_all.reshape(depth * n_exp, d, f2)
    w2 = w2_all.reshape(depth * n_exp, f, d)
    block_expert = block_expert + layer * n_exp
    return pl.pallas_call(
        functools.partial(_expert_kernel, d_expert=f),
        out_shape=jax.ShapeDtypeStruct((n_slots, d), F32),
        grid_spec=pltpu.PrefetchScalarGridSpec(
            num_scalar_prefetch=2,
            grid=(nb,),
            in_specs=[pl.BlockSpec((mb, d), lambda i, be, na: (jnp.minimum(i, na[0] - 1), 0)),
                      pl.BlockSpec((1, d, f2), lambda i, be, na: (be[i], 0, 0)),
                      pl.BlockSpec((1, 1, f2), lambda i, be, na: (be[i], 0, 0)),
                      pl.BlockSpec((1, f, d), lambda i, be, na: (be[i], 0, 0)),
                      pl.BlockSpec((1, 1, d), lambda i, be, na: (be[i], 0, 0))],
            out_specs=pl.BlockSpec((mb, d), lambda i, be, na: (i, 0)),
            scratch_shapes=[pltpu.VMEM((d, f2), BF16), pltpu.VMEM((f, d), BF16)]),
        compiler_params=_cparams("arbitrary"),
        name="moe_experts",
    )(block_expert, n_active, xs, w1, b1.reshape(depth * n_exp, 1, f2), w2, b2.reshape(depth * n_exp, 1, d))


def _combine_kernel(gstart_ref, gcnt_ref, npass_ref, x1_ref, lr_ref, ge_ref, g_ref, b_ref, ys_ref, o_ref,
                    ybuf, acc_ref, sem, *, tm, nb, alpha):
    b = pl.program_id(0)
    slot = b % 2

    def slab_copy(bb, c, e, s, action):
        _slab_dma(gstart_ref, gcnt_ref, bb, c, e, action, lambda start, rows: pltpu.make_async_copy(
            ys_ref.at[pl.ds(start, rows)], ybuf.at[s, pl.ds(e * SLAB, rows)], sem.at[s]))

    @pl.when(b == 0)
    def _():
        ybuf[...] = jnp.zeros_like(ybuf)
        for e in range(N_EXPERTS):
            slab_copy(0, 0, e, 0, "start")

    @pl.when(b + 1 < nb)
    def _():
        for e in range(N_EXPERTS):
            slab_copy(b + 1, 0, e, 1 - slot, "start")

    for e in range(N_EXPERTS):
        slab_copy(b, 0, e, slot, "wait")

    lane = lax.broadcasted_iota(I32, (1, LANES), 1)
    lr = lr_ref[...]
    ge = ge_ref[...]

    def weighted_sum(rank0):
        his, los = [], []
        for col0 in range(0, N_EXPERTS * SLAB, LANES):
            col = col0 + lane
            rank = jnp.full((tm, LANES), -2.0, F32)
            gate = jnp.zeros((tm, LANES), F32)
            want = jnp.zeros((1, LANES), I32)
            for e in range(col0 // SLAB, min((col0 + LANES - 1) // SLAB, N_EXPERTS - 1) + 1):
                inside = jnp.logical_and(col >= e * SLAB, col < (e + 1) * SLAB)
                rank = jnp.where(inside, lr[:, e:e + 1], rank)
                gate = jnp.where(inside, ge[:, e:e + 1], gate)
                want = jnp.where(inside, col - e * SLAB + rank0, want)
            w = jnp.where(rank == want.astype(F32), gate, 0.0)
            hi = w.astype(BF16)
            his.append(hi)
            los.append((w - hi.astype(F32)).astype(BF16))
        sel = jnp.concatenate([jnp.concatenate(his, axis=1), jnp.concatenate(los, axis=1)], axis=0)
        r = jnp.dot(sel, ybuf[slot].astype(BF16), preferred_element_type=F32)
        return r[:tm] + r[tm:]

    acc_ref[...] = alpha * x1_ref[...] + weighted_sum(0)

    @pl.when(npass_ref[b] > 1)
    def _():
        def extra_pass(c, carry):
            for action in ("start", "wait"):
                for e in range(N_EXPERTS):
                    @pl.when(gcnt_ref[b * N_EXPERTS + e] > c * SLAB)
                    def _():
                        slab_copy(b, c, e, slot, action)
            acc_ref[...] += weighted_sum(c * SLAB)
            return carry

        lax.fori_loop(1, npass_ref[b], extra_pass, 0)

    o_ref[...] = _layer_norm(acc_ref[...], g_ref[...], b_ref[...])


def _combine(tables, x1, lr, ge, ln_g, ln_b, ys, alpha, tm):
    t_all, d = x1.shape
    nb = t_all // tm
    ntab = len(tables)
    row = lambda i, *_: (i, 0)
    const = lambda i, *_: (0, 0)
    return pl.pallas_call(
        functools.partial(_combine_kernel, tm=tm, nb=nb, alpha=alpha),
        out_shape=jax.ShapeDtypeStruct((t_all, d), F32),
        grid_spec=pltpu.PrefetchScalarGridSpec(
            num_scalar_prefetch=ntab,
            grid=(nb,),
            in_specs=[pl.BlockSpec((tm, d), row),
                      pl.BlockSpec((tm, LANES), row),
                      pl.BlockSpec((tm, LANES), row),
                      pl.BlockSpec((1, d), const),
                      pl.BlockSpec((1, d), const),
                      pl.BlockSpec(memory_space=pl.ANY)],
            out_specs=pl.BlockSpec((tm, d), row),
            scratch_shapes=[pltpu.VMEM((2, N_EXPERTS * SLAB, d), F32),
                            pltpu.VMEM((tm, d), F32),
                            pltpu.SemaphoreType.DMA((2,))]),
        compiler_params=_cparams("arbitrary"),
        name="moe_combine",
    )(*tables, x1, lr, ge, ln_g.reshape(1, d), ln_b.reshape(1, d), ys)


def _lanes(values, offset):
    return jnp.zeros((1, LANES), F32).at[0, offset:offset + values.shape[0]].set(values.astype(F32))


def _split_in_proj(w_in_layer):
    offs = np.concatenate([[0], np.cumsum(IN_PROJ_SIZES)])
    seg = lambda n: w_in_layer[:, offs[n]:offs[n + 1]]
    qk_scale = DIFF_QK_DIM ** -0.5 * LOG2_E
    fox_scale = FOX_HEAD_DIM ** -0.5 * LOG2_E
    wa = jnp.concatenate([seg(0) * qk_scale, seg(1), seg(2), seg(3) * fox_scale, seg(4), seg(5)], axis=1)
    wg = jnp.concatenate([seg(7), seg(10)], axis=1)
    d = w_in_layer.shape[0]
    ws = jnp.zeros((d, LANES), F32)
    ws = ws.at[:, GATE_FF:GATE_FF + FOX_HEADS].set(seg(6))
    ws = ws.at[:, GATE_BETA:GATE_BETA + GDN_HEADS].set(seg(8))
    ws = ws.at[:, GATE_DECAY:GATE_DECAY + GDN_HEADS].set(seg(9))
    return wa.astype(BF16), wg.astype(BF16), ws


def _diff_lambda_init(layer):
    return 0.8 - 0.6 * math.exp(-0.3 * layer)


def _pick_block(seq_len, candidates):
    for c in candidates:
        if seq_len % c == 0:
            return c
    raise ValueError(f"unsupported sequence length {seq_len}")


MOE_ROWS = 512
TOKEN_BLOCK = 256
assert MOE_ROWS % ZERO_FILL_SIZES[0] == 0 and SLAB + MOE_ROWS - GROUP_ALIGN < 2 * ZERO_FILL_SIZES[0]


def _layer(stream, layer, seq_len, n_lead, alpha, p):
    t_all, d = stream.shape
    attn_tb = _pick_block(seq_len, (640, 512, 256, 128))
    wa, wg, ws = _split_in_proj(p["w_in"])
    attn, gdn_slab, gates, key_bias = _in_proj(stream, wa, wg, ws, _lanes(p["fox_forget_b"], GATE_FF),
                                               seq_len, n_lead, attn_tb)

    lv = p["diff_lambda"].astype(F32)
    lambda_init = _diff_lambda_init(layer)
    lam = (jnp.exp(jnp.sum(lv[0] * lv[1])) - jnp.exp(jnp.sum(lv[2] * lv[3])) + lambda_init).reshape(1)
    y_diff = _diff_attention(attn, lam, p["diff_subln_g"], seq_len, n_lead, lambda_init, attn_tb)

    y_fox = _fox_attention(attn, key_bias, seq_len, attn_tb)

    conv_w = jnp.zeros((SUBLANES, GDN_CONV_CH), F32).at[:CONV_K].set(p["gdn_conv_w"])
    y_gdn = _gated_deltanet(gdn_slab, gates, conv_w, _lanes(p["gdn_a_log"], GATE_DECAY),
                            _lanes(p["gdn_dt_bias"], GATE_DECAY), p["gdn_norm_g"], seq_len, attn_tb)

    rw = jnp.zeros((d, LANES), F32).at[:, :N_EXPERTS].set(p["router_w"])
    rb = _lanes(p["router_b"], 0)
    tm = TOKEN_BLOCK
    x1, lr, lrt, ge, counts = _mix_router(stream, y_diff, y_fox, y_gdn, p["w_out"].astype(BF16),
                                          p["ln1_g"], p["ln1_b"], rw, rb, alpha, tm, seq_len, n_lead)

    mb = MOE_ROWS
    nb = t_all // tm
    n_slots = -(-(t_all * TOP_K + (GROUP_ALIGN - 1) * nb * N_EXPERTS
                  + N_EXPERTS * (SLAB + mb - GROUP_ALIGN)) // mb) * mb
    n_blocks = n_slots // mb
    cnt = counts[:, 0, :N_EXPERTS].astype(I32)
    rows = (cnt + GROUP_ALIGN - 1) // GROUP_ALIGN * GROUP_ALIGN
    total = jnp.sum(rows, axis=0)
    region = (total + SLAB + mb - 1) // mb * mb
    pad_end = jnp.cumsum(region)
    pad_start = pad_end - region
    gstart = (pad_start[None, :] + jnp.cumsum(rows, axis=0) - rows).reshape(-1)
    npass = jnp.maximum((jnp.max(cnt, axis=1) + SLAB - 1) // SLAB, 1)
    block_first = jnp.arange(n_blocks, dtype=I32) * mb
    block_expert = jnp.minimum(jnp.sum((pad_end[None, :] <= block_first[:, None]).astype(I32), axis=1),
                               N_EXPERTS - 1)
    n_active = (pad_end[-1] // mb).astype(I32).reshape(1)
    group_tables = (gstart.astype(I32), cnt.reshape(-1), npass.astype(I32))

    xs = _dispatch(group_tables + ((pad_start + total).astype(I32), (region - total).astype(I32)),
                   x1, lrt, n_slots, tm)
    ys = _experts(block_expert, n_active, xs, p["expert_w1"], p["expert_b1"], p["expert_w2"], p["expert_b2"],
                  layer, mb)
    return _combine(group_tables, x1, lr, ge, p["ln2_g"], p["ln2_b"], ys, alpha, tm)


def kernel(x, meta_tokens, ln_in_g, ln_in_b, w_in, diff_lambda, diff_subln_g, fox_forget_b, gdn_conv_w,
           gdn_a_log, gdn_dt_bias, gdn_norm_g, w_out, ln1_g, ln1_b, router_w, router_b, expert_w1,
           expert_b1, expert_w2, expert_b2, ln2_g, ln2_b):
    bsz, s, d = x.shape
    depth = w_in.shape[0]
    n_lead = (-(s + N_META)) % Q_BLOCK
    if n_lead + N_META != Q_BLOCK:
        raise ValueError("sequence length must be a multiple of the query block")
    seq_len = n_lead + N_META + s
    alpha = (2 * depth) ** 0.25
    stream = _ln_in(x, meta_tokens.astype(x.dtype), ln_in_g, ln_in_b, n_lead).reshape(bsz * seq_len, d)
    per_layer = dict(w_in=w_in, diff_lambda=diff_lambda, diff_subln_g=diff_subln_g, fox_forget_b=fox_forget_b,
                     gdn_conv_w=gdn_conv_w, gdn_a_log=gdn_a_log, gdn_dt_bias=gdn_dt_bias,
                     gdn_norm_g=gdn_norm_g, w_out=w_out, ln1_g=ln1_g, ln1_b=ln1_b, router_w=router_w,
                     router_b=router_b, expert_w1=expert_w1, expert_b1=expert_b1, expert_w2=expert_w2,
                     expert_b2=expert_b2, ln2_g=ln2_g, ln2_b=ln2_b)
    stacked = ("expert_w1", "expert_b1", "expert_w2", "expert_b2")
    for layer in range(depth):
        p = {k: (v if k in stacked else v[layer]) for k, v in per_layer.items()}
        stream = _layer(stream, layer, seq_len, n_lead, alpha, p)
    return stream.reshape(bsz, seq_len, d)[:, n_lead + N_META:]
```

```python
import functools
import math

import numpy as np
import jax
import jax.numpy as jnp
from jax import lax
from jax.experimental import pallas as pl
from jax.experimental.pallas import tpu as pltpu

F32 = jnp.float32
BF16 = jnp.bfloat16
I32 = jnp.int32

CHUNK = 64
CHUNK_SHIFT = 6
Q_BLOCK = 128
N_META = 16
DIFF_HEADS = 4
DIFF_QK_DIM = 64
DIFF_V_DIM = 128
FOX_HEADS = 4
FOX_HEAD_DIM = 64
GDN_HEADS = 4
GDN_HEAD_DIM = 64
CONV_K = 4
N_EXPERTS = 32
TOP_K = 4
SWIGLU_LIMIT = 7.0
SWIGLU_ALPHA = 1.702
MASK_VALUE = -1e30
LOG2_E = 1.4426950408889634
FOX_BIAS_PIECES = 3

DIFF_QK_WIDTH = DIFF_HEADS * 2 * DIFF_QK_DIM
DIFF_V_WIDTH = DIFF_HEADS * DIFF_V_DIM
FOX_WIDTH = FOX_HEADS * FOX_HEAD_DIM
GDN_WIDTH = GDN_HEADS * GDN_HEAD_DIM
GDN_CONV_CH = 3 * GDN_WIDTH
IN_PROJ_SIZES = (DIFF_QK_WIDTH, DIFF_QK_WIDTH, DIFF_V_WIDTH, FOX_WIDTH, FOX_WIDTH, FOX_WIDTH, FOX_HEADS,
                 GDN_CONV_CH, GDN_HEADS, GDN_HEADS, GDN_WIDTH)

LANES = 128
SUBLANES = 8
VMEM_LIMIT = 56 * 1024 * 1024

GATE_FF = 0
GATE_BETA = 4
GATE_DECAY = 8

ATTN_WIDTH = 2 * DIFF_QK_WIDTH + DIFF_V_WIDTH + 3 * FOX_WIDTH
GDN_SLAB = GDN_CONV_CH + GDN_WIDTH


def _cparams(*sem):
    return pltpu.CompilerParams(dimension_semantics=tuple(sem), vmem_limit_bytes=VMEM_LIMIT)


def _dot(a, b):
    return jnp.dot(a.astype(BF16), b.astype(BF16), preferred_element_type=F32)


def _dot_nt(a, b):
    return lax.dot_general(a.astype(BF16), b.astype(BF16), (((1,), (1,)), ((), ())),
                           preferred_element_type=F32)


def _split2(x):
    hi = x.astype(BF16)
    lo = (x - hi.astype(F32)).astype(BF16)
    return hi, lo


def _split3(x):
    hi = x.astype(BF16)
    r = x - hi.astype(F32)
    mid = r.astype(BF16)
    lo = (r - mid.astype(F32)).astype(BF16)
    return hi, mid, lo


def _dot3(a, b):
    ah, al = _split2(a)
    bh, bl = _split2(b)
    return (jnp.dot(ah, bh, preferred_element_type=F32) + jnp.dot(ah, bl, preferred_element_type=F32)
            + jnp.dot(al, bh, preferred_element_type=F32))


def _dot_exact_lhs(a01, b):
    a = a01.astype(BF16)
    hi, mid, lo = _split3(b)
    return (jnp.dot(a, hi, preferred_element_type=F32) + jnp.dot(a, mid, preferred_element_type=F32)
            + jnp.dot(a, lo, preferred_element_type=F32))


_BDIMS = (((2,), (1,)), ((0,), (0,)))
_BDIMS_NT = (((2,), (2,)), ((0,), (0,)))


def _bdot(a, b):
    return lax.dot_general(a.astype(BF16), b.astype(BF16), _BDIMS, preferred_element_type=F32)


def _bdot_nt(a, b):
    return lax.dot_general(a.astype(BF16), b.astype(BF16), _BDIMS_NT, preferred_element_type=F32)


def _bdot3(a, b):
    ah, al = _split2(a)
    bh, bl = _split2(b)
    return (lax.dot_general(ah, bh, _BDIMS, preferred_element_type=F32)
            + lax.dot_general(ah, bl, _BDIMS, preferred_element_type=F32)
            + lax.dot_general(al, bh, _BDIMS, preferred_element_type=F32))


def _bdot_exact_lhs(a01, b):
    a = a01.astype(BF16)
    hi, mid, lo = _split3(b)
    return (lax.dot_general(a, hi, _BDIMS, preferred_element_type=F32)
            + lax.dot_general(a, mid, _BDIMS, preferred_element_type=F32)
            + lax.dot_general(a, lo, _BDIMS, preferred_element_type=F32))


def _layer_norm(x, g, b, eps=1e-5):
    mu = jnp.mean(x, axis=-1, keepdims=True)
    xc = x - mu
    var = jnp.mean(xc * xc, axis=-1, keepdims=True)
    return xc * lax.rsqrt(var + eps) * g + b


def _sigmoid(x):
    return 1.0 / (1.0 + jnp.exp(-x))


def _softplus(x):
    return jnp.maximum(x, 0.0) + jnp.log1p(jnp.exp(-jnp.abs(x)))


def _log_sigmoid(x):
    return -_softplus(-x)


def _ln_in_kernel(*refs, n_lead, n_sub):
    x_refs = refs[:n_sub]
    meta_ref, g_ref, b_ref, o_ref = refs[n_sub:]
    i = pl.program_id(1)
    g = g_ref[...]
    b = b_ref[...]
    blk = n_lead + N_META

    @pl.when(i == 0)
    def _():
        o_ref[0, :n_lead, :] = jnp.broadcast_to(b, (n_lead, b.shape[-1]))
        o_ref[0, n_lead:blk, :] = _layer_norm(meta_ref[...], g, b)

    @pl.when(i > 0)
    def _():
        o_ref[0, :blk, :] = _layer_norm(x_refs[0][0], g, b)

    for j in range(1, n_sub):
        o_ref[0, j * blk:(j + 1) * blk, :] = _layer_norm(x_refs[j][0], g, b)


def _ln_in(x, meta, g, b, n_lead, rows):
    bsz, s, d = x.shape
    l = n_lead + N_META + s
    blk = n_lead + N_META
    n_sub = rows // blk
    x_specs = [pl.BlockSpec((1, blk, d), functools.partial(
        lambda bb, i, j: (bb, jnp.maximum(n_sub * i + j - 1, 0), 0), j=j)) for j in range(n_sub)]
    return pl.pallas_call(
        functools.partial(_ln_in_kernel, n_lead=n_lead, n_sub=n_sub),
        out_shape=jax.ShapeDtypeStruct((bsz, l, d), F32),
        grid=(bsz, l // rows),
        in_specs=x_specs + [pl.BlockSpec((N_META, d), lambda bb, i: (0, 0)),
                            pl.BlockSpec((1, d), lambda bb, i: (0, 0)),
                            pl.BlockSpec((1, d), lambda bb, i: (0, 0))],
        out_specs=pl.BlockSpec((1, rows, d), lambda bb, i: (bb, i, 0)),
        compiler_params=_cparams("arbitrary", "arbitrary"),
        name="ln_in",
    )(*([x] * n_sub), meta, g.reshape(1, d), b.reshape(1, d))


PROJ_COL_CHUNK = 768


def _proj_kernel(x_ref, wa_ref, wg_ref, ws_ref, fb_ref, oa_ref, og_ref, os_ref, kb_ref, carry_ref, *, tm, n_lead):
    i = pl.program_id(1)
    pos = i * tm + lax.broadcasted_iota(I32, (tm, 1), 0)
    x = jnp.where(pos >= n_lead, x_ref[...], 0.0)
    xb = x.astype(BF16)
    for c in range(0, ATTN_WIDTH, PROJ_COL_CHUNK):
        oa_ref[:, c:c + PROJ_COL_CHUNK] = jnp.dot(
            xb, wa_ref[:, c:c + PROJ_COL_CHUNK], preferred_element_type=F32).astype(BF16)
    og_ref[...] = jnp.dot(xb, wg_ref[...], preferred_element_type=F32)
    gates = _dot3(x, ws_ref[...])
    os_ref[...] = gates

    @pl.when(i == 0)
    def _():
        carry_ref[...] = jnp.zeros_like(carry_ref)

    log_f = _log_sigmoid(gates + fb_ref[...])
    r = lax.broadcasted_iota(I32, (LANES, LANES), 0)
    c = lax.broadcasted_iota(I32, (LANES, LANES), 1)
    tri = jnp.where(c <= r, 1.0, 0.0).astype(BF16)
    carry = carry_ref[...]
    cums = []
    for r0 in range(0, tm, LANES):
        cums.append(_dot_exact_lhs(tri, log_f[r0:r0 + LANES, :]) + carry)
        carry = cums[-1][LANES - 1:LANES, :]
    carry_ref[...] = carry
    cum = jnp.concatenate(cums, axis=0)
    lane = lax.broadcasted_iota(I32, (1, LANES), 1)
    out = jnp.zeros((tm, LANES), F32)
    for hh in range(FOX_HEADS):
        bias = jnp.where(pos >= n_lead, -LOG2_E * cum[:, GATE_FF + hh:GATE_FF + hh + 1], MASK_VALUE)
        for n, piece in enumerate(_split3(bias)):
            out = jnp.where(lane == _fox_bias_lane(hh) + n, piece.astype(F32), out)
    kb_ref[...] = out.astype(BF16)


def _in_proj(stream, wa, wg, ws, forget_bias_lanes, seq_len, n_lead, tm):
    t_all, d = stream.shape
    nb = seq_len // tm
    row = lambda bb, i: (bb * nb + i, 0)
    const = lambda bb, i: (0, 0)
    return pl.pallas_call(
        functools.partial(_proj_kernel, tm=tm, n_lead=n_lead),
        out_shape=(jax.ShapeDtypeStruct((t_all, ATTN_WIDTH), BF16),
                   jax.ShapeDtypeStruct((t_all, GDN_SLAB), F32),
                   jax.ShapeDtypeStruct((t_all, LANES), F32),
                   jax.ShapeDtypeStruct((t_all, LANES), BF16)),
        grid=(t_all // seq_len, nb),
        in_specs=[pl.BlockSpec((tm, d), row),
                  pl.BlockSpec((d, ATTN_WIDTH), const),
                  pl.BlockSpec((d, GDN_SLAB), const),
                  pl.BlockSpec((d, LANES), const),
                  pl.BlockSpec((1, LANES), const)],
        out_specs=(pl.BlockSpec((tm, ATTN_WIDTH), row),
                   pl.BlockSpec((tm, GDN_SLAB), row),
                   pl.BlockSpec((tm, LANES), row),
                   pl.BlockSpec((tm, LANES), row)),
        scratch_shapes=[pltpu.VMEM((1, LANES), F32)],
        compiler_params=_cparams("arbitrary", "arbitrary"),
        name="in_proj",
    )(stream, wa, wg, ws, forget_bias_lanes)


def _online_softmax_step(s, vt, m_ref, l_ref, a_ref):
    m_old = m_ref[...]
    m_new = jnp.maximum(m_old, jnp.max(s, axis=0, keepdims=True))
    alpha = jnp.exp2(m_old - m_new)
    p = jnp.exp2(s - m_new)
    l_ref[...] = alpha * l_ref[...] + jnp.sum(p, axis=0, keepdims=True)
    a_ref[...] = alpha * a_ref[...] + jnp.dot(vt, p.astype(BF16), preferred_element_type=F32)
    m_ref[...] = m_new


def _pipelined_blocks(i, buf_a, buf_b, produce_first, produce, consume):
    produce_first(0, buf_a)

    def pair(t, carry):
        j = 2 * t
        produce(j + 1, buf_b)
        consume(j, buf_a)
        produce(j + 2, buf_a)
        consume(j + 1, buf_b)
        return carry

    lax.fori_loop(0, i // 2, pair, 0)

    @pl.when(i % 2 == 0)
    def _():
        consume(i, buf_a, diagonal=True)

    @pl.when(i % 2 == 1)
    def _():
        produce(i, buf_b)
        consume(i - 1, buf_a)
        consume(i, buf_b, diagonal=True)


def _transpose_to(dst_ref, src_ref, n_rows):
    for c in range(src_ref.shape[1] // LANES):
        def body(t, carry):
            r0 = pl.multiple_of(t * LANES, LANES)
            tile = src_ref[pl.ds(r0, LANES), c * LANES:(c + 1) * LANES].astype(F32)
            dst_ref[c * LANES:(c + 1) * LANES, pl.ds(r0, LANES)] = tile.T.astype(dst_ref.dtype)
            return carry
        lax.fori_loop(0, n_rows // LANES, body, 0)


def _init_softmax_state(m_ref, l_ref, a_ref):
    m_ref[...] = jnp.full(m_ref.shape, MASK_VALUE, F32)
    l_ref[...] = jnp.zeros(l_ref.shape, F32)
    a_ref[...] = jnp.zeros(a_ref.shape, F32)


def _diff_attn_kernel(lam_ref, q_ref, k_ref, v_ref, g_ref, o_ref, vt_ref, sa_ref, sb_ref,
                      m0, l0, a0, m1, l1, a1, *, tb, n_lead, seq_len, out_scale):
    i = pl.program_id(2)

    @pl.when(i == 0)
    def _():
        _transpose_to(vt_ref, v_ref, seq_len)

    qt = q_ref[...].astype(F32).T
    row = lax.broadcasted_iota(I32, (LANES, 1), 0)
    qt0 = jnp.where(row < DIFF_QK_DIM, qt, 0.0).astype(BF16)
    qt1 = jnp.where(row >= DIFF_QK_DIM, qt, 0.0).astype(BF16)
    _init_softmax_state(m0, l0, a0)
    _init_softmax_state(m1, l1, a1)

    def produce(j, buf, pad_bias=False):
        start = pl.multiple_of(j * tb, tb)
        k = k_ref[pl.ds(start, tb), :]
        for half, qth in enumerate((qt0, qt1)):
            s = jnp.dot(k, qth, preferred_element_type=F32)
            if pad_bias:
                kpos = lax.broadcasted_iota(I32, (tb, 1), 0)
                s = s + jnp.where(kpos >= n_lead, 0.0, MASK_VALUE)
            buf[half] = s

    def consume(j, buf, diagonal=False):
        start = pl.multiple_of(j * tb, tb)
        vt = vt_ref[:, pl.ds(start, tb)]
        for half, state in enumerate(((m0, l0, a0), (m1, l1, a1))):
            s = buf[half]
            if diagonal:
                kpos = j * tb + lax.broadcasted_iota(I32, (tb, 1), 0)
                qpos = i * tb + lax.broadcasted_iota(I32, (1, tb), 1)
                s = jnp.where(kpos >> CHUNK_SHIFT <= qpos >> CHUNK_SHIFT, s, MASK_VALUE)
            _online_softmax_step(s, vt, *state)

    _pipelined_blocks(i, sa_ref, sb_ref, functools.partial(produce, pad_bias=True), produce, consume)

    lam = lam_ref[0]
    o = a0[...] / l0[...] - lam * (a1[...] / l1[...])
    o = o * lax.rsqrt(jnp.mean(o * o, axis=0, keepdims=True) + 1e-5) * g_ref[...] * out_scale
    o_ref[...] = o.T.astype(o_ref.dtype)


def _diff_attention(attn, lam, subln_g, seq_len, n_lead, lambda_init, tb):
    t_all = attn.shape[0]
    bsz = t_all // seq_len
    nq = seq_len // tb
    kcol = DIFF_QK_WIDTH // LANES
    vcol = 2 * DIFF_QK_WIDTH // LANES
    scratch = [pltpu.VMEM((DIFF_V_DIM, seq_len), BF16),
               pltpu.VMEM((2, tb, tb), F32), pltpu.VMEM((2, tb, tb), F32)]
    for _ in range(2):
        scratch += [pltpu.VMEM((1, tb), F32), pltpu.VMEM((1, tb), F32), pltpu.VMEM((DIFF_V_DIM, tb), F32)]
    return pl.pallas_call(
        functools.partial(_diff_attn_kernel, tb=tb, n_lead=n_lead, seq_len=seq_len,
                          out_scale=1.0 - lambda_init),
        out_shape=jax.ShapeDtypeStruct((t_all, DIFF_V_WIDTH), BF16),
        grid_spec=pltpu.PrefetchScalarGridSpec(
            num_scalar_prefetch=1,
            grid=(bsz, DIFF_HEADS, nq),
            in_specs=[pl.BlockSpec((tb, LANES), lambda b, h, i, lam: (b * nq + i, h)),
                      pl.BlockSpec((seq_len, LANES), lambda b, h, i, lam: (b, kcol + h)),
                      pl.BlockSpec((seq_len, LANES), lambda b, h, i, lam: (b, vcol + h)),
                      pl.BlockSpec((DIFF_V_DIM, 1), lambda b, h, i, lam: (0, 0))],
            out_specs=pl.BlockSpec((tb, DIFF_V_DIM), lambda b, h, i, lam: (b * nq + i, h)),
            scratch_shapes=scratch),
        compiler_params=_cparams("arbitrary", "arbitrary", "arbitrary"),
        name="diff_attention",
    )(lam, attn, attn, attn, subln_g.reshape(DIFF_V_DIM, 1))


def _fox_bias_lane(hh):
    return (1 - hh % 2) * FOX_HEAD_DIM + FOX_BIAS_PIECES * (hh // 2)


def _fox_attn_kernel(q_ref, k_ref, v_ref, kb_ref, o_ref, vt_ref, sa_ref, sb_ref, *scratch, tb, seq_len):
    i = pl.program_id(1)
    pairs = FOX_HEADS // 2

    @pl.when(i == 0)
    def _():
        _transpose_to(vt_ref, v_ref, seq_len)

    row = lax.broadcasted_iota(I32, (LANES, 1), 0)
    lane = lax.broadcasted_iota(I32, (1, LANES), 1)
    qts = []
    for p in range(pairs):
        qt = q_ref[:, p * LANES:(p + 1) * LANES].astype(F32).T
        for r in range(2):
            own = jnp.logical_and(row >= r * FOX_HEAD_DIM, row < (r + 1) * FOX_HEAD_DIM)
            b0 = _fox_bias_lane(2 * p + r)
            ones = jnp.logical_and(row >= b0, row < b0 + FOX_BIAS_PIECES)
            qts.append(jnp.where(own, qt, jnp.where(ones, 1.0, 0.0)).astype(BF16))
    for hh in range(FOX_HEADS):
        _init_softmax_state(*scratch[3 * hh:3 * hh + 3])

    def produce(j, buf):
        start = pl.multiple_of(j * tb, tb)
        kb = kb_ref[pl.ds(start, tb), :]
        for p in range(pairs):
            k = k_ref[pl.ds(start, tb), p * LANES:(p + 1) * LANES]
            for r in range(2):
                own = jnp.logical_and(lane >= r * FOX_HEAD_DIM, lane < (r + 1) * FOX_HEAD_DIM)
                buf[2 * p + r] = jnp.dot(jnp.where(own, k, kb), qts[2 * p + r],
                                         preferred_element_type=F32)

    def consume(j, buf, diagonal=False):
        start = pl.multiple_of(j * tb, tb)
        for hh in range(FOX_HEADS):
            vt = vt_ref[hh * FOX_HEAD_DIM:(hh + 1) * FOX_HEAD_DIM, pl.ds(start, tb)]
            s = buf[hh]
            if diagonal:
                kpos = j * tb + lax.broadcasted_iota(I32, (tb, 1), 0)
                qpos = i * tb + lax.broadcasted_iota(I32, (1, tb), 1)
                s = jnp.where(kpos <= qpos, s, MASK_VALUE)
            _online_softmax_step(s, vt, *scratch[3 * hh:3 * hh + 3])

    _pipelined_blocks(i, sa_ref, sb_ref, produce, produce, consume)

    for p in range(pairs):
        _, l_lo, a_lo = scratch[6 * p:6 * p + 3]
        _, l_hi, a_hi = scratch[6 * p + 3:6 * p + 6]
        o = jnp.concatenate([a_lo[...] / l_lo[...], a_hi[...] / l_hi[...]], axis=0)
        o_ref[:, p * LANES:(p + 1) * LANES] = o.T.astype(o_ref.dtype)


def _fox_attention(attn, key_bias, seq_len, tb):
    t_all = attn.shape[0]
    bsz = t_all // seq_len
    nq = seq_len // tb
    base = (2 * DIFF_QK_WIDTH + DIFF_V_WIDTH) // FOX_WIDTH
    scratch = [pltpu.VMEM((FOX_WIDTH, seq_len), BF16),
               pltpu.VMEM((FOX_HEADS, tb, tb), F32), pltpu.VMEM((FOX_HEADS, tb, tb), F32)]
    for _ in range(FOX_HEADS):
        scratch += [pltpu.VMEM((1, tb), F32), pltpu.VMEM((1, tb), F32), pltpu.VMEM((FOX_HEAD_DIM, tb), F32)]
    return pl.pallas_call(
        functools.partial(_fox_attn_kernel, tb=tb, seq_len=seq_len),
        out_shape=jax.ShapeDtypeStruct((t_all, FOX_WIDTH), BF16),
        grid=(bsz, nq),
        in_specs=[pl.BlockSpec((tb, FOX_WIDTH), lambda b, i: (b * nq + i, base)),
                  pl.BlockSpec((seq_len, FOX_WIDTH), lambda b, i: (b, base + 1)),
                  pl.BlockSpec((seq_len, FOX_WIDTH), lambda b, i: (b, base + 2)),
                  pl.BlockSpec((seq_len, LANES), lambda b, i: (b, 0))],
        out_specs=pl.BlockSpec((tb, FOX_WIDTH), lambda b, i: (b * nq + i, 0)),
        scratch_shapes=scratch,
        compiler_params=_cparams("arbitrary", "arbitrary"),
        name="fox_attention",
    )(attn, attn, attn, key_bias)


def _gdn_kernel(x_ref, gate_ref, cw_ref, alog_ref, dtb_ref, ng_ref, o_ref,
                halo_ref, xs_ref, state_ref, *, rows):
    cg = pl.program_id(1)
    g_chunks = rows // CHUNK
    hd = GDN_HEAD_DIM

    @pl.when(cg == 0)
    def _():
        halo_ref[...] = jnp.zeros_like(halo_ref)
        state_ref[...] = jnp.zeros_like(state_ref)

    xin = x_ref[:, :GDN_CONV_CH]
    xs_ref[0:SUBLANES, :] = halo_ref[...]
    xs_ref[SUBLANES:SUBLANES + rows, :] = xin
    halo_ref[...] = xin[rows - SUBLANES:rows, :]
    y = jnp.zeros((rows, GDN_CONV_CH), F32)
    for tap in range(CONV_K):
        back = CONV_K - 1 - tap
        y = y + cw_ref[tap:tap + 1, :] * xs_ref[SUBLANES - back:SUBLANES - back + rows, :]
    y = y * _sigmoid(y)

    gates = gate_ref[...]
    beta_all = _sigmoid(gates)
    g_all = -jnp.exp(alog_ref[...]) * _softplus(gates + dtb_ref[...])
    rowi = lax.broadcasted_iota(I32, (rows, 1), 0) & (CHUNK - 1)
    gc_all = g_all
    shift = 1
    while shift < CHUNK:
        gc_all = gc_all + jnp.where(rowi >= shift, pltpu.roll(gc_all, shift, 0), 0.0)
        shift *= 2

    ri = lax.broadcasted_iota(I32, (CHUNK, CHUNK), 0)
    ci = lax.broadcasted_iota(I32, (CHUNK, CHUNK), 1)
    eye = jnp.where(ri == ci, 1.0, 0.0).astype(F32)
    causal = (ci <= ri)[None]
    strict = (ci < ri)[None]
    eye_b = jnp.broadcast_to(eye[None], (g_chunks, CHUNK, CHUNK))

    prep = []
    for h in range(GDN_HEADS):
        def head3(base):
            return y[:, base + h * hd:base + (h + 1) * hd]
        q = head3(0)
        k = head3(GDN_WIDTH)
        v = head3(2 * GDN_WIDTH)
        q = q * lax.rsqrt(jnp.sum(q * q, axis=-1, keepdims=True) + 1e-6) * (hd ** -0.5)
        k = k * lax.rsqrt(jnp.sum(k * k, axis=-1, keepdims=True) + 1e-6)
        beta = beta_all[:, GATE_BETA + h:GATE_BETA + h + 1]
        gcol = jnp.broadcast_to(gc_all[:, GATE_DECAY + h:GATE_DECAY + h + 1], (rows, hd))
        egc = jnp.exp(gcol)

        q3 = q.reshape(g_chunks, CHUNK, hd)
        k3 = k.reshape(g_chunks, CHUNK, hd)
        gcol3 = gcol.reshape(g_chunks, CHUNK, CHUNK)
        beta3 = jnp.broadcast_to(beta, (rows, hd)).reshape(g_chunks, CHUNK, hd)
        grow3 = jnp.swapaxes(gcol3, 1, 2)
        decay = jnp.exp(jnp.where(causal, gcol3 - grow3, MASK_VALUE))
        a = jnp.where(strict, _bdot_nt(k3, k3) * beta3 * decay, 0.0)

        x = -a
        tinv = eye_b + x
        span = 2
        while span < CHUNK:
            x = _bdot(x, x)
            tinv = tinv + _bdot(tinv, x)
            span *= 2

        vb = (v * beta).reshape(g_chunks, CHUNK, hd)
        kb = (k * beta * egc).reshape(g_chunks, CHUNK, hd)
        uw3 = _bdot(tinv, jnp.concatenate([vb, kb], axis=-1))
        qk3 = _bdot_nt(q3, k3) * decay
        qd3 = (q * egc).reshape(g_chunks, CHUNK, hd)
        glast3 = gcol3[:, CHUNK - 1:CHUNK, :]
        kdt3 = jnp.swapaxes(k3 * jnp.exp(glast3 - gcol3), 1, 2)
        cd3 = jnp.exp(glast3)
        k_uw = _bdot(kdt3, uw3)
        q_uw = _bdot(qk3, uw3)
        prep.append((k_uw[..., hd:], k_uw[..., :hd], qd3 - q_uw[..., hd:], q_uw[..., :hd], cd3))

    states = [state_ref[h] for h in range(GDN_HEADS)]
    outs = [[] for _ in range(GDN_HEADS)]
    for c in range(g_chunks):
        for h in range(GDN_HEADS):
            kw3, ku3, qp3, o03, cd3 = prep[h]
            state = states[h]
            outs[h].append(_dot(qp3[c], state) + o03[c])
            states[h] = state * cd3[c] + (ku3[c] - _dot(kw3[c], state))

    for h in range(GDN_HEADS):
        state_ref[h] = states[h]
        o = jnp.concatenate(outs[h], axis=0)
        o = o * lax.rsqrt(jnp.mean(o * o, axis=-1, keepdims=True) + 1e-6) * ng_ref[...]
        z = x_ref[:, GDN_CONV_CH + h * hd:GDN_CONV_CH + (h + 1) * hd]
        o_ref[:, h * hd:(h + 1) * hd] = (o * (z * _sigmoid(z))).astype(o_ref.dtype)


def _gated_deltanet(gdn_slab, gates, conv_w, a_log_lanes, dt_bias_lanes, norm_g, seq_len, rows):
    t_all = gdn_slab.shape[0]
    bsz = t_all // seq_len
    nb = seq_len // rows
    row = lambda b, i: (b * nb + i, 0)
    const = lambda b, i: (0, 0)
    return pl.pallas_call(
        functools.partial(_gdn_kernel, rows=rows),
        out_shape=jax.ShapeDtypeStruct((t_all, GDN_WIDTH), BF16),
        grid=(bsz, nb),
        in_specs=[pl.BlockSpec((rows, GDN_SLAB), row),
                  pl.BlockSpec((rows, LANES), row),
                  pl.BlockSpec((SUBLANES, GDN_CONV_CH), const),
                  pl.BlockSpec((1, LANES), const),
                  pl.BlockSpec((1, LANES), const),
                  pl.BlockSpec((1, GDN_HEAD_DIM), const)],
        out_specs=pl.BlockSpec((rows, GDN_WIDTH), row),
        scratch_shapes=[pltpu.VMEM((SUBLANES, GDN_CONV_CH), F32),
                        pltpu.VMEM((rows + SUBLANES, GDN_CONV_CH), F32),
                        pltpu.VMEM((GDN_HEADS, GDN_HEAD_DIM, GDN_HEAD_DIM), F32)],
        compiler_params=_cparams("arbitrary", "arbitrary"),
        name="gated_deltanet",
    )(gdn_slab, gates, conv_w, a_log_lanes, dt_bias_lanes, norm_g.reshape(1, GDN_HEAD_DIM))


def _mix_router_kernel(x_ref, yd_ref, yf_ref, yg_ref, wd_ref, wf_ref, wg_ref, g_ref, b_ref,
                       rw_ref, rb_ref, x1_ref, lr_ref, lrt_ref, ge_ref, cnt_ref, *,
                       tm, rb_rows, alpha, seq_len, n_lead, n_seq):
    mix = (jnp.dot(yd_ref[...], wd_ref[...], preferred_element_type=F32)
           + jnp.dot(yf_ref[...], wf_ref[...], preferred_element_type=F32)
           + jnp.dot(yg_ref[...], wg_ref[...], preferred_element_type=F32))
    x1 = _layer_norm(alpha * x_ref[...] + mix, g_ref[...], b_ref[...])
    x1_ref[...] = x1

    lane = lax.broadcasted_iota(I32, (tm, LANES), 1)
    lane_f = lane.astype(F32)
    logits = _dot3(x1, rw_ref[...]) + rb_ref[...]
    vals = jnp.where(lane < N_EXPERTS, logits, -jnp.inf)
    hits, tops = [], []
    for _ in range(TOP_K):
        top = jnp.max(vals, axis=-1, keepdims=True)
        idx = jnp.min(jnp.where(vals == top, lane_f, float(LANES)), axis=-1, keepdims=True)
        hit = lane_f == idx
        vals = jnp.where(hit, -jnp.inf, vals)
        hits.append(hit)
        tops.append(top)
    exps = [jnp.exp(t - tops[0]) for t in tops]
    denom = exps[0] + exps[1] + exps[2] + exps[3]
    row = pl.program_id(0) * tm + lax.broadcasted_iota(I32, (tm, 1), 0)
    is_pad = row < 0
    for sq in range(n_seq):
        is_pad = jnp.logical_or(is_pad, jnp.logical_and(row >= sq * seq_len, row < sq * seq_len + n_lead))
    onehot = jnp.zeros((tm, LANES), F32)
    gate_by_expert = jnp.zeros((tm, LANES), F32)
    for kk in range(TOP_K):
        hit = jnp.logical_and(hits[kk], jnp.logical_not(is_pad))
        onehot = jnp.where(hit, 1.0, onehot)
        gate_by_expert = jnp.where(hit, exps[kk] / denom, gate_by_expert)
    ge_ref[...] = gate_by_expert

    r = lax.broadcasted_iota(I32, (rb_rows, rb_rows), 0)
    c = lax.broadcasted_iota(I32, (rb_rows, rb_rows), 1)
    tri = jnp.where(c < r, 1.0, 0.0).astype(BF16)
    for blk in range(tm // rb_rows):
        oh = onehot[blk * rb_rows:(blk + 1) * rb_rows, :]
        rank = jnp.dot(tri, oh.astype(BF16), preferred_element_type=F32)
        local = jnp.where(oh > 0.0, rank, -1.0)
        lr_ref[blk * rb_rows:(blk + 1) * rb_rows, :] = local
        lrt_ref[blk * N_EXPERTS:(blk + 1) * N_EXPERTS, :] = local.T[:N_EXPERTS, :]
        cnt_ref[blk] = jnp.sum(oh, axis=0, keepdims=True)


def _mix_router(stream, yd, yf, yg, w_out_b, ln_g, ln_b, rw, rb, alpha, rb_rows, seq_len, n_lead):
    t_all, d = stream.shape
    n_rb = t_all // rb_rows
    per_step = max(k for k in (5, 4, 3, 2, 1) if n_rb % k == 0)
    tm = per_step * rb_rows
    row = lambda i: (i, 0)
    const = lambda i: (0, 0)
    wd = w_out_b[:DIFF_V_WIDTH]
    wf = w_out_b[DIFF_V_WIDTH:DIFF_V_WIDTH + FOX_WIDTH]
    wg = w_out_b[DIFF_V_WIDTH + FOX_WIDTH:]
    return pl.pallas_call(
        functools.partial(_mix_router_kernel, tm=tm, rb_rows=rb_rows, alpha=alpha, seq_len=seq_len,
                          n_lead=n_lead, n_seq=t_all // seq_len),
        out_shape=(jax.ShapeDtypeStruct((t_all, d), F32),
                   jax.ShapeDtypeStruct((t_all, LANES), F32),
                   jax.ShapeDtypeStruct((n_rb * N_EXPERTS, rb_rows), F32),
                   jax.ShapeDtypeStruct((t_all, LANES), F32),
                   jax.ShapeDtypeStruct((n_rb, 1, LANES), F32)),
        grid=(t_all // tm,),
        in_specs=[pl.BlockSpec((tm, d), row),
                  pl.BlockSpec((tm, DIFF_V_WIDTH), row),
                  pl.BlockSpec((tm, FOX_WIDTH), row),
                  pl.BlockSpec((tm, GDN_WIDTH), row),
                  pl.BlockSpec((DIFF_V_WIDTH, d), const),
                  pl.BlockSpec((FOX_WIDTH, d), const),
                  pl.BlockSpec((GDN_WIDTH, d), const),
                  pl.BlockSpec((1, d), const),
                  pl.BlockSpec((1, d), const),
                  pl.BlockSpec((d, LANES), const),
                  pl.BlockSpec((1, LANES), const)],
        out_specs=(pl.BlockSpec((tm, d), row),
                   pl.BlockSpec((tm, LANES), row),
                   pl.BlockSpec((per_step * N_EXPERTS, rb_rows), row),
                   pl.BlockSpec((tm, LANES), row),
                   pl.BlockSpec((per_step, 1, LANES), lambda i: (i, 0, 0))),
        compiler_params=_cparams("arbitrary"),
        name="mix_router",
    )(stream, yd, yf, yg, wd, wf, wg, ln_g.reshape(1, d), ln_b.reshape(1, d), rw, rb)


SLAB = 64
GROUP_ALIGN = 8
ZERO_FILL_SIZES = (512, 256, 128, 64, 32, 16, 8)


def _slab_dma(gstart_ref, gcnt_ref, bb, c, e, action, make_copy):
    idx = bb * N_EXPERTS + e
    start = pl.multiple_of(gstart_ref[idx] + c * SLAB, GROUP_ALIGN)
    small = gcnt_ref[idx] - c * SLAB <= SLAB // 2
    for rows, cond in ((SLAB // 2, small), (SLAB, jnp.logical_not(small))):
        @pl.when(cond)
        def _():
            getattr(make_copy(start, rows), action)()


def _slab_rows(src_ref, rank_row0):
    want = (rank_row0 + lax.broadcasted_iota(I32, (SLAB, 1), 0)).astype(F32)
    return jnp.concatenate([jnp.where(src_ref[e:e + 1, :] == want, 1.0, 0.0).astype(BF16)
                            for e in range(N_EXPERTS)], axis=0)


def _dispatch_kernel(gstart_ref, gcnt_ref, npass_ref, zstart_ref, zrows_ref, x_ref, lrt_ref, xs_ref,
                     zbuf, zero_ref, sem, zsem, *, nb, n_slots):
    b = pl.program_id(0)
    slot = b % 2

    def slab_copy(bb, c, e, s, action):
        _slab_dma(gstart_ref, gcnt_ref, bb, c, e, action, lambda start, rows: pltpu.make_async_copy(
            zbuf.at[s, pl.ds(e * SLAB, rows)], xs_ref.at[pl.ds(start, rows)], sem.at[s]))

    def zero_fill(act):
        for e in range(N_EXPERTS):
            rows = zrows_ref[e]
            for size in ZERO_FILL_SIZES:
                @pl.when((rows & size) != 0)
                def _():
                    start = pl.multiple_of(zstart_ref[e] + (rows & ~(2 * size - 1)), GROUP_ALIGN)
                    act(pltpu.make_async_copy(zero_ref.at[pl.ds(0, size)], xs_ref.at[pl.ds(start, size)], zsem))

    def zero_tail(act):
        first = zstart_ref[N_EXPERTS - 1] + zrows_ref[N_EXPERTS - 1]
        size = ZERO_FILL_SIZES[0]

        def body(t, carry):
            start = pl.multiple_of(first + t * size, GROUP_ALIGN)
            act(pltpu.make_async_copy(zero_ref, xs_ref.at[pl.ds(start, size)], zsem))
            return carry

        lax.fori_loop(0, (n_slots - first) // size, body, 0)

    @pl.when(b == 0)
    def _():
        zero_ref[...] = jnp.zeros_like(zero_ref)
        for fill in (zero_fill, zero_tail):
            fill(lambda cp: cp.start())
        for fill in (zero_fill, zero_tail):
            fill(lambda cp: cp.wait())

    xb = x_ref[...].astype(BF16)
    zbuf[slot] = jnp.dot(_slab_rows(lrt_ref, 0), xb, preferred_element_type=F32)

    @pl.when(b > 0)
    def _():
        @pl.when(npass_ref[b - 1] <= 1)
        def _():
            for e in range(N_EXPERTS):
                slab_copy(b - 1, 0, e, 1 - slot, "wait")

    for e in range(N_EXPERTS):
        slab_copy(b, 0, e, slot, "start")

    @pl.when(npass_ref[b] > 1)
    def _():
        for e in range(N_EXPERTS):
            slab_copy(b, 0, e, slot, "wait")

        def extra_pass(c, carry):
            zbuf[slot] = jnp.dot(_slab_rows(lrt_ref, c * SLAB), xb, preferred_element_type=F32)
            for action in ("start", "wait"):
                for e in range(N_EXPERTS):
                    @pl.when(gcnt_ref[b * N_EXPERTS + e] > c * SLAB)
                    def _():
                        slab_copy(b, c, e, slot, action)
            return carry

        lax.fori_loop(1, npass_ref[b], extra_pass, 0)

    @pl.when(b == nb - 1)
    def _():
        @pl.when(npass_ref[b] <= 1)
        def _():
            for e in range(N_EXPERTS):
                slab_copy(b, 0, e, slot, "wait")


def _dispatch(tables, x1, lrt, n_slots, tm):
    t_all, d = x1.shape
    nb = t_all // tm
    ntab = len(tables)
    return pl.pallas_call(
        functools.partial(_dispatch_kernel, nb=nb, n_slots=n_slots),
        out_shape=jax.ShapeDtypeStruct((n_slots, d), F32),
        grid_spec=pltpu.PrefetchScalarGridSpec(
            num_scalar_prefetch=ntab,
            grid=(nb,),
            in_specs=[pl.BlockSpec((tm, d), lambda i, *_: (i, 0)),
                      pl.BlockSpec((N_EXPERTS, tm), lambda i, *_: (i, 0))],
            out_specs=pl.BlockSpec(memory_space=pl.ANY),
            scratch_shapes=[pltpu.VMEM((2, N_EXPERTS * SLAB, d), F32),
                            pltpu.VMEM((ZERO_FILL_SIZES[0], d), F32),
                            pltpu.SemaphoreType.DMA((2,)),
                            pltpu.SemaphoreType.DMA]),
        compiler_params=_cparams("arbitrary"),
        name="moe_dispatch",
    )(*tables, x1, lrt)


EXPERT_W_CHUNKS = 4


def _expert_kernel(be_ref, na_ref, x_ref, w1_ref, b1_ref, w2_ref, b2_ref, o_ref, w1b_ref, w2b_ref, *, d_expert):
    i = pl.program_id(0)
    active = i < na_ref[0]
    new_expert = jnp.logical_or(i == 0, be_ref[i] != be_ref[jnp.maximum(i - 1, 0)])

    def ffn(w1_chunk, w2_chunk):
        xb = x_ref[...].astype(BF16)
        hid = jnp.concatenate([jnp.dot(xb, w1_chunk(c), preferred_element_type=F32)
                               for c in range(EXPERT_W_CHUNKS)], axis=1) + b1_ref[0]
        glu = jnp.minimum(hid[:, :d_expert], SWIGLU_LIMIT)
        lin = jnp.clip(hid[:, d_expert:], -SWIGLU_LIMIT, SWIGLU_LIMIT)
        act = (glu * _sigmoid(SWIGLU_ALPHA * glu) * (lin + 1.0)).astype(BF16)
        o_ref[...] = jnp.concatenate([jnp.dot(act, w2_chunk(c), preferred_element_type=F32)
                                      for c in range(EXPERT_W_CHUNKS)], axis=1) + b2_ref[0]

    def chunk_of(ref, c, lead=()):
        width = ref.shape[-1] // EXPERT_W_CHUNKS
        return ref[lead + (slice(None), slice(c * width, (c + 1) * width))]

    def cast_chunk(src_ref, dst_ref, c):
        wc = chunk_of(src_ref, c, (0,)).astype(BF16)
        width = dst_ref.shape[-1] // EXPERT_W_CHUNKS
        dst_ref[:, c * width:(c + 1) * width] = wc
        return wc

    @pl.when(jnp.logical_and(active, new_expert))
    def _():
        ffn(functools.partial(cast_chunk, w1_ref, w1b_ref), functools.partial(cast_chunk, w2_ref, w2b_ref))

    @pl.when(jnp.logical_and(active, jnp.logical_not(new_expert)))
    def _():
        ffn(functools.partial(chunk_of, w1b_ref), functools.partial(chunk_of, w2b_ref))

    @pl.when(jnp.logical_not(active))
    def _():
        o_ref[...] = jnp.zeros_like(o_ref)


def _experts(block_expert, n_active, xs, w1_all, b1, w2_all, b2, layer, mb):
    n_slots, d = xs.shape
    depth, n_exp, _, f2 = w1_all.shape
    f = f2 // 2
    nb = n_slots // mb
    w1 = w1_all.reshape(depth * n_exp, d, f2)
    w2 = w2_all.reshape(depth * n_exp, f, d)
    block_expert = block_expert + layer * n_exp
    return pl.pallas_call(
        functools.partial(_expert_kernel, d_expert=f),
        out_shape=jax.ShapeDtypeStruct((n_slots, d), F32),
        grid_spec=pltpu.PrefetchScalarGridSpec(
            num_scalar_prefetch=2,
            grid=(nb,),
            in_specs=[pl.BlockSpec((mb, d), lambda i, be, na: (jnp.minimum(i, na[0] - 1), 0)),
                      pl.BlockSpec((1, d, f2), lambda i, be, na: (be[i], 0, 0)),
                      pl.BlockSpec((1, 1, f2), lambda i, be, na: (be[i], 0, 0)),
                      pl.BlockSpec((1, f, d), lambda i, be, na: (be[i], 0, 0)),
                      pl.BlockSpec((1, 1, d), lambda i, be, na: (be[i], 0, 0))],
            out_specs=pl.BlockSpec((mb, d), lambda i, be, na: (i, 0)),
            scratch_shapes=[pltpu.VMEM((d, f2), BF16), pltpu.VMEM((f, d), BF16)]),
        compiler_params=_cparams("arbitrary"),
        name="moe_experts",
    )(block_expert, n_active, xs, w1, b1.reshape(depth * n_exp, 1, f2), w2, b2.reshape(depth * n_exp, 1, d))


def _combine_kernel(gstart_ref, gcnt_ref, npass_ref, x1_ref, lr_ref, ge_ref, g_ref, b_ref, ys_ref, o_ref,
                    ybuf, acc_ref, sem, *, tm, nb, alpha):
    b = pl.program_id(0)
    slot = b % 2

    def slab_copy(bb, c, e, s, action):
        _slab_dma(gstart_ref, gcnt_ref, bb, c, e, action, lambda start, rows: pltpu.make_async_copy(
            ys_ref.at[pl.ds(start, rows)], ybuf.at[s, pl.ds(e * SLAB, rows)], sem.at[s]))

    @pl.when(b == 0)
    def _():
        ybuf[...] = jnp.zeros_like(ybuf)
        for e in range(N_EXPERTS):
            slab_copy(0, 0, e, 0, "start")

    @pl.when(b + 1 < nb)
    def _():
        for e in range(N_EXPERTS):
            slab_copy(b + 1, 0, e, 1 - slot, "start")

    for e in range(N_EXPERTS):
        slab_copy(b, 0, e, slot, "wait")

    lane = lax.broadcasted_iota(I32, (1, LANES), 1)
    lr = lr_ref[...]
    ge = ge_ref[...]

    def weighted_sum(rank0):
        his, los = [], []
        for col0 in range(0, N_EXPERTS * SLAB, LANES):
            col = col0 + lane
            rank = jnp.full((tm, LANES), -2.0, F32)
            gate = jnp.zeros((tm, LANES), F32)
            want = jnp.zeros((1, LANES), I32)
            for e in range(col0 // SLAB, min((col0 + LANES - 1) // SLAB, N_EXPERTS - 1) + 1):
                inside = jnp.logical_and(col >= e * SLAB, col < (e + 1) * SLAB)
                rank = jnp.where(inside, lr[:, e:e + 1], rank)
                gate = jnp.where(inside, ge[:, e:e + 1], gate)
                want = jnp.where(inside, col - e * SLAB + rank0, want)
            w = jnp.where(rank == want.astype(F32), gate, 0.0)
            hi = w.astype(BF16)
            his.append(hi)
            los.append((w - hi.astype(F32)).astype(BF16))
        sel = jnp.concatenate([jnp.concatenate(his, axis=1), jnp.concatenate(los, axis=1)], axis=0)
        r = jnp.dot(sel, ybuf[slot].astype(BF16), preferred_element_type=F32)
        return r[:tm] + r[tm:]

    acc_ref[...] = alpha * x1_ref[...] + weighted_sum(0)

    @pl.when(npass_ref[b] > 1)
    def _():
        def extra_pass(c, carry):
            for action in ("start", "wait"):
                for e in range(N_EXPERTS):
                    @pl.when(gcnt_ref[b * N_EXPERTS + e] > c * SLAB)
                    def _():
                        slab_copy(b, c, e, slot, action)
            acc_ref[...] += weighted_sum(c * SLAB)
            return carry

        lax.fori_loop(1, npass_ref[b], extra_pass, 0)

    o_ref[...] = _layer_norm(acc_ref[...], g_ref[...], b_ref[...])


def _combine(tables, x1, lr, ge, ln_g, ln_b, ys, alpha, tm):
    t_all, d = x1.shape
    nb = t_all // tm
    ntab = len(tables)
    row = lambda i, *_: (i, 0)
    const = lambda i, *_: (0, 0)
    return pl.pallas_call(
        functools.partial(_combine_kernel, tm=tm, nb=nb, alpha=alpha),
        out_shape=jax.ShapeDtypeStruct((t_all, d), F32),
        grid_spec=pltpu.PrefetchScalarGridSpec(
            num_scalar_prefetch=ntab,
            grid=(nb,),
            in_specs=[pl.BlockSpec((tm, d), row),
                      pl.BlockSpec((tm, LANES), row),
                      pl.BlockSpec((tm, LANES), row),
                      pl.BlockSpec((1, d), const),
                      pl.BlockSpec((1, d), const),
                      pl.BlockSpec(memory_space=pl.ANY)],
            out_specs=pl.BlockSpec((tm, d), row),
            scratch_shapes=[pltpu.VMEM((2, N_EXPERTS * SLAB, d), F32),
                            pltpu.VMEM((tm, d), F32),
                            pltpu.SemaphoreType.DMA((2,))]),
        compiler_params=_cparams("arbitrary"),
        name="moe_combine",
    )(*tables, x1, lr, ge, ln_g.reshape(1, d), ln_b.reshape(1, d), ys)


def _lanes(values, offset):
    return jnp.zeros((1, LANES), F32).at[0, offset:offset + values.shape[0]].set(values.astype(F32))


def _split_in_proj(w_in_layer):
    offs = np.concatenate([[0], np.cumsum(IN_PROJ_SIZES)])
    seg = lambda n: w_in_layer[:, offs[n]:offs[n + 1]]
    qk_scale = DIFF_QK_DIM ** -0.5 * LOG2_E
    fox_scale = FOX_HEAD_DIM ** -0.5 * LOG2_E
    wa = jnp.concatenate([seg(0) * qk_scale, seg(1), seg(2), seg(3) * fox_scale, seg(4), seg(5)], axis=1)
    wg = jnp.concatenate([seg(7), seg(10)], axis=1)
    d = w_in_layer.shape[0]
    ws = jnp.zeros((d, LANES), F32)
    ws = ws.at[:, GATE_FF:GATE_FF + FOX_HEADS].set(seg(6))
    ws = ws.at[:, GATE_BETA:GATE_BETA + GDN_HEADS].set(seg(8))
    ws = ws.at[:, GATE_DECAY:GATE_DECAY + GDN_HEADS].set(seg(9))
    return wa.astype(BF16), wg.astype(BF16), ws


def _diff_lambda_init(layer):
    return 0.8 - 0.6 * math.exp(-0.3 * layer)


def _pick_block(seq_len, candidates):
    for c in candidates:
        if seq_len % c == 0:
            return c
    raise ValueError(f"unsupported sequence length {seq_len}")


MOE_ROWS = 512
TOKEN_BLOCK = 256
assert MOE_ROWS % ZERO_FILL_SIZES[0] == 0 and SLAB + MOE_ROWS - GROUP_ALIGN < 2 * ZERO_FILL_SIZES[0]


def _layer(stream, layer, seq_len, n_lead, alpha, p):
    t_all, d = stream.shape
    attn_tb = _pick_block(seq_len, (640, 512, 256, 128))
    wa, wg, ws = _split_in_proj(p["w_in"])
    attn, gdn_slab, gates, key_bias = _in_proj(stream, wa, wg, ws, _lanes(p["fox_forget_b"], GATE_FF),
                                               seq_len, n_lead, attn_tb)

    lv = p["diff_lambda"].astype(F32)
    lambda_init = _diff_lambda_init(layer)
    lam = (jnp.exp(jnp.sum(lv[0] * lv[1])) - jnp.exp(jnp.sum(lv[2] * lv[3])) + lambda_init).reshape(1)
    y_diff = _diff_attention(attn, lam, p["diff_subln_g"], seq_len, n_lead, lambda_init, attn_tb)

    y_fox = _fox_attention(attn, key_bias, seq_len, attn_tb)

    conv_w = jnp.zeros((SUBLANES, GDN_CONV_CH), F32).at[:CONV_K].set(p["gdn_conv_w"])
    y_gdn = _gated_deltanet(gdn_slab, gates, conv_w, _lanes(p["gdn_a_log"], GATE_DECAY),
                            _lanes(p["gdn_dt_bias"], GATE_DECAY), p["gdn_norm_g"], seq_len, attn_tb)

    rw = jnp.zeros((d, LANES), F32).at[:, :N_EXPERTS].set(p["router_w"])
    rb = _lanes(p["router_b"], 0)
    tm = TOKEN_BLOCK
    x1, lr, lrt, ge, counts = _mix_router(stream, y_diff, y_fox, y_gdn, p["w_out"].astype(BF16),
                                          p["ln1_g"], p["ln1_b"], rw, rb, alpha, tm, seq_len, n_lead)

    mb = MOE_ROWS
    nb = t_all // tm
    n_slots = -(-(t_all * TOP_K + (GROUP_ALIGN - 1) * nb * N_EXPERTS
                  + N_EXPERTS * (SLAB + mb - GROUP_ALIGN)) // mb) * mb
    n_blocks = n_slots // mb
    cnt = counts[:, 0, :N_EXPERTS].astype(I32)
    rows = (cnt + GROUP_ALIGN - 1) // GROUP_ALIGN * GROUP_ALIGN
    total = jnp.sum(rows, axis=0)
    region = (total + SLAB + mb - 1) // mb * mb
    pad_end = jnp.cumsum(region)
    pad_start = pad_end - region
    gstart = (pad_start[None, :] + jnp.cumsum(rows, axis=0) - rows).reshape(-1)
    npass = jnp.maximum((jnp.max(cnt, axis=1) + SLAB - 1) // SLAB, 1)
    block_first = jnp.arange(n_blocks, dtype=I32) * mb
    block_expert = jnp.minimum(jnp.sum((pad_end[None, :] <= block_first[:, None]).astype(I32), axis=1),
                               N_EXPERTS - 1)
    n_active = (pad_end[-1] // mb).astype(I32).reshape(1)
    group_tables = (gstart.astype(I32), cnt.reshape(-1), npass.astype(I32))

    xs = _dispatch(group_tables + ((pad_start + total).astype(I32), (region - total).astype(I32)),
                   x1, lrt, n_slots, tm)
    ys = _experts(block_expert, n_active, xs, p["expert_w1"], p["expert_b1"], p["expert_w2"], p["expert_b2"],
                  layer, mb)
    return _combine(group_tables, x1, lr, ge, p["ln2_g"], p["ln2_b"], ys, alpha, tm)


def kernel(x, meta_tokens, ln_in_g, ln_in_b, w_in, diff_lambda, diff_subln_g, fox_forget_b, gdn_conv_w,
           gdn_a_log, gdn_dt_bias, gdn_norm_g, w_out, ln1_g, ln1_b, router_w, router_b, expert_w1,
           expert_b1, expert_w2, expert_b2, ln2_g, ln2_b):
    bsz, s, d = x.shape
    depth = w_in.shape[0]
    n_lead = (-(s + N_META)) % Q_BLOCK
    if n_lead + N_META != Q_BLOCK:
        raise ValueError("sequence length must be a multiple of the query block")
    seq_len = n_lead + N_META + s
    alpha = (2 * depth) ** 0.25
    stream = _ln_in(x, meta_tokens.astype(x.dtype), ln_in_g, ln_in_b, n_lead,
                    _pick_block(seq_len, (640, 512, 256, 128))).reshape(bsz * seq_len, d)
    per_layer = dict(w_in=w_in, diff_lambda=diff_lambda, diff_subln_g=diff_subln_g, fox_forget_b=fox_forget_b,
                     gdn_conv_w=gdn_conv_w, gdn_a_log=gdn_a_log, gdn_dt_bias=gdn_dt_bias,
                     gdn_norm_g=gdn_norm_g, w_out=w_out, ln1_g=ln1_g, ln1_b=ln1_b, router_w=router_w,
                     router_b=router_b, expert_w1=expert_w1, expert_b1=expert_b1, expert_w2=expert_w2,
                     expert_b2=expert_b2, ln2_g=ln2_g, ln2_b=ln2_b)
    stacked = ("expert_w1", "expert_b1", "expert_w2", "expert_b2")
    for layer in range(depth):
        p = {k: (v if k in stacked else v[layer]) for k, v in per_layer.items()}
        stream = _layer(stream, layer, seq_len, n_lead, alpha, p)
    return stream.reshape(bsz, seq_len, d)[:, n_lead + N_META:]
```

```python
import functools
import math

import numpy as np
import jax
import jax.numpy as jnp
from jax import lax
from jax.experimental import pallas as pl
from jax.experimental.pallas import tpu as pltpu

F32 = jnp.float32
BF16 = jnp.bfloat16
I32 = jnp.int32

CHUNK = 64
CHUNK_SHIFT = 6
Q_BLOCK = 128
N_META = 16
DIFF_HEADS = 4
DIFF_QK_DIM = 64
DIFF_V_DIM = 128
FOX_HEADS = 4
FOX_HEAD_DIM = 64
GDN_HEADS = 4
GDN_HEAD_DIM = 64
CONV_K = 4
N_EXPERTS = 32
TOP_K = 4
SWIGLU_LIMIT = 7.0
SWIGLU_ALPHA = 1.702
MASK_VALUE = -1e30
LOG2_E = 1.4426950408889634
FOX_BIAS_PIECES = 3

DIFF_QK_WIDTH = DIFF_HEADS * 2 * DIFF_QK_DIM
DIFF_V_WIDTH = DIFF_HEADS * DIFF_V_DIM
FOX_WIDTH = FOX_HEADS * FOX_HEAD_DIM
GDN_WIDTH = GDN_HEADS * GDN_HEAD_DIM
GDN_CONV_CH = 3 * GDN_WIDTH
IN_PROJ_SIZES = (DIFF_QK_WIDTH, DIFF_QK_WIDTH, DIFF_V_WIDTH, FOX_WIDTH, FOX_WIDTH, FOX_WIDTH, FOX_HEADS,
                 GDN_CONV_CH, GDN_HEADS, GDN_HEADS, GDN_WIDTH)

LANES = 128
SUBLANES = 8
VMEM_LIMIT = 56 * 1024 * 1024

GATE_FF = 0
GATE_BETA = 4
GATE_DECAY = 8

ATTN_WIDTH = 2 * DIFF_QK_WIDTH + DIFF_V_WIDTH + 3 * FOX_WIDTH
GDN_SLAB = GDN_CONV_CH + GDN_WIDTH


def _cparams(*sem):
    return pltpu.CompilerParams(dimension_semantics=tuple(sem), vmem_limit_bytes=VMEM_LIMIT)


def _dot(a, b):
    return jnp.dot(a.astype(BF16), b.astype(BF16), preferred_element_type=F32)


def _dot_nt(a, b):
    return lax.dot_general(a.astype(BF16), b.astype(BF16), (((1,), (1,)), ((), ())),
                           preferred_element_type=F32)


def _split2(x):
    hi = x.astype(BF16)
    lo = (x - hi.astype(F32)).astype(BF16)
    return hi, lo


def _split3(x):
    hi = x.astype(BF16)
    r = x - hi.astype(F32)
    mid = r.astype(BF16)
    lo = (r - mid.astype(F32)).astype(BF16)
    return hi, mid, lo


def _dot3(a, b):
    ah, al = _split2(a)
    bh, bl = _split2(b)
    return (jnp.dot(ah, bh, preferred_element_type=F32) + jnp.dot(ah, bl, preferred_element_type=F32)
            + jnp.dot(al, bh, preferred_element_type=F32))


def _dot_exact_lhs(a01, b):
    a = a01.astype(BF16)
    hi, mid, lo = _split3(b)
    return (jnp.dot(a, hi, preferred_element_type=F32) + jnp.dot(a, mid, preferred_element_type=F32)
            + jnp.dot(a, lo, preferred_element_type=F32))


_BDIMS = (((2,), (1,)), ((0,), (0,)))
_BDIMS_NT = (((2,), (2,)), ((0,), (0,)))


def _bdot(a, b):
    return lax.dot_general(a.astype(BF16), b.astype(BF16), _BDIMS, preferred_element_type=F32)


def _bdot_nt(a, b):
    return lax.dot_general(a.astype(BF16), b.astype(BF16), _BDIMS_NT, preferred_element_type=F32)


def _bdot3(a, b):
    ah, al = _split2(a)
    bh, bl = _split2(b)
    return (lax.dot_general(ah, bh, _BDIMS, preferred_element_type=F32)
            + lax.dot_general(ah, bl, _BDIMS, preferred_element_type=F32)
            + lax.dot_general(al, bh, _BDIMS, preferred_element_type=F32))


def _bdot_exact_lhs(a01, b):
    a = a01.astype(BF16)
    hi, mid, lo = _split3(b)
    return (lax.dot_general(a, hi, _BDIMS, preferred_element_type=F32)
            + lax.dot_general(a, mid, _BDIMS, preferred_element_type=F32)
            + lax.dot_general(a, lo, _BDIMS, preferred_element_type=F32))


def _layer_norm(x, g, b, eps=1e-5):
    mu = jnp.mean(x, axis=-1, keepdims=True)
    xc = x - mu
    var = jnp.mean(xc * xc, axis=-1, keepdims=True)
    return xc * lax.rsqrt(var + eps) * g + b


def _sigmoid(x):
    return 1.0 / (1.0 + jnp.exp(-x))


def _softplus(x):
    return jnp.maximum(x, 0.0) + jnp.log1p(jnp.exp(-jnp.abs(x)))


def _log_sigmoid(x):
    return -_softplus(-x)


def _ln_in_kernel(*refs, n_lead, n_sub):
    x_refs = refs[:n_sub]
    meta_ref, g_ref, b_ref, o_ref = refs[n_sub:]
    i = pl.program_id(1)
    g = g_ref[...]
    b = b_ref[...]
    blk = n_lead + N_META

    @pl.when(i == 0)
    def _():
        o_ref[0, :n_lead, :] = jnp.broadcast_to(b, (n_lead, b.shape[-1]))
        o_ref[0, n_lead:blk, :] = _layer_norm(meta_ref[...], g, b)

    @pl.when(i > 0)
    def _():
        o_ref[0, :blk, :] = _layer_norm(x_refs[0][0], g, b)

    for j in range(1, n_sub):
        o_ref[0, j * blk:(j + 1) * blk, :] = _layer_norm(x_refs[j][0], g, b)


def _ln_in(x, meta, g, b, n_lead, rows):
    bsz, s, d = x.shape
    l = n_lead + N_META + s
    blk = n_lead + N_META
    n_sub = rows // blk
    x_specs = [pl.BlockSpec((1, blk, d), functools.partial(
        lambda bb, i, j: (bb, jnp.maximum(n_sub * i + j - 1, 0), 0), j=j)) for j in range(n_sub)]
    return pl.pallas_call(
        functools.partial(_ln_in_kernel, n_lead=n_lead, n_sub=n_sub),
        out_shape=jax.ShapeDtypeStruct((bsz, l, d), F32),
        grid=(bsz, l // rows),
        in_specs=x_specs + [pl.BlockSpec((N_META, d), lambda bb, i: (0, 0)),
                            pl.BlockSpec((1, d), lambda bb, i: (0, 0)),
                            pl.BlockSpec((1, d), lambda bb, i: (0, 0))],
        out_specs=pl.BlockSpec((1, rows, d), lambda bb, i: (bb, i, 0)),
        compiler_params=_cparams("arbitrary", "arbitrary"),
        name="ln_in",
    )(*([x] * n_sub), meta, g.reshape(1, d), b.reshape(1, d))


PROJ_COL_CHUNK = 768


def _proj_kernel(x_ref, wa_ref, wg_ref, ws_ref, fb_ref, oa_ref, og_ref, os_ref, kb_ref, carry_ref, *, tm, n_lead):
    i = pl.program_id(1)
    pos = i * tm + lax.broadcasted_iota(I32, (tm, 1), 0)
    x = jnp.where(pos >= n_lead, x_ref[...], 0.0)
    xb = x.astype(BF16)
    for c in range(0, ATTN_WIDTH, PROJ_COL_CHUNK):
        oa_ref[:, c:c + PROJ_COL_CHUNK] = jnp.dot(
            xb, wa_ref[:, c:c + PROJ_COL_CHUNK], preferred_element_type=F32).astype(BF16)
    og_ref[...] = jnp.dot(xb, wg_ref[...], preferred_element_type=F32)
    gates = _dot3(x, ws_ref[...])
    os_ref[...] = gates

    @pl.when(i == 0)
    def _():
        carry_ref[...] = jnp.zeros_like(carry_ref)

    log_f = _log_sigmoid(gates + fb_ref[...])
    r = lax.broadcasted_iota(I32, (LANES, LANES), 0)
    c = lax.broadcasted_iota(I32, (LANES, LANES), 1)
    tri = jnp.where(c <= r, 1.0, 0.0).astype(BF16)
    carry = carry_ref[...]
    cums = []
    for r0 in range(0, tm, LANES):
        cums.append(_dot_exact_lhs(tri, log_f[r0:r0 + LANES, :]) + carry)
        carry = cums[-1][LANES - 1:LANES, :]
    carry_ref[...] = carry
    cum = jnp.concatenate(cums, axis=0)
    lane = lax.broadcasted_iota(I32, (1, LANES), 1)
    out = jnp.zeros((tm, LANES), F32)
    for hh in range(FOX_HEADS):
        bias = jnp.where(pos >= n_lead, -LOG2_E * cum[:, GATE_FF + hh:GATE_FF + hh + 1], MASK_VALUE)
        for n, piece in enumerate(_split3(bias)):
            out = jnp.where(lane == _fox_bias_lane(hh) + n, piece.astype(F32), out)
    kb_ref[...] = out.astype(BF16)


def _in_proj(stream, wa, wg, ws, forget_bias_lanes, seq_len, n_lead, tm):
    t_all, d = stream.shape
    nb = seq_len // tm
    row = lambda bb, i: (bb * nb + i, 0)
    const = lambda bb, i: (0, 0)
    return pl.pallas_call(
        functools.partial(_proj_kernel, tm=tm, n_lead=n_lead),
        out_shape=(jax.ShapeDtypeStruct((t_all, ATTN_WIDTH), BF16),
                   jax.ShapeDtypeStruct((t_all, GDN_SLAB), F32),
                   jax.ShapeDtypeStruct((t_all, LANES), F32),
                   jax.ShapeDtypeStruct((t_all, LANES), BF16)),
        grid=(t_all // seq_len, nb),
        in_specs=[pl.BlockSpec((tm, d), row),
                  pl.BlockSpec((d, ATTN_WIDTH), const),
                  pl.BlockSpec((d, GDN_SLAB), const),
                  pl.BlockSpec((d, LANES), const),
                  pl.BlockSpec((1, LANES), const)],
        out_specs=(pl.BlockSpec((tm, ATTN_WIDTH), row),
                   pl.BlockSpec((tm, GDN_SLAB), row),
                   pl.BlockSpec((tm, LANES), row),
                   pl.BlockSpec((tm, LANES), row)),
        scratch_shapes=[pltpu.VMEM((1, LANES), F32)],
        compiler_params=_cparams("arbitrary", "arbitrary"),
        name="in_proj",
    )(stream, wa, wg, ws, forget_bias_lanes)


def _online_softmax_step(s, vt, m_ref, l_ref, a_ref):
    m_old = m_ref[...]
    m_new = jnp.maximum(m_old, jnp.max(s, axis=0, keepdims=True))
    alpha = jnp.exp2(m_old - m_new)
    p = jnp.exp2(s - m_new)
    l_ref[...] = alpha * l_ref[...] + jnp.sum(p, axis=0, keepdims=True)
    a_ref[...] = alpha * a_ref[...] + jnp.dot(vt, p.astype(BF16), preferred_element_type=F32)
    m_ref[...] = m_new


def _pipelined_blocks(i, buf_a, buf_b, produce_first, produce, consume):
    produce_first(0, buf_a)

    def pair(t, carry):
        j = 2 * t
        produce(j + 1, buf_b)
        consume(j, buf_a)
        produce(j + 2, buf_a)
        consume(j + 1, buf_b)
        return carry

    lax.fori_loop(0, i // 2, pair, 0)

    @pl.when(i % 2 == 0)
    def _():
        consume(i, buf_a, diagonal=True)

    @pl.when(i % 2 == 1)
    def _():
        produce(i, buf_b)
        consume(i - 1, buf_a)
        consume(i, buf_b, diagonal=True)


def _transpose_to(dst_ref, src_ref, n_rows):
    for c in range(src_ref.shape[1] // LANES):
        def body(t, carry):
            r0 = pl.multiple_of(t * LANES, LANES)
            tile = src_ref[pl.ds(r0, LANES), c * LANES:(c + 1) * LANES].astype(F32)
            dst_ref[c * LANES:(c + 1) * LANES, pl.ds(r0, LANES)] = tile.T.astype(dst_ref.dtype)
            return carry
        lax.fori_loop(0, n_rows // LANES, body, 0)


def _init_softmax_state(m_ref, l_ref, a_ref):
    m_ref[...] = jnp.full(m_ref.shape, MASK_VALUE, F32)
    l_ref[...] = jnp.zeros(l_ref.shape, F32)
    a_ref[...] = jnp.zeros(a_ref.shape, F32)


def _diff_attn_kernel(lam_ref, q_ref, k_ref, v_ref, g_ref, o_ref, vt_ref, sa_ref, sb_ref,
                      m0, l0, a0, m1, l1, a1, *, tb, n_lead, seq_len, out_scale):
    i = pl.program_id(2)

    @pl.when(i == 0)
    def _():
        _transpose_to(vt_ref, v_ref, seq_len)

    qt = q_ref[...].astype(F32).T
    row = lax.broadcasted_iota(I32, (LANES, 1), 0)
    qt0 = jnp.where(row < DIFF_QK_DIM, qt, 0.0).astype(BF16)
    qt1 = jnp.where(row >= DIFF_QK_DIM, qt, 0.0).astype(BF16)
    _init_softmax_state(m0, l0, a0)
    _init_softmax_state(m1, l1, a1)

    def produce(j, buf, pad_bias=False):
        start = pl.multiple_of(j * tb, tb)
        k = k_ref[pl.ds(start, tb), :]
        for half, qth in enumerate((qt0, qt1)):
            s = jnp.dot(k, qth, preferred_element_type=F32)
            if pad_bias:
                kpos = lax.broadcasted_iota(I32, (tb, 1), 0)
                s = s + jnp.where(kpos >= n_lead, 0.0, MASK_VALUE)
            buf[half] = s

    def consume(j, buf, diagonal=False):
        start = pl.multiple_of(j * tb, tb)
        vt = vt_ref[:, pl.ds(start, tb)]
        for half, state in enumerate(((m0, l0, a0), (m1, l1, a1))):
            s = buf[half]
            if diagonal:
                kpos = j * tb + lax.broadcasted_iota(I32, (tb, 1), 0)
                qpos = i * tb + lax.broadcasted_iota(I32, (1, tb), 1)
                s = jnp.where(kpos >> CHUNK_SHIFT <= qpos >> CHUNK_SHIFT, s, MASK_VALUE)
            _online_softmax_step(s, vt, *state)

    _pipelined_blocks(i, sa_ref, sb_ref, functools.partial(produce, pad_bias=True), produce, consume)

    lam = lam_ref[0]
    o = a0[...] / l0[...] - lam * (a1[...] / l1[...])
    o = o * lax.rsqrt(jnp.mean(o * o, axis=0, keepdims=True) + 1e-5) * g_ref[...] * out_scale
    o_ref[...] = o.T.astype(o_ref.dtype)


def _diff_attention(attn, lam, subln_g, seq_len, n_lead, lambda_init, tb):
    t_all = attn.shape[0]
    bsz = t_all // seq_len
    nq = seq_len // tb
    kcol = DIFF_QK_WIDTH // LANES
    vcol = 2 * DIFF_QK_WIDTH // LANES
    scratch = [pltpu.VMEM((DIFF_V_DIM, seq_len), BF16),
               pltpu.VMEM((2, tb, tb), F32), pltpu.VMEM((2, tb, tb), F32)]
    for _ in range(2):
        scratch += [pltpu.VMEM((1, tb), F32), pltpu.VMEM((1, tb), F32), pltpu.VMEM((DIFF_V_DIM, tb), F32)]
    return pl.pallas_call(
        functools.partial(_diff_attn_kernel, tb=tb, n_lead=n_lead, seq_len=seq_len,
                          out_scale=1.0 - lambda_init),
        out_shape=jax.ShapeDtypeStruct((t_all, DIFF_V_WIDTH), BF16),
        grid_spec=pltpu.PrefetchScalarGridSpec(
            num_scalar_prefetch=1,
            grid=(bsz, DIFF_HEADS, nq),
            in_specs=[pl.BlockSpec((tb, LANES), lambda b, h, i, lam: (b * nq + i, h)),
                      pl.BlockSpec((seq_len, LANES), lambda b, h, i, lam: (b, kcol + h)),
                      pl.BlockSpec((seq_len, LANES), lambda b, h, i, lam: (b, vcol + h)),
                      pl.BlockSpec((DIFF_V_DIM, 1), lambda b, h, i, lam: (0, 0))],
            out_specs=pl.BlockSpec((tb, DIFF_V_DIM), lambda b, h, i, lam: (b * nq + i, h)),
            scratch_shapes=scratch),
        compiler_params=_cparams("arbitrary", "arbitrary", "arbitrary"),
        name="diff_attention",
    )(lam, attn, attn, attn, subln_g.reshape(DIFF_V_DIM, 1))


def _fox_bias_lane(hh):
    return (1 - hh % 2) * FOX_HEAD_DIM + FOX_BIAS_PIECES * (hh // 2)


def _fox_attn_kernel(q_ref, k_ref, v_ref, kb_ref, o_ref, vt_ref, sa_ref, sb_ref, *scratch, tb, seq_len):
    i = pl.program_id(1)
    pairs = FOX_HEADS // 2

    @pl.when(i == 0)
    def _():
        _transpose_to(vt_ref, v_ref, seq_len)

    row = lax.broadcasted_iota(I32, (LANES, 1), 0)
    lane = lax.broadcasted_iota(I32, (1, LANES), 1)
    qts = []
    for p in range(pairs):
        qt = q_ref[:, p * LANES:(p + 1) * LANES].astype(F32).T
        for r in range(2):
            own = jnp.logical_and(row >= r * FOX_HEAD_DIM, row < (r + 1) * FOX_HEAD_DIM)
            b0 = _fox_bias_lane(2 * p + r)
            ones = jnp.logical_and(row >= b0, row < b0 + FOX_BIAS_PIECES)
            qts.append(jnp.where(own, qt, jnp.where(ones, 1.0, 0.0)).astype(BF16))
    for hh in range(FOX_HEADS):
        _init_softmax_state(*scratch[3 * hh:3 * hh + 3])

    def produce(j, buf):
        start = pl.multiple_of(j * tb, tb)
        kb = kb_ref[pl.ds(start, tb), :]
        for p in range(pairs):
            k = k_ref[pl.ds(start, tb), p * LANES:(p + 1) * LANES]
            for r in range(2):
                own = jnp.logical_and(lane >= r * FOX_HEAD_DIM, lane < (r + 1) * FOX_HEAD_DIM)
                buf[2 * p + r] = jnp.dot(jnp.where(own, k, kb), qts[2 * p + r],
                                         preferred_element_type=F32)

    def consume(j, buf, diagonal=False):
        start = pl.multiple_of(j * tb, tb)
        for hh in range(FOX_HEADS):
            vt = vt_ref[hh * FOX_HEAD_DIM:(hh + 1) * FOX_HEAD_DIM, pl.ds(start, tb)]
            s = buf[hh]
            if diagonal:
                kpos = j * tb + lax.broadcasted_iota(I32, (tb, 1), 0)
                qpos = i * tb + lax.broadcasted_iota(I32, (1, tb), 1)
                s = jnp.where(kpos <= qpos, s, MASK_VALUE)
            _online_softmax_step(s, vt, *scratch[3 * hh:3 * hh + 3])

    _pipelined_blocks(i, sa_ref, sb_ref, produce, produce, consume)

    for p in range(pairs):
        _, l_lo, a_lo = scratch[6 * p:6 * p + 3]
        _, l_hi, a_hi = scratch[6 * p + 3:6 * p + 6]
        o = jnp.concatenate([a_lo[...] / l_lo[...], a_hi[...] / l_hi[...]], axis=0)
        o_ref[:, p * LANES:(p + 1) * LANES] = o.T.astype(o_ref.dtype)


def _fox_attention(attn, key_bias, seq_len, tb):
    t_all = attn.shape[0]
    bsz = t_all // seq_len
    nq = seq_len // tb
    base = (2 * DIFF_QK_WIDTH + DIFF_V_WIDTH) // FOX_WIDTH
    scratch = [pltpu.VMEM((FOX_WIDTH, seq_len), BF16),
               pltpu.VMEM((FOX_HEADS, tb, tb), F32), pltpu.VMEM((FOX_HEADS, tb, tb), F32)]
    for _ in range(FOX_HEADS):
        scratch += [pltpu.VMEM((1, tb), F32), pltpu.VMEM((1, tb), F32), pltpu.VMEM((FOX_HEAD_DIM, tb), F32)]
    return pl.pallas_call(
        functools.partial(_fox_attn_kernel, tb=tb, seq_len=seq_len),
        out_shape=jax.ShapeDtypeStruct((t_all, FOX_WIDTH), BF16),
        grid=(bsz, nq),
        in_specs=[pl.BlockSpec((tb, FOX_WIDTH), lambda b, i: (b * nq + i, base)),
                  pl.BlockSpec((seq_len, FOX_WIDTH), lambda b, i: (b, base + 1)),
                  pl.BlockSpec((seq_len, FOX_WIDTH), lambda b, i: (b, base + 2)),
                  pl.BlockSpec((seq_len, LANES), lambda b, i: (b, 0))],
        out_specs=pl.BlockSpec((tb, FOX_WIDTH), lambda b, i: (b * nq + i, 0)),
        scratch_shapes=scratch,
        compiler_params=_cparams("arbitrary", "arbitrary"),
        name="fox_attention",
    )(attn, attn, attn, key_bias)


def _gdn_kernel(x_ref, gate_ref, cw_ref, alog_ref, dtb_ref, ng_ref, o_ref,
                halo_ref, xs_ref, state_ref, *, rows):
    cg = pl.program_id(1)
    g_chunks = rows // CHUNK
    hd = GDN_HEAD_DIM

    @pl.when(cg == 0)
    def _():
        halo_ref[...] = jnp.zeros_like(halo_ref)
        state_ref[...] = jnp.zeros_like(state_ref)

    xin = x_ref[:, :GDN_CONV_CH]
    xs_ref[0:SUBLANES, :] = halo_ref[...]
    xs_ref[SUBLANES:SUBLANES + rows, :] = xin
    halo_ref[...] = xin[rows - SUBLANES:rows, :]
    y = jnp.zeros((rows, GDN_CONV_CH), F32)
    for tap in range(CONV_K):
        back = CONV_K - 1 - tap
        y = y + cw_ref[tap:tap + 1, :] * xs_ref[SUBLANES - back:SUBLANES - back + rows, :]
    y = y * _sigmoid(y)

    gates = gate_ref[...]
    beta_all = _sigmoid(gates)
    g_all = -jnp.exp(alog_ref[...]) * _softplus(gates + dtb_ref[...])
    rowi = lax.broadcasted_iota(I32, (rows, 1), 0) & (CHUNK - 1)
    gc_all = g_all
    shift = 1
    while shift < CHUNK:
        gc_all = gc_all + jnp.where(rowi >= shift, pltpu.roll(gc_all, shift, 0), 0.0)
        shift *= 2

    ri = lax.broadcasted_iota(I32, (CHUNK, CHUNK), 0)
    ci = lax.broadcasted_iota(I32, (CHUNK, CHUNK), 1)
    eye = jnp.where(ri == ci, 1.0, 0.0).astype(F32)
    causal = (ci <= ri)[None]
    strict = (ci < ri)[None]
    eye_b = jnp.broadcast_to(eye[None], (g_chunks, CHUNK, CHUNK))

    prep = []
    for h in range(GDN_HEADS):
        def head3(base):
            return y[:, base + h * hd:base + (h + 1) * hd]
        q = head3(0)
        k = head3(GDN_WIDTH)
        v = head3(2 * GDN_WIDTH)
        q = q * lax.rsqrt(jnp.sum(q * q, axis=-1, keepdims=True) + 1e-6) * (hd ** -0.5)
        k = k * lax.rsqrt(jnp.sum(k * k, axis=-1, keepdims=True) + 1e-6)
        beta = beta_all[:, GATE_BETA + h:GATE_BETA + h + 1]
        gcol = jnp.broadcast_to(gc_all[:, GATE_DECAY + h:GATE_DECAY + h + 1], (rows, hd))
        egc = jnp.exp(gcol)

        q3 = q.reshape(g_chunks, CHUNK, hd)
        k3 = k.reshape(g_chunks, CHUNK, hd)
        gcol3 = gcol.reshape(g_chunks, CHUNK, CHUNK)
        beta3 = jnp.broadcast_to(beta, (rows, hd)).reshape(g_chunks, CHUNK, hd)
        grow3 = jnp.swapaxes(gcol3, 1, 2)
        decay = jnp.exp(jnp.where(causal, gcol3 - grow3, MASK_VALUE))
        a = jnp.where(strict, _bdot_nt(k3, k3) * beta3 * decay, 0.0)

        x = -a
        tinv = eye_b + x
        span = 2
        while span < CHUNK:
            x = _bdot(x, x)
            tinv = tinv + _bdot(tinv, x)
            span *= 2

        vb = (v * beta).reshape(g_chunks, CHUNK, hd)
        kb = (k * beta * egc).reshape(g_chunks, CHUNK, hd)
        uw3 = _bdot(tinv, jnp.concatenate([vb, kb], axis=-1))
        qk3 = _bdot_nt(q3, k3) * decay
        qd3 = (q * egc).reshape(g_chunks, CHUNK, hd)
        glast3 = gcol3[:, CHUNK - 1:CHUNK, :]
        kdt3 = jnp.swapaxes(k3 * jnp.exp(glast3 - gcol3), 1, 2)
        cd3 = jnp.exp(glast3)
        k_uw = _bdot(kdt3, uw3)
        q_uw = _bdot(qk3, uw3)
        prep.append((k_uw[..., hd:], k_uw[..., :hd], qd3 - q_uw[..., hd:], q_uw[..., :hd], cd3))

    states = [state_ref[h] for h in range(GDN_HEADS)]
    outs = [[] for _ in range(GDN_HEADS)]
    for c in range(g_chunks):
        for h in range(GDN_HEADS):
            kw3, ku3, qp3, o03, cd3 = prep[h]
            state = states[h]
            outs[h].append(_dot(qp3[c], state) + o03[c])
            states[h] = state * cd3[c] + (ku3[c] - _dot(kw3[c], state))

    for h in range(GDN_HEADS):
        state_ref[h] = states[h]
        o = jnp.concatenate(outs[h], axis=0)
        o = o * lax.rsqrt(jnp.mean(o * o, axis=-1, keepdims=True) + 1e-6) * ng_ref[...]
        z = x_ref[:, GDN_CONV_CH + h * hd:GDN_CONV_CH + (h + 1) * hd]
        o_ref[:, h * hd:(h + 1) * hd] = (o * (z * _sigmoid(z))).astype(o_ref.dtype)


def _gated_deltanet(gdn_slab, gates, conv_w, a_log_lanes, dt_bias_lanes, norm_g, seq_len, rows):
    t_all = gdn_slab.shape[0]
    bsz = t_all // seq_len
    nb = seq_len // rows
    row = lambda b, i: (b * nb + i, 0)
    const = lambda b, i: (0, 0)
    return pl.pallas_call(
        functools.partial(_gdn_kernel, rows=rows),
        out_shape=jax.ShapeDtypeStruct((t_all, GDN_WIDTH), BF16),
        grid=(bsz, nb),
        in_specs=[pl.BlockSpec((rows, GDN_SLAB), row),
                  pl.BlockSpec((rows, LANES), row),
                  pl.BlockSpec((SUBLANES, GDN_CONV_CH), const),
                  pl.BlockSpec((1, LANES), const),
                  pl.BlockSpec((1, LANES), const),
                  pl.BlockSpec((1, GDN_HEAD_DIM), const)],
        out_specs=pl.BlockSpec((rows, GDN_WIDTH), row),
        scratch_shapes=[pltpu.VMEM((SUBLANES, GDN_CONV_CH), F32),
                        pltpu.VMEM((rows + SUBLANES, GDN_CONV_CH), F32),
                        pltpu.VMEM((GDN_HEADS, GDN_HEAD_DIM, GDN_HEAD_DIM), F32)],
        compiler_params=_cparams("arbitrary", "arbitrary"),
        name="gated_deltanet",
    )(gdn_slab, gates, conv_w, a_log_lanes, dt_bias_lanes, norm_g.reshape(1, GDN_HEAD_DIM))


def _mix_router_kernel(x_ref, yd_ref, yf_ref, yg_ref, wd_ref, wf_ref, wg_ref, g_ref, b_ref,
                       rw_ref, rb_ref, x1_ref, lr_ref, lrt_ref, ge_ref, cnt_ref, *,
                       tm, rb_rows, alpha, seq_len, n_lead, n_seq):
    mix = (jnp.dot(yd_ref[...], wd_ref[...], preferred_element_type=F32)
           + jnp.dot(yf_ref[...], wf_ref[...], preferred_element_type=F32)
           + jnp.dot(yg_ref[...], wg_ref[...], preferred_element_type=F32))
    x1 = _layer_norm(alpha * x_ref[...] + mix, g_ref[...], b_ref[...])
    x1_ref[...] = x1

    lane = lax.broadcasted_iota(I32, (tm, LANES), 1)
    lane_f = lane.astype(F32)
    logits = _dot3(x1, rw_ref[...]) + rb_ref[...]
    vals = jnp.where(lane < N_EXPERTS, logits, -jnp.inf)
    hits, tops = [], []
    for _ in range(TOP_K):
        top = jnp.max(vals, axis=-1, keepdims=True)
        idx = jnp.min(jnp.where(vals == top, lane_f, float(LANES)), axis=-1, keepdims=True)
        hit = lane_f == idx
        vals = jnp.where(hit, -jnp.inf, vals)
        hits.append(hit)
        tops.append(top)
    exps = [jnp.exp(t - tops[0]) for t in tops]
    denom = exps[0] + exps[1] + exps[2] + exps[3]
    row = pl.program_id(0) * tm + lax.broadcasted_iota(I32, (tm, 1), 0)
    is_pad = row < 0
    for sq in range(n_seq):
        is_pad = jnp.logical_or(is_pad, jnp.logical_and(row >= sq * seq_len, row < sq * seq_len + n_lead))
    onehot = jnp.zeros((tm, LANES), F32)
    gate_by_expert = jnp.zeros((tm, LANES), F32)
    for kk in range(TOP_K):
        hit = jnp.logical_and(hits[kk], jnp.logical_not(is_pad))
        onehot = jnp.where(hit, 1.0, onehot)
        gate_by_expert = jnp.where(hit, exps[kk] / denom, gate_by_expert)
    ge_ref[...] = gate_by_expert

    r = lax.broadcasted_iota(I32, (rb_rows, rb_rows), 0)
    c = lax.broadcasted_iota(I32, (rb_rows, rb_rows), 1)
    tri = jnp.where(c < r, 1.0, 0.0).astype(BF16)
    for blk in range(tm // rb_rows):
        oh = onehot[blk * rb_rows:(blk + 1) * rb_rows, :]
        rank = jnp.dot(tri, oh.astype(BF16), preferred_element_type=F32)
        local = jnp.where(oh > 0.0, rank, -1.0)
        lr_ref[blk * rb_rows:(blk + 1) * rb_rows, :] = local
        lrt_ref[blk * N_EXPERTS:(blk + 1) * N_EXPERTS, :] = local.T[:N_EXPERTS, :]
        cnt_ref[blk] = jnp.sum(oh, axis=0, keepdims=True)


def _mix_router(stream, yd, yf, yg, w_out_b, ln_g, ln_b, rw, rb, alpha, rb_rows, seq_len, n_lead):
    t_all, d = stream.shape
    n_rb = t_all // rb_rows
    per_step = max(k for k in (5, 4, 3, 2, 1) if n_rb % k == 0)
    tm = per_step * rb_rows
    row = lambda i: (i, 0)
    const = lambda i: (0, 0)
    wd = w_out_b[:DIFF_V_WIDTH]
    wf = w_out_b[DIFF_V_WIDTH:DIFF_V_WIDTH + FOX_WIDTH]
    wg = w_out_b[DIFF_V_WIDTH + FOX_WIDTH:]
    return pl.pallas_call(
        functools.partial(_mix_router_kernel, tm=tm, rb_rows=rb_rows, alpha=alpha, seq_len=seq_len,
                          n_lead=n_lead, n_seq=t_all // seq_len),
        out_shape=(jax.ShapeDtypeStruct((t_all, d), F32),
                   jax.ShapeDtypeStruct((t_all, LANES), F32),
                   jax.ShapeDtypeStruct((n_rb * N_EXPERTS, rb_rows), F32),
                   jax.ShapeDtypeStruct((t_all, LANES), F32),
                   jax.ShapeDtypeStruct((n_rb, 1, LANES), F32)),
        grid=(t_all // tm,),
        in_specs=[pl.BlockSpec((tm, d), row),
                  pl.BlockSpec((tm, DIFF_V_WIDTH), row),
                  pl.BlockSpec((tm, FOX_WIDTH), row),
                  pl.BlockSpec((tm, GDN_WIDTH), row),
                  pl.BlockSpec((DIFF_V_WIDTH, d), const),
                  pl.BlockSpec((FOX_WIDTH, d), const),
                  pl.BlockSpec((GDN_WIDTH, d), const),
                  pl.BlockSpec((1, d), const),
                  pl.BlockSpec((1, d), const),
                  pl.BlockSpec((d, LANES), const),
                  pl.BlockSpec((1, LANES), const)],
        out_specs=(pl.BlockSpec((tm, d), row),
                   pl.BlockSpec((tm, LANES), row),
                   pl.BlockSpec((per_step * N_EXPERTS, rb_rows), row),
                   pl.BlockSpec((tm, LANES), row),
                   pl.BlockSpec((per_step, 1, LANES), lambda i: (i, 0, 0))),
        compiler_params=_cparams("arbitrary"),
        name="mix_router",
    )(stream, yd, yf, yg, wd, wf, wg, ln_g.reshape(1, d), ln_b.reshape(1, d), rw, rb)


SLAB = 64
GROUP_ALIGN = 8
ZERO_FILL_SIZES = (512, 256, 128, 64, 32, 16, 8)


def _slab_dma(gstart_ref, gcnt_ref, bb, c, e, action, make_copy):
    idx = bb * N_EXPERTS + e
    start = pl.multiple_of(gstart_ref[idx] + c * SLAB, GROUP_ALIGN)
    small = gcnt_ref[idx] - c * SLAB <= SLAB // 2
    for rows, cond in ((SLAB // 2, small), (SLAB, jnp.logical_not(small))):
        @pl.when(cond)
        def _():
            getattr(make_copy(start, rows), action)()


def _slab_rows(src_ref, rank_row0):
    want = (rank_row0 + lax.broadcasted_iota(I32, (SLAB, 1), 0)).astype(F32)
    return jnp.concatenate([jnp.where(src_ref[e:e + 1, :] == want, 1.0, 0.0).astype(BF16)
                            for e in range(N_EXPERTS)], axis=0)


def _dispatch_kernel(gstart_ref, gcnt_ref, npass_ref, zstart_ref, zrows_ref, x_ref, lrt_ref, xs_ref,
                     zbuf, zero_ref, sem, zsem, *, nb, n_slots):
    b = pl.program_id(0)
    slot = b % 2

    def slab_copy(bb, c, e, s, action):
        _slab_dma(gstart_ref, gcnt_ref, bb, c, e, action, lambda start, rows: pltpu.make_async_copy(
            zbuf.at[s, pl.ds(e * SLAB, rows)], xs_ref.at[pl.ds(start, rows)], sem.at[s]))

    def zero_fill(act):
        for e in range(N_EXPERTS):
            rows = zrows_ref[e]
            for size in ZERO_FILL_SIZES:
                @pl.when((rows & size) != 0)
                def _():
                    start = pl.multiple_of(zstart_ref[e] + (rows & ~(2 * size - 1)), GROUP_ALIGN)
                    act(pltpu.make_async_copy(zero_ref.at[pl.ds(0, size)], xs_ref.at[pl.ds(start, size)], zsem))

    def zero_tail(act):
        first = zstart_ref[N_EXPERTS - 1] + zrows_ref[N_EXPERTS - 1]
        size = ZERO_FILL_SIZES[0]

        def body(t, carry):
            start = pl.multiple_of(first + t * size, GROUP_ALIGN)
            act(pltpu.make_async_copy(zero_ref, xs_ref.at[pl.ds(start, size)], zsem))
            return carry

        lax.fori_loop(0, (n_slots - first) // size, body, 0)

    @pl.when(b == 0)
    def _():
        zero_ref[...] = jnp.zeros_like(zero_ref)
        for fill in (zero_fill, zero_tail):
            fill(lambda cp: cp.start())
        for fill in (zero_fill, zero_tail):
            fill(lambda cp: cp.wait())

    xb = x_ref[...].astype(BF16)
    zbuf[slot] = jnp.dot(_slab_rows(lrt_ref, 0), xb, preferred_element_type=F32)

    @pl.when(b > 0)
    def _():
        @pl.when(npass_ref[b - 1] <= 1)
        def _():
            for e in range(N_EXPERTS):
                slab_copy(b - 1, 0, e, 1 - slot, "wait")

    for e in range(N_EXPERTS):
        slab_copy(b, 0, e, slot, "start")

    @pl.when(npass_ref[b] > 1)
    def _():
        for e in range(N_EXPERTS):
            slab_copy(b, 0, e, slot, "wait")

        def extra_pass(c, carry):
            zbuf[slot] = jnp.dot(_slab_rows(lrt_ref, c * SLAB), xb, preferred_element_type=F32)
            for action in ("start", "wait"):
                for e in range(N_EXPERTS):
                    @pl.when(gcnt_ref[b * N_EXPERTS + e] > c * SLAB)
                    def _():
                        slab_copy(b, c, e, slot, action)
            return carry

        lax.fori_loop(1, npass_ref[b], extra_pass, 0)

    @pl.when(b == nb - 1)
    def _():
        @pl.when(npass_ref[b] <= 1)
        def _():
            for e in range(N_EXPERTS):
                slab_copy(b, 0, e, slot, "wait")


def _dispatch(tables, x1, lrt, n_slots, tm):
    t_all, d = x1.shape
    nb = t_all // tm
    ntab = len(tables)
    return pl.pallas_call(
        functools.partial(_dispatch_kernel, nb=nb, n_slots=n_slots),
        out_shape=jax.ShapeDtypeStruct((n_slots, d), F32),
        grid_spec=pltpu.PrefetchScalarGridSpec(
            num_scalar_prefetch=ntab,
            grid=(nb,),
            in_specs=[pl.BlockSpec((tm, d), lambda i, *_: (i, 0)),
                      pl.BlockSpec((N_EXPERTS, tm), lambda i, *_: (i, 0))],
            out_specs=pl.BlockSpec(memory_space=pl.ANY),
            scratch_shapes=[pltpu.VMEM((2, N_EXPERTS * SLAB, d), F32),
                            pltpu.VMEM((ZERO_FILL_SIZES[0], d), F32),
                            pltpu.SemaphoreType.DMA((2,)),
                            pltpu.SemaphoreType.DMA]),
        compiler_params=_cparams("arbitrary"),
        name="moe_dispatch",
    )(*tables, x1, lrt)


EXPERT_W_CHUNKS = 4


def _expert_kernel(be_ref, na_ref, x_ref, w1_ref, b1_ref, w2_ref, b2_ref, o_ref, w1b_ref, w2b_ref, *, d_expert):
    i = pl.program_id(0)
    active = i < na_ref[0]
    new_expert = jnp.logical_or(i == 0, be_ref[i] != be_ref[jnp.maximum(i - 1, 0)])

    def ffn(w1_chunk, w2_chunk):
        xb = x_ref[...].astype(BF16)
        hid = jnp.concatenate([jnp.dot(xb, w1_chunk(c), preferred_element_type=F32)
                               for c in range(EXPERT_W_CHUNKS)], axis=1) + b1_ref[0]
        glu = jnp.minimum(hid[:, :d_expert], SWIGLU_LIMIT)
        lin = jnp.clip(hid[:, d_expert:], -SWIGLU_LIMIT, SWIGLU_LIMIT)
        act = (glu * _sigmoid(SWIGLU_ALPHA * glu) * (lin + 1.0)).astype(BF16)
        o_ref[...] = jnp.concatenate([jnp.dot(act, w2_chunk(c), preferred_element_type=F32)
                                      for c in range(EXPERT_W_CHUNKS)], axis=1) + b2_ref[0]

    def chunk_of(ref, c, lead=()):
        width = ref.shape[-1] // EXPERT_W_CHUNKS
        return ref[lead + (slice(None), slice(c * width, (c + 1) * width))]

    def cast_chunk(src_ref, dst_ref, c):
        wc = chunk_of(src_ref, c, (0,)).astype(BF16)
        width = dst_ref.shape[-1] // EXPERT_W_CHUNKS
        dst_ref[:, c * width:(c + 1) * width] = wc
        return wc

    @pl.when(jnp.logical_and(active, new_expert))
    def _():
        ffn(functools.partial(cast_chunk, w1_ref, w1b_ref), functools.partial(cast_chunk, w2_ref, w2b_ref))

    @pl.when(jnp.logical_and(active, jnp.logical_not(new_expert)))
    def _():
        ffn(functools.partial(chunk_of, w1b_ref), functools.partial(chunk_of, w2b_ref))

    @pl.when(jnp.logical_not(active))
    def _():
        o_ref[...] = jnp.zeros_like(o_ref)


def _experts(block_expert, n_active, xs, w1_all, b1, w2_all, b2, layer, mb):
    n_slots, d = xs.shape
    depth, n_exp, _, f2 = w1_all.shape
    f = f2 // 2
    nb = n_slots // mb
    w1 = w1_all.reshape(depth * n_exp, d, f2)
    w2 = w2_all.reshape(depth * n_exp, f, d)
    block_expert = block_expert + layer * n_exp
    return pl.pallas_call(
        functools.partial(_expert_kernel, d_expert=f),
        out_shape=jax.ShapeDtypeStruct((n_slots, d), F32),
        grid_spec=pltpu.PrefetchScalarGridSpec(
            num_scalar_prefetch=2,
            grid=(nb,),
            in_specs=[pl.BlockSpec((mb, d), lambda i, be, na: (jnp.minimum(i, na[0] - 1), 0)),
                      pl.BlockSpec((1, d, f2), lambda i, be, na: (be[i], 0, 0)),
                      pl.BlockSpec((1, 1, f2), lambda i, be, na: (be[i], 0, 0)),
                      pl.BlockSpec((1, f, d), lambda i, be, na: (be[i], 0, 0)),
                      pl.BlockSpec((1, 1, d), lambda i, be, na: (be[i], 0, 0))],
            out_specs=pl.BlockSpec((mb, d), lambda i, be, na: (i, 0)),
            scratch_shapes=[pltpu.VMEM((d, f2), BF16), pltpu.VMEM((f, d), BF16)]),
        compiler_params=_cparams("arbitrary"),
        name="moe_experts",
    )(block_expert, n_active, xs, w1, b1.reshape(depth * n_exp, 1, f2), w2, b2.reshape(depth * n_exp, 1, d))


def _combine_kernel(gstart_ref, gcnt_ref, npass_ref, x1_ref, lr_ref, ge_ref, g_ref, b_ref, ys_ref, o_ref,
                    ybuf, acc_ref, sem, *frame_scratch, tm, nb, alpha, frames):
    b = pl.program_id(0)
    slot = b % 2

    def slab_copy(bb, c, e, s, action):
        _slab_dma(gstart_ref, gcnt_ref, bb, c, e, action, lambda start, rows: pltpu.make_async_copy(
            ys_ref.at[pl.ds(start, rows)], ybuf.at[s, pl.ds(e * SLAB, rows)], sem.at[s]))

    @pl.when(b == 0)
    def _():
        ybuf[...] = jnp.zeros_like(ybuf)
        for e in range(N_EXPERTS):
            slab_copy(0, 0, e, 0, "start")

    @pl.when(b + 1 < nb)
    def _():
        for e in range(N_EXPERTS):
            slab_copy(b + 1, 0, e, 1 - slot, "start")

    for e in range(N_EXPERTS):
        slab_copy(b, 0, e, slot, "wait")

    lane = lax.broadcasted_iota(I32, (1, LANES), 1)
    lr = lr_ref[...]
    ge = ge_ref[...]

    def weighted_sum(rank0):
        tiles = []
        for col0 in range(0, N_EXPERTS * SLAB, LANES):
            col = col0 + lane
            rank = jnp.full((tm, LANES), -2.0, F32)
            gate = jnp.zeros((tm, LANES), F32)
            want = jnp.zeros((1, LANES), I32)
            for e in range(col0 // SLAB, min((col0 + LANES - 1) // SLAB, N_EXPERTS - 1) + 1):
                inside = jnp.logical_and(col >= e * SLAB, col < (e + 1) * SLAB)
                rank = jnp.where(inside, lr[:, e:e + 1], rank)
                gate = jnp.where(inside, ge[:, e:e + 1], gate)
                want = jnp.where(inside, col - e * SLAB + rank0, want)
            tiles.append(jnp.where(rank == want.astype(F32), gate, 0.0).astype(BF16))
        sel = jnp.concatenate(tiles, axis=1)
        return jnp.dot(sel, ybuf[slot].astype(BF16), preferred_element_type=F32)

    acc_ref[...] = alpha * x1_ref[...] + weighted_sum(0)

    @pl.when(npass_ref[b] > 1)
    def _():
        def extra_pass(c, carry):
            for action in ("start", "wait"):
                for e in range(N_EXPERTS):
                    @pl.when(gcnt_ref[b * N_EXPERTS + e] > c * SLAB)
                    def _():
                        slab_copy(b, c, e, slot, action)
            acc_ref[...] += weighted_sum(c * SLAB)
            return carry

        lax.fori_loop(1, npass_ref[b], extra_pass, 0)

    result = _layer_norm(acc_ref[...], g_ref[...], b_ref[...])
    if frames is None:
        o_ref[...] = result
        return

    seq_len, lead = frames
    obuf, osem = frame_scratch

    def frame_copies(bb, s, action):
        for piece in range(tm // lead):
            row0 = bb * tm + piece * lead
            sq = jnp.int32(0)
            for j in range(1, nb * tm // seq_len):
                sq = sq + (row0 >= j * seq_len).astype(I32)
            pos = row0 - sq * seq_len

            @pl.when(pos >= lead)
            def _():
                dst = pl.multiple_of(sq * (seq_len - lead) + pos - lead, lead)
                getattr(pltpu.make_async_copy(obuf.at[s, pl.ds(piece * lead, lead)],
                                              o_ref.at[pl.ds(dst, lead)], osem.at[s]), action)()

    @pl.when(b >= 2)
    def _():
        frame_copies(b - 2, slot, "wait")

    obuf[slot] = result
    frame_copies(b, slot, "start")

    @pl.when(b == nb - 1)
    def _():
        frame_copies(b, slot, "wait")
        if nb >= 2:
            frame_copies(b - 1, 1 - slot, "wait")


def _combine(tables, x1, lr, ge, ln_g, ln_b, ys, alpha, tm, frames=None):
    t_all, d = x1.shape
    nb = t_all // tm
    ntab = len(tables)
    row = lambda i, *_: (i, 0)
    const = lambda i, *_: (0, 0)
    scratch = [pltpu.VMEM((2, N_EXPERTS * SLAB, d), F32), pltpu.VMEM((tm, d), F32), pltpu.SemaphoreType.DMA((2,))]
    if frames is None:
        out_rows, out_spec = t_all, pl.BlockSpec((tm, d), row)
    else:
        seq_len, lead = frames
        assert tm % lead == 0 and seq_len % lead == 0
        out_rows, out_spec = t_all // seq_len * (seq_len - lead), pl.BlockSpec(memory_space=pl.ANY)
        scratch += [pltpu.VMEM((2, tm, d), F32), pltpu.SemaphoreType.DMA((2,))]
    return pl.pallas_call(
        functools.partial(_combine_kernel, tm=tm, nb=nb, alpha=alpha, frames=frames),
        out_shape=jax.ShapeDtypeStruct((out_rows, d), F32),
        grid_spec=pltpu.PrefetchScalarGridSpec(
            num_scalar_prefetch=ntab,
            grid=(nb,),
            in_specs=[pl.BlockSpec((tm, d), row),
                      pl.BlockSpec((tm, LANES), row),
                      pl.BlockSpec((tm, LANES), row),
                      pl.BlockSpec((1, d), const),
                      pl.BlockSpec((1, d), const),
                      pl.BlockSpec(memory_space=pl.ANY)],
            out_specs=out_spec,
            scratch_shapes=scratch),
        compiler_params=_cparams("arbitrary"),
        name="moe_combine",
    )(*tables, x1, lr, ge, ln_g.reshape(1, d), ln_b.reshape(1, d), ys)


def _lanes(values, offset):
    return jnp.zeros((1, LANES), F32).at[0, offset:offset + values.shape[0]].set(values.astype(F32))


def _split_in_proj(w_in_layer):
    offs = np.concatenate([[0], np.cumsum(IN_PROJ_SIZES)])
    seg = lambda n: w_in_layer[:, offs[n]:offs[n + 1]]
    qk_scale = DIFF_QK_DIM ** -0.5 * LOG2_E
    fox_scale = FOX_HEAD_DIM ** -0.5 * LOG2_E
    wa = jnp.concatenate([seg(0) * qk_scale, seg(1), seg(2), seg(3) * fox_scale, seg(4), seg(5)], axis=1)
    wg = jnp.concatenate([seg(7), seg(10)], axis=1)
    d = w_in_layer.shape[0]
    ws = jnp.zeros((d, LANES), F32)
    ws = ws.at[:, GATE_FF:GATE_FF + FOX_HEADS].set(seg(6))
    ws = ws.at[:, GATE_BETA:GATE_BETA + GDN_HEADS].set(seg(8))
    ws = ws.at[:, GATE_DECAY:GATE_DECAY + GDN_HEADS].set(seg(9))
    return wa.astype(BF16), wg.astype(BF16), ws


def _diff_lambda_init(layer):
    return 0.8 - 0.6 * math.exp(-0.3 * layer)


def _pick_block(seq_len, candidates):
    for c in candidates:
        if seq_len % c == 0:
            return c
    raise ValueError(f"unsupported sequence length {seq_len}")


MOE_ROWS = 512
TOKEN_BLOCK = 256
assert MOE_ROWS % ZERO_FILL_SIZES[0] == 0 and SLAB + MOE_ROWS - GROUP_ALIGN < 2 * ZERO_FILL_SIZES[0]


def _layer(stream, layer, seq_len, n_lead, alpha, p, frames_only):
    t_all, d = stream.shape
    attn_tb = _pick_block(seq_len, (640, 512, 256, 128))
    wa, wg, ws = _split_in_proj(p["w_in"])
    attn, gdn_slab, gates, key_bias = _in_proj(stream, wa, wg, ws, _lanes(p["fox_forget_b"], GATE_FF),
                                               seq_len, n_lead, attn_tb)

    lv = p["diff_lambda"].astype(F32)
    lambda_init = _diff_lambda_init(layer)
    lam = (jnp.exp(jnp.sum(lv[0] * lv[1])) - jnp.exp(jnp.sum(lv[2] * lv[3])) + lambda_init).reshape(1)
    y_diff = _diff_attention(attn, lam, p["diff_subln_g"], seq_len, n_lead, lambda_init, attn_tb)

    y_fox = _fox_attention(attn, key_bias, seq_len, attn_tb)

    conv_w = jnp.zeros((SUBLANES, GDN_CONV_CH), F32).at[:CONV_K].set(p["gdn_conv_w"])
    y_gdn = _gated_deltanet(gdn_slab, gates, conv_w, _lanes(p["gdn_a_log"], GATE_DECAY),
                            _lanes(p["gdn_dt_bias"], GATE_DECAY), p["gdn_norm_g"], seq_len, attn_tb)

    rw = jnp.zeros((d, LANES), F32).at[:, :N_EXPERTS].set(p["router_w"])
    rb = _lanes(p["router_b"], 0)
    tm = TOKEN_BLOCK
    x1, lr, lrt, ge, counts = _mix_router(stream, y_diff, y_fox, y_gdn, p["w_out"].astype(BF16),
                                          p["ln1_g"], p["ln1_b"], rw, rb, alpha, tm, seq_len, n_lead)

    mb = MOE_ROWS
    nb = t_all // tm
    n_slots = -(-(t_all * TOP_K + (GROUP_ALIGN - 1) * nb * N_EXPERTS
                  + N_EXPERTS * (SLAB + mb - GROUP_ALIGN)) // mb) * mb
    n_blocks = n_slots // mb
    cnt = counts[:, 0, :N_EXPERTS].astype(I32)
    rows = (cnt + GROUP_ALIGN - 1) // GROUP_ALIGN * GROUP_ALIGN
    total = jnp.sum(rows, axis=0)
    region = (total + SLAB + mb - 1) // mb * mb
    pad_end = jnp.cumsum(region)
    pad_start = pad_end - region
    gstart = (pad_start[None, :] + jnp.cumsum(rows, axis=0) - rows).reshape(-1)
    npass = jnp.maximum((jnp.max(cnt, axis=1) + SLAB - 1) // SLAB, 1)
    block_first = jnp.arange(n_blocks, dtype=I32) * mb
    block_expert = jnp.minimum(jnp.sum((pad_end[None, :] <= block_first[:, None]).astype(I32), axis=1),
                               N_EXPERTS - 1)
    n_active = (pad_end[-1] // mb).astype(I32).reshape(1)
    group_tables = (gstart.astype(I32), cnt.reshape(-1), npass.astype(I32))

    xs = _dispatch(group_tables + ((pad_start + total).astype(I32), (region - total).astype(I32)),
                   x1, lrt, n_slots, tm)
    ys = _experts(block_expert, n_active, xs, p["expert_w1"], p["expert_b1"], p["expert_w2"], p["expert_b2"],
                  layer, mb)
    frames = (seq_len, n_lead + N_META) if frames_only else None
    return _combine(group_tables, x1, lr, ge, p["ln2_g"], p["ln2_b"], ys, alpha, tm, frames)


def kernel(x, meta_tokens, ln_in_g, ln_in_b, w_in, diff_lambda, diff_subln_g, fox_forget_b, gdn_conv_w,
           gdn_a_log, gdn_dt_bias, gdn_norm_g, w_out, ln1_g, ln1_b, router_w, router_b, expert_w1,
           expert_b1, expert_w2, expert_b2, ln2_g, ln2_b):
    bsz, s, d = x.shape
    depth = w_in.shape[0]
    n_lead = (-(s + N_META)) % Q_BLOCK
    if n_lead + N_META != Q_BLOCK:
        raise ValueError("sequence length must be a multiple of the query block")
    seq_len = n_lead + N_META + s
    alpha = (2 * depth) ** 0.25
    stream = _ln_in(x, meta_tokens.astype(x.dtype), ln_in_g, ln_in_b, n_lead,
                    _pick_block(seq_len, (640, 512, 256, 128))).reshape(bsz * seq_len, d)
    per_layer = dict(w_in=w_in, diff_lambda=diff_lambda, diff_subln_g=diff_subln_g, fox_forget_b=fox_forget_b,
                     gdn_conv_w=gdn_conv_w, gdn_a_log=gdn_a_log, gdn_dt_bias=gdn_dt_bias,
                     gdn_norm_g=gdn_norm_g, w_out=w_out, ln1_g=ln1_g, ln1_b=ln1_b, router_w=router_w,
                     router_b=router_b, expert_w1=expert_w1, expert_b1=expert_b1, expert_w2=expert_w2,
                     expert_b2=expert_b2, ln2_g=ln2_g, ln2_b=ln2_b)
    stacked = ("expert_w1", "expert_b1", "expert_w2", "expert_b2")
    for layer in range(depth):
        p = {k: (v if k in stacked else v[layer]) for k, v in per_layer.items()}
        stream = _layer(stream, layer, seq_len, n_lead, alpha, p, frames_only=layer == depth - 1)
    return stream.reshape(bsz, s, d)
```

```python
import functools
import math

import numpy as np
import jax
import jax.numpy as jnp
from jax import lax
from jax.experimental import pallas as pl
from jax.experimental.pallas import tpu as pltpu

F32 = jnp.float32
BF16 = jnp.bfloat16
I32 = jnp.int32

CHUNK = 64
CHUNK_SHIFT = 6
Q_BLOCK = 128
N_META = 16
DIFF_HEADS = 4
DIFF_QK_DIM = 64
DIFF_V_DIM = 128
FOX_HEADS = 4
FOX_HEAD_DIM = 64
GDN_HEADS = 4
GDN_HEAD_DIM = 64
CONV_K = 4
N_EXPERTS = 32
TOP_K = 4
SWIGLU_LIMIT = 7.0
SWIGLU_ALPHA = 1.702
MASK_VALUE = -1e30
LOG2_E = 1.4426950408889634
FOX_BIAS_PIECES = 3

DIFF_QK_WIDTH = DIFF_HEADS * 2 * DIFF_QK_DIM
DIFF_V_WIDTH = DIFF_HEADS * DIFF_V_DIM
FOX_WIDTH = FOX_HEADS * FOX_HEAD_DIM
GDN_WIDTH = GDN_HEADS * GDN_HEAD_DIM
GDN_CONV_CH = 3 * GDN_WIDTH
IN_PROJ_SIZES = (DIFF_QK_WIDTH, DIFF_QK_WIDTH, DIFF_V_WIDTH, FOX_WIDTH, FOX_WIDTH, FOX_WIDTH, FOX_HEADS,
                 GDN_CONV_CH, GDN_HEADS, GDN_HEADS, GDN_WIDTH)

LANES = 128
SUBLANES = 8
VMEM_LIMIT = 56 * 1024 * 1024

GATE_FF = 0
GATE_BETA = 4
GATE_DECAY = 8

ATTN_WIDTH = 2 * DIFF_QK_WIDTH + DIFF_V_WIDTH + 3 * FOX_WIDTH
GDN_SLAB = GDN_CONV_CH + GDN_WIDTH


def _cparams(*sem):
    return pltpu.CompilerParams(dimension_semantics=tuple(sem), vmem_limit_bytes=VMEM_LIMIT)


def _dot(a, b):
    return jnp.dot(a.astype(BF16), b.astype(BF16), preferred_element_type=F32)


def _split2(x):
    hi = x.astype(BF16)
    lo = (x - hi.astype(F32)).astype(BF16)
    return hi, lo


def _split3(x):
    hi = x.astype(BF16)
    r = x - hi.astype(F32)
    mid = r.astype(BF16)
    lo = (r - mid.astype(F32)).astype(BF16)
    return hi, mid, lo


def _dot3(a, b):
    ah, al = _split2(a)
    bh, bl = _split2(b)
    return (jnp.dot(ah, bh, preferred_element_type=F32) + jnp.dot(ah, bl, preferred_element_type=F32)
            + jnp.dot(al, bh, preferred_element_type=F32))


def _dot_exact_lhs(a01, b):
    a = a01.astype(BF16)
    hi, mid, lo = _split3(b)
    return (jnp.dot(a, hi, preferred_element_type=F32) + jnp.dot(a, mid, preferred_element_type=F32)
            + jnp.dot(a, lo, preferred_element_type=F32))


_BDIMS = (((2,), (1,)), ((0,), (0,)))
_BDIMS_NT = (((2,), (2,)), ((0,), (0,)))


def _bdot(a, b):
    return lax.dot_general(a.astype(BF16), b.astype(BF16), _BDIMS, preferred_element_type=F32)


def _bdot_nt(a, b):
    return lax.dot_general(a.astype(BF16), b.astype(BF16), _BDIMS_NT, preferred_element_type=F32)


def _layer_norm(x, g, b, eps=1e-5):
    mu = jnp.mean(x, axis=-1, keepdims=True)
    xc = x - mu
    var = jnp.mean(xc * xc, axis=-1, keepdims=True)
    return xc * lax.rsqrt(var + eps) * g + b


def _sigmoid(x):
    return 1.0 / (1.0 + jnp.exp(-x))


def _softplus(x):
    return jnp.maximum(x, 0.0) + jnp.log1p(jnp.exp(-jnp.abs(x)))


def _log_sigmoid(x):
    return -_softplus(-x)


def _ln_in_kernel(*refs, n_lead, n_sub):
    x_refs = refs[:n_sub]
    meta_ref, g_ref, b_ref, o_ref = refs[n_sub:]
    i = pl.program_id(1)
    g = g_ref[...]
    b = b_ref[...]
    blk = n_lead + N_META

    @pl.when(i == 0)
    def _():
        o_ref[0, :n_lead, :] = jnp.broadcast_to(b, (n_lead, b.shape[-1]))
        o_ref[0, n_lead:blk, :] = _layer_norm(meta_ref[...], g, b)

    @pl.when(i > 0)
    def _():
        o_ref[0, :blk, :] = _layer_norm(x_refs[0][0], g, b)

    for j in range(1, n_sub):
        o_ref[0, j * blk:(j + 1) * blk, :] = _layer_norm(x_refs[j][0], g, b)


def _ln_in(x, meta, g, b, n_lead, rows):
    bsz, s, d = x.shape
    l = n_lead + N_META + s
    blk = n_lead + N_META
    n_sub = rows // blk
    x_specs = [pl.BlockSpec((1, blk, d), functools.partial(
        lambda bb, i, j: (bb, jnp.maximum(n_sub * i + j - 1, 0), 0), j=j)) for j in range(n_sub)]
    return pl.pallas_call(
        functools.partial(_ln_in_kernel, n_lead=n_lead, n_sub=n_sub),
        out_shape=jax.ShapeDtypeStruct((bsz, l, d), F32),
        grid=(bsz, l // rows),
        in_specs=x_specs + [pl.BlockSpec((N_META, d), lambda bb, i: (0, 0)),
                            pl.BlockSpec((1, d), lambda bb, i: (0, 0)),
                            pl.BlockSpec((1, d), lambda bb, i: (0, 0))],
        out_specs=pl.BlockSpec((1, rows, d), lambda bb, i: (bb, i, 0)),
        compiler_params=_cparams("arbitrary", "arbitrary"),
        name="ln_in",
    )(*([x] * n_sub), meta, g.reshape(1, d), b.reshape(1, d))


PROJ_COL_CHUNK = 768


def _proj_kernel(x_ref, wa_ref, wg_ref, ws_ref, fb_ref, oa_ref, og_ref, os_ref, kb_ref, carry_ref, *, tm, n_lead):
    i = pl.program_id(1)
    pos = i * tm + lax.broadcasted_iota(I32, (tm, 1), 0)
    x = jnp.where(pos >= n_lead, x_ref[...], 0.0)
    xb = x.astype(BF16)
    for c in range(0, ATTN_WIDTH, PROJ_COL_CHUNK):
        oa_ref[:, c:c + PROJ_COL_CHUNK] = jnp.dot(
            xb, wa_ref[:, c:c + PROJ_COL_CHUNK], preferred_element_type=F32).astype(BF16)
    og_ref[...] = jnp.dot(xb, wg_ref[...], preferred_element_type=F32)
    gates = _dot3(x, ws_ref[...])
    os_ref[...] = gates

    @pl.when(i == 0)
    def _():
        carry_ref[...] = jnp.zeros_like(carry_ref)

    log_f = _log_sigmoid(gates + fb_ref[...])
    r = lax.broadcasted_iota(I32, (LANES, LANES), 0)
    c = lax.broadcasted_iota(I32, (LANES, LANES), 1)
    tri = jnp.where(c <= r, 1.0, 0.0).astype(BF16)
    carry = carry_ref[...]
    cums = []
    for r0 in range(0, tm, LANES):
        cums.append(_dot_exact_lhs(tri, log_f[r0:r0 + LANES, :]) + carry)
        carry = cums[-1][LANES - 1:LANES, :]
    carry_ref[...] = carry
    cum = jnp.concatenate(cums, axis=0)
    lane = lax.broadcasted_iota(I32, (1, LANES), 1)
    out = jnp.zeros((tm, LANES), F32)
    for hh in range(FOX_HEADS):
        bias = jnp.where(pos >= n_lead, -LOG2_E * cum[:, GATE_FF + hh:GATE_FF + hh + 1], MASK_VALUE)
        for n, piece in enumerate(_split3(bias)):
            out = jnp.where(lane == _fox_bias_lane(hh) + n, piece.astype(F32), out)
    kb_ref[...] = out.astype(BF16)


def _in_proj(stream, wa, wg, ws, forget_bias_lanes, seq_len, n_lead, tm):
    t_all, d = stream.shape
    nb = seq_len // tm
    row = lambda bb, i: (bb * nb + i, 0)
    const = lambda bb, i: (0, 0)
    return pl.pallas_call(
        functools.partial(_proj_kernel, tm=tm, n_lead=n_lead),
        out_shape=(jax.ShapeDtypeStruct((t_all, ATTN_WIDTH), BF16),
                   jax.ShapeDtypeStruct((t_all, GDN_SLAB), F32),
                   jax.ShapeDtypeStruct((t_all, LANES), F32),
                   jax.ShapeDtypeStruct((t_all, LANES), BF16)),
        grid=(t_all // seq_len, nb),
        in_specs=[pl.BlockSpec((tm, d), row),
                  pl.BlockSpec((d, ATTN_WIDTH), const),
                  pl.BlockSpec((d, GDN_SLAB), const),
                  pl.BlockSpec((d, LANES), const),
                  pl.BlockSpec((1, LANES), const)],
        out_specs=(pl.BlockSpec((tm, ATTN_WIDTH), row),
                   pl.BlockSpec((tm, GDN_SLAB), row),
                   pl.BlockSpec((tm, LANES), row),
                   pl.BlockSpec((tm, LANES), row)),
        scratch_shapes=[pltpu.VMEM((1, LANES), F32)],
        compiler_params=_cparams("arbitrary", "arbitrary"),
        name="in_proj",
    )(stream, wa, wg, ws, forget_bias_lanes)


def _online_softmax_step(s, vt, m_ref, l_ref, a_ref):
    m_old = m_ref[...]
    m_new = jnp.maximum(m_old, jnp.max(s, axis=0, keepdims=True))
    alpha = jnp.exp2(m_old - m_new)
    p = jnp.exp2(s - m_new)
    l_ref[...] = alpha * l_ref[...] + jnp.sum(p, axis=0, keepdims=True)
    a_ref[...] = alpha * a_ref[...] + jnp.dot(vt, p.astype(BF16), preferred_element_type=F32)
    m_ref[...] = m_new


def _pipelined_blocks(i, buf_a, buf_b, produce_first, produce, consume):
    produce_first(0, buf_a)

    def pair(t, carry):
        j = 2 * t
        produce(j + 1, buf_b)
        consume(j, buf_a)
        produce(j + 2, buf_a)
        consume(j + 1, buf_b)
        return carry

    lax.fori_loop(0, i // 2, pair, 0)

    @pl.when(i % 2 == 0)
    def _():
        consume(i, buf_a, diagonal=True)

    @pl.when(i % 2 == 1)
    def _():
        produce(i, buf_b)
        consume(i - 1, buf_a)
        consume(i, buf_b, diagonal=True)


def _transpose_to(dst_ref, src_ref, n_rows):
    for c in range(src_ref.shape[1] // LANES):
        def body(t, carry):
            r0 = pl.multiple_of(t * LANES, LANES)
            tile = src_ref[pl.ds(r0, LANES), c * LANES:(c + 1) * LANES].astype(F32)
            dst_ref[c * LANES:(c + 1) * LANES, pl.ds(r0, LANES)] = tile.T.astype(dst_ref.dtype)
            return carry
        lax.fori_loop(0, n_rows // LANES, body, 0)


def _init_softmax_state(m_ref, l_ref, a_ref):
    m_ref[...] = jnp.full(m_ref.shape, MASK_VALUE, F32)
    l_ref[...] = jnp.zeros(l_ref.shape, F32)
    a_ref[...] = jnp.zeros(a_ref.shape, F32)


def _diff_attn_kernel(lam_ref, q_ref, k_ref, v_ref, g_ref, o_ref, vt_ref, sa_ref, sb_ref,
                      m0, l0, a0, m1, l1, a1, *, tb, n_lead, seq_len, out_scale):
    i = pl.program_id(2)

    @pl.when(i == 0)
    def _():
        _transpose_to(vt_ref, v_ref, seq_len)

    qt = q_ref[...].astype(F32).T
    row = lax.broadcasted_iota(I32, (LANES, 1), 0)
    qt0 = jnp.where(row < DIFF_QK_DIM, qt, 0.0).astype(BF16)
    qt1 = jnp.where(row >= DIFF_QK_DIM, qt, 0.0).astype(BF16)
    _init_softmax_state(m0, l0, a0)
    _init_softmax_state(m1, l1, a1)

    def produce(j, buf, pad_bias=False):
        start = pl.multiple_of(j * tb, tb)
        k = k_ref[pl.ds(start, tb), :]
        for half, qth in enumerate((qt0, qt1)):
            s = jnp.dot(k, qth, preferred_element_type=F32)
            if pad_bias:
                kpos = lax.broadcasted_iota(I32, (tb, 1), 0)
                s = s + jnp.where(kpos >= n_lead, 0.0, MASK_VALUE)
            buf[half] = s

    def consume(j, buf, diagonal=False):
        start = pl.multiple_of(j * tb, tb)
        vt = vt_ref[:, pl.ds(start, tb)]
        for half, state in enumerate(((m0, l0, a0), (m1, l1, a1))):
            s = buf[half]
            if diagonal:
                kpos = j * tb + lax.broadcasted_iota(I32, (tb, 1), 0)
                qpos = i * tb + lax.broadcasted_iota(I32, (1, tb), 1)
                s = jnp.where(kpos >> CHUNK_SHIFT <= qpos >> CHUNK_SHIFT, s, MASK_VALUE)
            _online_softmax_step(s, vt, *state)

    _pipelined_blocks(i, sa_ref, sb_ref, functools.partial(produce, pad_bias=True), produce, consume)

    lam = lam_ref[0]
    o = a0[...] / l0[...] - lam * (a1[...] / l1[...])
    o = o * lax.rsqrt(jnp.mean(o * o, axis=0, keepdims=True) + 1e-5) * g_ref[...] * out_scale
    o_ref[...] = o.T.astype(o_ref.dtype)


def _diff_attention(attn, lam, subln_g, seq_len, n_lead, lambda_init, tb):
    t_all = attn.shape[0]
    bsz = t_all // seq_len
    nq = seq_len // tb
    kcol = DIFF_QK_WIDTH // LANES
    vcol = 2 * DIFF_QK_WIDTH // LANES
    scratch = [pltpu.VMEM((DIFF_V_DIM, seq_len), BF16),
               pltpu.VMEM((2, tb, tb), F32), pltpu.VMEM((2, tb, tb), F32)]
    for _ in range(2):
        scratch += [pltpu.VMEM((1, tb), F32), pltpu.VMEM((1, tb), F32), pltpu.VMEM((DIFF_V_DIM, tb), F32)]
    return pl.pallas_call(
        functools.partial(_diff_attn_kernel, tb=tb, n_lead=n_lead, seq_len=seq_len,
                          out_scale=1.0 - lambda_init),
        out_shape=jax.ShapeDtypeStruct((t_all, DIFF_V_WIDTH), BF16),
        grid_spec=pltpu.PrefetchScalarGridSpec(
            num_scalar_prefetch=1,
            grid=(bsz, DIFF_HEADS, nq),
            in_specs=[pl.BlockSpec((tb, LANES), lambda b, h, i, lam: (b * nq + i, h)),
                      pl.BlockSpec((seq_len, LANES), lambda b, h, i, lam: (b, kcol + h)),
                      pl.BlockSpec((seq_len, LANES), lambda b, h, i, lam: (b, vcol + h)),
                      pl.BlockSpec((DIFF_V_DIM, 1), lambda b, h, i, lam: (0, 0))],
            out_specs=pl.BlockSpec((tb, DIFF_V_DIM), lambda b, h, i, lam: (b * nq + i, h)),
            scratch_shapes=scratch),
        compiler_params=_cparams("arbitrary", "arbitrary", "arbitrary"),
        name="diff_attention",
    )(lam, attn, attn, attn, subln_g.reshape(DIFF_V_DIM, 1))


def _fox_bias_lane(hh):
    return (1 - hh % 2) * FOX_HEAD_DIM + FOX_BIAS_PIECES * (hh // 2)


def _fox_attn_kernel(q_ref, k_ref, v_ref, kb_ref, o_ref, vt_ref, sa_ref, sb_ref, *scratch, tb, seq_len):
    i = pl.program_id(1)
    pairs = FOX_HEADS // 2

    @pl.when(i == 0)
    def _():
        _transpose_to(vt_ref, v_ref, seq_len)

    row = lax.broadcasted_iota(I32, (LANES, 1), 0)
    lane = lax.broadcasted_iota(I32, (1, LANES), 1)
    qts = []
    for p in range(pairs):
        qt = q_ref[:, p * LANES:(p + 1) * LANES].astype(F32).T
        for r in range(2):
            own = jnp.logical_and(row >= r * FOX_HEAD_DIM, row < (r + 1) * FOX_HEAD_DIM)
            b0 = _fox_bias_lane(2 * p + r)
            ones = jnp.logical_and(row >= b0, row < b0 + FOX_BIAS_PIECES)
            qts.append(jnp.where(own, qt, jnp.where(ones, 1.0, 0.0)).astype(BF16))
    for hh in range(FOX_HEADS):
        _init_softmax_state(*scratch[3 * hh:3 * hh + 3])

    def produce(j, buf):
        start = pl.multiple_of(j * tb, tb)
        kb = kb_ref[pl.ds(start, tb), :]
        for p in range(pairs):
            k = k_ref[pl.ds(start, tb), p * LANES:(p + 1) * LANES]
            for r in range(2):
                own = jnp.logical_and(lane >= r * FOX_HEAD_DIM, lane < (r + 1) * FOX_HEAD_DIM)
                buf[2 * p + r] = jnp.dot(jnp.where(own, k, kb), qts[2 * p + r],
                                         preferred_element_type=F32)

    def consume(j, buf, diagonal=False):
        start = pl.multiple_of(j * tb, tb)
        for hh in range(FOX_HEADS):
            vt = vt_ref[hh * FOX_HEAD_DIM:(hh + 1) * FOX_HEAD_DIM, pl.ds(start, tb)]
            s = buf[hh]
            if diagonal:
                kpos = j * tb + lax.broadcasted_iota(I32, (tb, 1), 0)
                qpos = i * tb + lax.broadcasted_iota(I32, (1, tb), 1)
                s = jnp.where(kpos <= qpos, s, MASK_VALUE)
            _online_softmax_step(s, vt, *scratch[3 * hh:3 * hh + 3])

    _pipelined_blocks(i, sa_ref, sb_ref, produce, produce, consume)

    for p in range(pairs):
        _, l_lo, a_lo = scratch[6 * p:6 * p + 3]
        _, l_hi, a_hi = scratch[6 * p + 3:6 * p + 6]
        o = jnp.concatenate([a_lo[...] / l_lo[...], a_hi[...] / l_hi[...]], axis=0)
        o_ref[:, p * LANES:(p + 1) * LANES] = o.T.astype(o_ref.dtype)


def _fox_attention(attn, key_bias, seq_len, tb):
    t_all = attn.shape[0]
    bsz = t_all // seq_len
    nq = seq_len // tb
    base = (2 * DIFF_QK_WIDTH + DIFF_V_WIDTH) // FOX_WIDTH
    scratch = [pltpu.VMEM((FOX_WIDTH, seq_len), BF16),
               pltpu.VMEM((FOX_HEADS, tb, tb), F32), pltpu.VMEM((FOX_HEADS, tb, tb), F32)]
    for _ in range(FOX_HEADS):
        scratch += [pltpu.VMEM((1, tb), F32), pltpu.VMEM((1, tb), F32), pltpu.VMEM((FOX_HEAD_DIM, tb), F32)]
    return pl.pallas_call(
        functools.partial(_fox_attn_kernel, tb=tb, seq_len=seq_len),
        out_shape=jax.ShapeDtypeStruct((t_all, FOX_WIDTH), BF16),
        grid=(bsz, nq),
        in_specs=[pl.BlockSpec((tb, FOX_WIDTH), lambda b, i: (b * nq + i, base)),
                  pl.BlockSpec((seq_len, FOX_WIDTH), lambda b, i: (b, base + 1)),
                  pl.BlockSpec((seq_len, FOX_WIDTH), lambda b, i: (b, base + 2)),
                  pl.BlockSpec((seq_len, LANES), lambda b, i: (b, 0))],
        out_specs=pl.BlockSpec((tb, FOX_WIDTH), lambda b, i: (b * nq + i, 0)),
        scratch_shapes=scratch,
        compiler_params=_cparams("arbitrary", "arbitrary"),
        name="fox_attention",
    )(attn, attn, attn, key_bias)


def _gdn_kernel(x_ref, gate_ref, cw_ref, alog_ref, dtb_ref, ng_ref, o_ref,
                halo_ref, xs_ref, state_ref, *, rows):
    cg = pl.program_id(1)
    g_chunks = rows // CHUNK
    hd = GDN_HEAD_DIM

    @pl.when(cg == 0)
    def _():
        halo_ref[...] = jnp.zeros_like(halo_ref)
        state_ref[...] = jnp.zeros_like(state_ref)

    xin = x_ref[:, :GDN_CONV_CH]
    xs_ref[0:SUBLANES, :] = halo_ref[...]
    xs_ref[SUBLANES:SUBLANES + rows, :] = xin
    halo_ref[...] = xin[rows - SUBLANES:rows, :]
    y = jnp.zeros((rows, GDN_CONV_CH), F32)
    for tap in range(CONV_K):
        back = CONV_K - 1 - tap
        y = y + cw_ref[tap:tap + 1, :] * xs_ref[SUBLANES - back:SUBLANES - back + rows, :]
    y = y * _sigmoid(y)

    gates = gate_ref[...]
    beta_all = _sigmoid(gates)
    g_all = -jnp.exp(alog_ref[...]) * _softplus(gates + dtb_ref[...])
    rowi = lax.broadcasted_iota(I32, (rows, 1), 0) & (CHUNK - 1)
    gc_all = g_all
    shift = 1
    while shift < CHUNK:
        gc_all = gc_all + jnp.where(rowi >= shift, pltpu.roll(gc_all, shift, 0), 0.0)
        shift *= 2

    ri = lax.broadcasted_iota(I32, (CHUNK, CHUNK), 0)
    ci = lax.broadcasted_iota(I32, (CHUNK, CHUNK), 1)
    eye = jnp.where(ri == ci, 1.0, 0.0).astype(F32)
    causal = (ci <= ri)[None]
    strict = (ci < ri)[None]
    eye_b = jnp.broadcast_to(eye[None], (g_chunks, CHUNK, CHUNK))

    prep = []
    for h in range(GDN_HEADS):
        def head3(base):
            return y[:, base + h * hd:base + (h + 1) * hd]
        q = head3(0)
        k = head3(GDN_WIDTH)
        v = head3(2 * GDN_WIDTH)
        q = q * lax.rsqrt(jnp.sum(q * q, axis=-1, keepdims=True) + 1e-6) * (hd ** -0.5)
        k = k * lax.rsqrt(jnp.sum(k * k, axis=-1, keepdims=True) + 1e-6)
        beta = beta_all[:, GATE_BETA + h:GATE_BETA + h + 1]
        gcol = jnp.broadcast_to(gc_all[:, GATE_DECAY + h:GATE_DECAY + h + 1], (rows, hd))
        egc = jnp.exp(gcol)

        q3 = q.reshape(g_chunks, CHUNK, hd)
        k3 = k.reshape(g_chunks, CHUNK, hd)
        gcol3 = gcol.reshape(g_chunks, CHUNK, CHUNK)
        beta3 = jnp.broadcast_to(beta, (rows, hd)).reshape(g_chunks, CHUNK, hd)
        grow3 = jnp.swapaxes(gcol3, 1, 2)
        decay = jnp.exp(jnp.where(causal, gcol3 - grow3, MASK_VALUE))
        a = jnp.where(strict, _bdot_nt(k3, k3) * beta3 * decay, 0.0)

        x = -a
        tinv = eye_b + x
        span = 2
        while span < CHUNK:
            x = _bdot(x, x)
            tinv = tinv + _bdot(tinv, x)
            span *= 2

        vb = (v * beta).reshape(g_chunks, CHUNK, hd)
        kb = (k * beta * egc).reshape(g_chunks, CHUNK, hd)
        uw3 = _bdot(tinv, jnp.concatenate([vb, kb], axis=-1))
        qk3 = _bdot_nt(q3, k3) * decay
        qd3 = (q * egc).reshape(g_chunks, CHUNK, hd)
        glast3 = gcol3[:, CHUNK - 1:CHUNK, :]
        kdt3 = jnp.swapaxes(k3 * jnp.exp(glast3 - gcol3), 1, 2)
        cd3 = jnp.exp(glast3)
        k_uw = _bdot(kdt3, uw3)
        q_uw = _bdot(qk3, uw3)
        prep.append((k_uw[..., hd:], k_uw[..., :hd], qd3 - q_uw[..., hd:], q_uw[..., :hd], cd3))

    states = [state_ref[h] for h in range(GDN_HEADS)]
    outs = [[] for _ in range(GDN_HEADS)]
    for c in range(g_chunks):
        for h in range(GDN_HEADS):
            kw3, ku3, qp3, o03, cd3 = prep[h]
            state = states[h]
            outs[h].append(_dot(qp3[c], state) + o03[c])
            states[h] = state * cd3[c] + (ku3[c] - _dot(kw3[c], state))

    for h in range(GDN_HEADS):
        state_ref[h] = states[h]
        o = jnp.concatenate(outs[h], axis=0)
        o = o * lax.rsqrt(jnp.mean(o * o, axis=-1, keepdims=True) + 1e-6) * ng_ref[...]
        z = x_ref[:, GDN_CONV_CH + h * hd:GDN_CONV_CH + (h + 1) * hd]
        o_ref[:, h * hd:(h + 1) * hd] = (o * (z * _sigmoid(z))).astype(o_ref.dtype)


def _gated_deltanet(gdn_slab, gates, conv_w, a_log_lanes, dt_bias_lanes, norm_g, seq_len, rows):
    t_all = gdn_slab.shape[0]
    bsz = t_all // seq_len
    nb = seq_len // rows
    row = lambda b, i: (b * nb + i, 0)
    const = lambda b, i: (0, 0)
    return pl.pallas_call(
        functools.partial(_gdn_kernel, rows=rows),
        out_shape=jax.ShapeDtypeStruct((t_all, GDN_WIDTH), BF16),
        grid=(bsz, nb),
        in_specs=[pl.BlockSpec((rows, GDN_SLAB), row),
                  pl.BlockSpec((rows, LANES), row),
                  pl.BlockSpec((SUBLANES, GDN_CONV_CH), const),
                  pl.BlockSpec((1, LANES), const),
                  pl.BlockSpec((1, LANES), const),
                  pl.BlockSpec((1, GDN_HEAD_DIM), const)],
        out_specs=pl.BlockSpec((rows, GDN_WIDTH), row),
        scratch_shapes=[pltpu.VMEM((SUBLANES, GDN_CONV_CH), F32),
                        pltpu.VMEM((rows + SUBLANES, GDN_CONV_CH), F32),
                        pltpu.VMEM((GDN_HEADS, GDN_HEAD_DIM, GDN_HEAD_DIM), F32)],
        compiler_params=_cparams("arbitrary", "arbitrary"),
        name="gated_deltanet",
    )(gdn_slab, gates, conv_w, a_log_lanes, dt_bias_lanes, norm_g.reshape(1, GDN_HEAD_DIM))


def _mix_router_kernel(x_ref, yd_ref, yf_ref, yg_ref, wd_ref, wf_ref, wg_ref, g_ref, b_ref,
                       rw_ref, rb_ref, x1_ref, lr_ref, lrt_ref, ge_ref, cnt_ref, *,
                       tm, rb_rows, alpha, seq_len, n_lead, n_seq):
    mix = (jnp.dot(yd_ref[...], wd_ref[...], preferred_element_type=F32)
           + jnp.dot(yf_ref[...], wf_ref[...], preferred_element_type=F32)
           + jnp.dot(yg_ref[...], wg_ref[...], preferred_element_type=F32))
    x1 = _layer_norm(alpha * x_ref[...] + mix, g_ref[...], b_ref[...])
    x1_ref[...] = x1

    lane = lax.broadcasted_iota(I32, (tm, LANES), 1)
    lane_f = lane.astype(F32)
    logits = _dot3(x1, rw_ref[...]) + rb_ref[...]
    vals = jnp.where(lane < N_EXPERTS, logits, -jnp.inf)
    hits, tops = [], []
    for _ in range(TOP_K):
        top = jnp.max(vals, axis=-1, keepdims=True)
        idx = jnp.min(jnp.where(vals == top, lane_f, float(LANES)), axis=-1, keepdims=True)
        hit = lane_f == idx
        vals = jnp.where(hit, -jnp.inf, vals)
        hits.append(hit)
        tops.append(top)
    exps = [jnp.exp(t - tops[0]) for t in tops]
    denom = exps[0] + exps[1] + exps[2] + exps[3]
    row = pl.program_id(0) * tm + lax.broadcasted_iota(I32, (tm, 1), 0)
    is_pad = row < 0
    for sq in range(n_seq):
        is_pad = jnp.logical_or(is_pad, jnp.logical_and(row >= sq * seq_len, row < sq * seq_len + n_lead))
    onehot = jnp.zeros((tm, LANES), F32)
    gate_by_expert = jnp.zeros((tm, LANES), F32)
    for kk in range(TOP_K):
        hit = jnp.logical_and(hits[kk], jnp.logical_not(is_pad))
        onehot = jnp.where(hit, 1.0, onehot)
        gate_by_expert = jnp.where(hit, exps[kk] / denom, gate_by_expert)
    ge_ref[...] = gate_by_expert

    r = lax.broadcasted_iota(I32, (rb_rows, rb_rows), 0)
    c = lax.broadcasted_iota(I32, (rb_rows, rb_rows), 1)
    tri = jnp.where(c < r, 1.0, 0.0).astype(BF16)
    for blk in range(tm // rb_rows):
        oh = onehot[blk * rb_rows:(blk + 1) * rb_rows, :]
        rank = jnp.dot(tri, oh.astype(BF16), preferred_element_type=F32)
        local = jnp.where(oh > 0.0, rank, -1.0)
        lr_ref[blk * rb_rows:(blk + 1) * rb_rows, :] = local
        lrt_ref[blk * N_EXPERTS:(blk + 1) * N_EXPERTS, :] = local.T[:N_EXPERTS, :]
        cnt_ref[blk] = jnp.sum(oh, axis=0, keepdims=True)


def _mix_router(stream, yd, yf, yg, w_out_b, ln_g, ln_b, rw, rb, alpha, rb_rows, seq_len, n_lead):
    t_all, d = stream.shape
    n_rb = t_all // rb_rows
    per_step = max(k for k in (5, 4, 3, 2, 1) if n_rb % k == 0)
    tm = per_step * rb_rows
    row = lambda i: (i, 0)
    const = lambda i: (0, 0)
    wd = w_out_b[:DIFF_V_WIDTH]
    wf = w_out_b[DIFF_V_WIDTH:DIFF_V_WIDTH + FOX_WIDTH]
    wg = w_out_b[DIFF_V_WIDTH + FOX_WIDTH:]
    return pl.pallas_call(
        functools.partial(_mix_router_kernel, tm=tm, rb_rows=rb_rows, alpha=alpha, seq_len=seq_len,
                          n_lead=n_lead, n_seq=t_all // seq_len),
        out_shape=(jax.ShapeDtypeStruct((t_all, d), F32),
                   jax.ShapeDtypeStruct((t_all, LANES), F32),
                   jax.ShapeDtypeStruct((n_rb * N_EXPERTS, rb_rows), F32),
                   jax.ShapeDtypeStruct((t_all, LANES), F32),
                   jax.ShapeDtypeStruct((n_rb, 1, LANES), F32)),
        grid=(t_all // tm,),
        in_specs=[pl.BlockSpec((tm, d), row),
                  pl.BlockSpec((tm, DIFF_V_WIDTH), row),
                  pl.BlockSpec((tm, FOX_WIDTH), row),
                  pl.BlockSpec((tm, GDN_WIDTH), row),
                  pl.BlockSpec((DIFF_V_WIDTH, d), const),
                  pl.BlockSpec((FOX_WIDTH, d), const),
                  pl.BlockSpec((GDN_WIDTH, d), const),
                  pl.BlockSpec((1, d), const),
                  pl.BlockSpec((1, d), const),
                  pl.BlockSpec((d, LANES), const),
                  pl.BlockSpec((1, LANES), const)],
        out_specs=(pl.BlockSpec((tm, d), row),
                   pl.BlockSpec((tm, LANES), row),
                   pl.BlockSpec((per_step * N_EXPERTS, rb_rows), row),
                   pl.BlockSpec((tm, LANES), row),
                   pl.BlockSpec((per_step, 1, LANES), lambda i: (i, 0, 0))),
        compiler_params=_cparams("arbitrary"),
        name="mix_router",
    )(stream, yd, yf, yg, wd, wf, wg, ln_g.reshape(1, d), ln_b.reshape(1, d), rw, rb)


SLAB = 64
GROUP_ALIGN = 8
ZERO_FILL_SIZES = (512, 256, 128, 64, 32, 16, 8)


HIGH_HALF = -65536


def _pack_bf16_pairs(x):
    w = x.shape[1] // 2
    bits = lax.bitcast_convert_type(x.astype(BF16).astype(F32), I32)
    return (bits[:, w:] & HIGH_HALF) | ((bits[:, :w] >> 16) & 0xFFFF)


def _unpack_bf16_pairs(p):
    low = lax.bitcast_convert_type(p << 16, F32)
    high = lax.bitcast_convert_type(p & HIGH_HALF, F32)
    return jnp.concatenate([low, high], axis=1).astype(BF16)


def _slab_dma(gstart_ref, gcnt_ref, bb, c, e, action, make_copy, halves=True):
    idx = bb * N_EXPERTS + e
    start = pl.multiple_of(gstart_ref[idx] + c * SLAB, GROUP_ALIGN)
    if not halves:
        getattr(make_copy(start, SLAB), action)()
        return
    small = gcnt_ref[idx] - c * SLAB <= SLAB // 2
    for rows, cond in ((SLAB // 2, small), (SLAB, jnp.logical_not(small))):
        @pl.when(cond)
        def _():
            getattr(make_copy(start, rows), action)()


def _slab_rows(src_ref, rank_row0):
    want = (rank_row0 + lax.broadcasted_iota(I32, (SLAB, 1), 0)).astype(F32)
    return jnp.concatenate([jnp.where(src_ref[e:e + 1, :] == want, 1.0, 0.0).astype(BF16)
                            for e in range(N_EXPERTS)], axis=0)


def _dispatch_kernel(gstart_ref, gcnt_ref, npass_ref, zstart_ref, zrows_ref, x_ref, lrt_ref, xs_ref,
                     zbuf, zero_ref, sem, zsem, *, nb, n_slots):
    b = pl.program_id(0)
    slot = b % 2

    def slab_copy(bb, c, e, s, action):
        _slab_dma(gstart_ref, gcnt_ref, bb, c, e, action, lambda start, rows: pltpu.make_async_copy(
            zbuf.at[s, pl.ds(e * SLAB, rows)], xs_ref.at[pl.ds(start, rows)], sem.at[s]))

    def zero_fill(act):
        for e in range(N_EXPERTS):
            rows = zrows_ref[e]
            for size in ZERO_FILL_SIZES:
                @pl.when((rows & size) != 0)
                def _():
                    start = pl.multiple_of(zstart_ref[e] + (rows & ~(2 * size - 1)), GROUP_ALIGN)
                    act(pltpu.make_async_copy(zero_ref.at[pl.ds(0, size)], xs_ref.at[pl.ds(start, size)], zsem))

    def zero_tail(act):
        first = zstart_ref[N_EXPERTS - 1] + zrows_ref[N_EXPERTS - 1]
        size = ZERO_FILL_SIZES[0]

        def body(t, carry):
            start = pl.multiple_of(first + t * size, GROUP_ALIGN)
            act(pltpu.make_async_copy(zero_ref, xs_ref.at[pl.ds(start, size)], zsem))
            return carry

        lax.fori_loop(0, (n_slots - first) // size, body, 0)

    @pl.when(b == 0)
    def _():
        zero_ref[...] = jnp.zeros_like(zero_ref)
        for fill in (zero_fill, zero_tail):
            fill(lambda cp: cp.start())
        for fill in (zero_fill, zero_tail):
            fill(lambda cp: cp.wait())

    xb = x_ref[...].astype(BF16)
    zbuf[slot] = _pack_bf16_pairs(jnp.dot(_slab_rows(lrt_ref, 0), xb, preferred_element_type=F32))

    @pl.when(b > 0)
    def _():
        @pl.when(npass_ref[b - 1] <= 1)
        def _():
            for e in range(N_EXPERTS):
                slab_copy(b - 1, 0, e, 1 - slot, "wait")

    for e in range(N_EXPERTS):
        slab_copy(b, 0, e, slot, "start")

    @pl.when(npass_ref[b] > 1)
    def _():
        for e in range(N_EXPERTS):
            slab_copy(b, 0, e, slot, "wait")

        def extra_pass(c, carry):
            zbuf[slot] = _pack_bf16_pairs(
                jnp.dot(_slab_rows(lrt_ref, c * SLAB), xb, preferred_element_type=F32))
            for action in ("start", "wait"):
                for e in range(N_EXPERTS):
                    @pl.when(gcnt_ref[b * N_EXPERTS + e] > c * SLAB)
                    def _():
                        slab_copy(b, c, e, slot, action)
            return carry

        lax.fori_loop(1, npass_ref[b], extra_pass, 0)

    @pl.when(b == nb - 1)
    def _():
        @pl.when(npass_ref[b] <= 1)
        def _():
            for e in range(N_EXPERTS):
                slab_copy(b, 0, e, slot, "wait")


def _dispatch(tables, x1, lrt, n_slots, tm):
    t_all, d = x1.shape
    nb = t_all // tm
    ntab = len(tables)
    return pl.pallas_call(
        functools.partial(_dispatch_kernel, nb=nb, n_slots=n_slots),
        out_shape=jax.ShapeDtypeStruct((n_slots, d // 2), I32),
        grid_spec=pltpu.PrefetchScalarGridSpec(
            num_scalar_prefetch=ntab,
            grid=(nb,),
            in_specs=[pl.BlockSpec((tm, d), lambda i, *_: (i, 0)),
                      pl.BlockSpec((N_EXPERTS, tm), lambda i, *_: (i, 0))],
            out_specs=pl.BlockSpec(memory_space=pl.ANY),
            scratch_shapes=[pltpu.VMEM((2, N_EXPERTS * SLAB, d // 2), I32),
                            pltpu.VMEM((ZERO_FILL_SIZES[0], d // 2), I32),
                            pltpu.SemaphoreType.DMA((2,)),
                            pltpu.SemaphoreType.DMA]),
        compiler_params=_cparams("arbitrary"),
        name="moe_dispatch",
    )(*tables, x1, lrt)


EXPERT_W_CHUNKS = 4


def _expert_kernel(be_ref, na_ref, x_ref, w1_ref, b1_ref, w2_ref, b2_ref, o_ref, w1b_ref, w2b_ref, *, d_expert):
    i = pl.program_id(0)
    active = i < na_ref[0]
    new_expert = jnp.logical_or(i == 0, be_ref[i] != be_ref[jnp.maximum(i - 1, 0)])

    def ffn(w1_chunk, w2_chunk):
        xb = _unpack_bf16_pairs(x_ref[...])
        hid = jnp.concatenate([jnp.dot(xb, w1_chunk(c), preferred_element_type=F32)
                               for c in range(EXPERT_W_CHUNKS)], axis=1) + b1_ref[0]
        glu = jnp.minimum(hid[:, :d_expert], SWIGLU_LIMIT)
        lin = jnp.clip(hid[:, d_expert:], -SWIGLU_LIMIT, SWIGLU_LIMIT)
        act = (glu * _sigmoid(SWIGLU_ALPHA * glu) * (lin + 1.0)).astype(BF16)
        o_ref[...] = _pack_bf16_pairs(jnp.concatenate([jnp.dot(act, w2_chunk(c), preferred_element_type=F32)
                                                       for c in range(EXPERT_W_CHUNKS)], axis=1) + b2_ref[0])

    def chunk_of(ref, c, lead=()):
        width = ref.shape[-1] // EXPERT_W_CHUNKS
        return ref[lead + (slice(None), slice(c * width, (c + 1) * width))]

    def cast_chunk(src_ref, dst_ref, c):
        wc = chunk_of(src_ref, c, (0,)).astype(BF16)
        width = dst_ref.shape[-1] // EXPERT_W_CHUNKS
        dst_ref[:, c * width:(c + 1) * width] = wc
        return wc

    @pl.when(jnp.logical_and(active, new_expert))
    def _():
        ffn(functools.partial(cast_chunk, w1_ref, w1b_ref), functools.partial(cast_chunk, w2_ref, w2b_ref))

    @pl.when(jnp.logical_and(active, jnp.logical_not(new_expert)))
    def _():
        ffn(functools.partial(chunk_of, w1b_ref), functools.partial(chunk_of, w2b_ref))

    @pl.when(jnp.logical_not(active))
    def _():
        o_ref[...] = jnp.zeros_like(o_ref)


def _experts(block_expert, n_active, xs, w1_all, b1, w2_all, b2, layer, mb):
    n_slots = xs.shape[0]
    depth, n_exp, d, f2 = w1_all.shape
    f = f2 // 2
    nb = n_slots // mb
    w1 = w1_all.reshape(depth * n_exp, d, f2)
    w2 = w2_all.reshape(depth * n_exp, f, d)
    block_expert = block_expert + layer * n_exp
    return pl.pallas_call(
        functools.partial(_expert_kernel, d_expert=f),
        out_shape=jax.ShapeDtypeStruct((n_slots, d // 2), I32),
        grid_spec=pltpu.PrefetchScalarGridSpec(
            num_scalar_prefetch=2,
            grid=(nb,),
            in_specs=[pl.BlockSpec((mb, d // 2), lambda i, be, na: (jnp.minimum(i, na[0] - 1), 0)),
                      pl.BlockSpec((1, d, f2), lambda i, be, na: (be[i], 0, 0)),
                      pl.BlockSpec((1, 1, f2), lambda i, be, na: (be[i], 0, 0)),
                      pl.BlockSpec((1, f, d), lambda i, be, na: (be[i], 0, 0)),
                      pl.BlockSpec((1, 1, d), lambda i, be, na: (be[i], 0, 0))],
            out_specs=pl.BlockSpec((mb, d // 2), lambda i, be, na: (i, 0)),
            scratch_shapes=[pltpu.VMEM((d, f2), BF16), pltpu.VMEM((f, d), BF16)]),
        compiler_params=_cparams("arbitrary"),
        name="moe_experts",
    )(block_expert, n_active, xs, w1, b1.reshape(depth * n_exp, 1, f2), w2, b2.reshape(depth * n_exp, 1, d))


def _combine_kernel(gstart_ref, gcnt_ref, npass_ref, x1_ref, lr_ref, ge_ref, g_ref, b_ref, ys_ref, o_ref,
                    ybuf, acc_ref, sem, *frame_scratch, tm, nb, alpha, frames):
    b = pl.program_id(0)
    slot = b % 2

    def slab_copy(bb, c, e, s, action):
        _slab_dma(gstart_ref, gcnt_ref, bb, c, e, action, lambda start, rows: pltpu.make_async_copy(
            ys_ref.at[pl.ds(start, rows)], ybuf.at[s, pl.ds(e * SLAB, rows)], sem.at[s]), halves=False)

    @pl.when(b == 0)
    def _():
        ybuf[...] = jnp.zeros_like(ybuf)
        for e in range(N_EXPERTS):
            slab_copy(0, 0, e, 0, "start")

    @pl.when(b + 1 < nb)
    def _():
        for e in range(N_EXPERTS):
            slab_copy(b + 1, 0, e, 1 - slot, "start")

    for e in range(N_EXPERTS):
        slab_copy(b, 0, e, slot, "wait")

    lane = lax.broadcasted_iota(I32, (1, LANES), 1)
    lr = lr_ref[...]
    ge = ge_ref[...]

    def weighted_sum(rank0):
        tiles = []
        for col0 in range(0, N_EXPERTS * SLAB, LANES):
            col = col0 + lane
            rank = jnp.full((tm, LANES), -2.0, F32)
            gate = jnp.zeros((tm, LANES), F32)
            want = jnp.zeros((1, LANES), I32)
            for e in range(col0 // SLAB, min((col0 + LANES - 1) // SLAB, N_EXPERTS - 1) + 1):
                inside = jnp.logical_and(col >= e * SLAB, col < (e + 1) * SLAB)
                rank = jnp.where(inside, lr[:, e:e + 1], rank)
                gate = jnp.where(inside, ge[:, e:e + 1], gate)
                want = jnp.where(inside, col - e * SLAB + rank0, want)
            tiles.append(jnp.where(rank == want.astype(F32), gate, 0.0).astype(BF16))
        sel = jnp.concatenate(tiles, axis=1)
        return jnp.dot(sel, _unpack_bf16_pairs(ybuf[slot]), preferred_element_type=F32)

    acc_ref[...] = alpha * x1_ref[...] + weighted_sum(0)

    @pl.when(npass_ref[b] > 1)
    def _():
        def extra_pass(c, carry):
            for action in ("start", "wait"):
                for e in range(N_EXPERTS):
                    @pl.when(gcnt_ref[b * N_EXPERTS + e] > c * SLAB)
                    def _():
                        slab_copy(b, c, e, slot, action)
            acc_ref[...] += weighted_sum(c * SLAB)
            return carry

        lax.fori_loop(1, npass_ref[b], extra_pass, 0)

    result = _layer_norm(acc_ref[...], g_ref[...], b_ref[...])
    if frames is None:
        o_ref[...] = result
        return

    seq_len, lead = frames
    obuf, osem = frame_scratch

    def frame_copies(bb, s, action):
        for piece in range(tm // lead):
            row0 = bb * tm + piece * lead
            sq = jnp.int32(0)
            for j in range(1, nb * tm // seq_len):
                sq = sq + (row0 >= j * seq_len).astype(I32)
            pos = row0 - sq * seq_len

            @pl.when(pos >= lead)
            def _():
                dst = pl.multiple_of(sq * (seq_len - lead) + pos - lead, lead)
                getattr(pltpu.make_async_copy(obuf.at[s, pl.ds(piece * lead, lead)],
                                              o_ref.at[pl.ds(dst, lead)], osem.at[s]), action)()

    @pl.when(b >= 2)
    def _():
        frame_copies(b - 2, slot, "wait")

    obuf[slot] = result
    frame_copies(b, slot, "start")

    @pl.when(b == nb - 1)
    def _():
        frame_copies(b, slot, "wait")
        if nb >= 2:
            frame_copies(b - 1, 1 - slot, "wait")


def _combine(tables, x1, lr, ge, ln_g, ln_b, ys, alpha, tm, frames=None):
    t_all, d = x1.shape
    nb = t_all // tm
    ntab = len(tables)
    row = lambda i, *_: (i, 0)
    const = lambda i, *_: (0, 0)
    scratch = [pltpu.VMEM((2, N_EXPERTS * SLAB, d // 2), I32), pltpu.VMEM((tm, d), F32),
               pltpu.SemaphoreType.DMA((2,))]
    if frames is None:
        out_rows, out_spec = t_all, pl.BlockSpec((tm, d), row)
    else:
        seq_len, lead = frames
        assert tm % lead == 0 and seq_len % lead == 0
        out_rows, out_spec = t_all // seq_len * (seq_len - lead), pl.BlockSpec(memory_space=pl.ANY)
        scratch += [pltpu.VMEM((2, tm, d), F32), pltpu.SemaphoreType.DMA((2,))]
    return pl.pallas_call(
        functools.partial(_combine_kernel, tm=tm, nb=nb, alpha=alpha, frames=frames),
        out_shape=jax.ShapeDtypeStruct((out_rows, d), F32),
        grid_spec=pltpu.PrefetchScalarGridSpec(
            num_scalar_prefetch=ntab,
            grid=(nb,),
            in_specs=[pl.BlockSpec((tm, d), row),
                      pl.BlockSpec((tm, LANES), row),
                      pl.BlockSpec((tm, LANES), row),
                      pl.BlockSpec((1, d), const),
                      pl.BlockSpec((1, d), const),
                      pl.BlockSpec(memory_space=pl.ANY)],
            out_specs=out_spec,
            scratch_shapes=scratch),
        compiler_params=_cparams("arbitrary"),
        name="moe_combine",
    )(*tables, x1, lr, ge, ln_g.reshape(1, d), ln_b.reshape(1, d), ys)


def _lanes(values, offset):
    return jnp.zeros((1, LANES), F32).at[0, offset:offset + values.shape[0]].set(values.astype(F32))


def _split_in_proj(w_in_layer):
    offs = np.concatenate([[0], np.cumsum(IN_PROJ_SIZES)])
    seg = lambda n: w_in_layer[:, offs[n]:offs[n + 1]]
    qk_scale = DIFF_QK_DIM ** -0.5 * LOG2_E
    fox_scale = FOX_HEAD_DIM ** -0.5 * LOG2_E
    wa = jnp.concatenate([seg(0) * qk_scale, seg(1), seg(2), seg(3) * fox_scale, seg(4), seg(5)], axis=1)
    wg = jnp.concatenate([seg(7), seg(10)], axis=1)
    d = w_in_layer.shape[0]
    ws = jnp.zeros((d, LANES), F32)
    ws = ws.at[:, GATE_FF:GATE_FF + FOX_HEADS].set(seg(6))
    ws = ws.at[:, GATE_BETA:GATE_BETA + GDN_HEADS].set(seg(8))
    ws = ws.at[:, GATE_DECAY:GATE_DECAY + GDN_HEADS].set(seg(9))
    return wa.astype(BF16), wg.astype(BF16), ws


def _diff_lambda_init(layer):
    return 0.8 - 0.6 * math.exp(-0.3 * layer)


def _pick_block(seq_len, candidates):
    for c in candidates:
        if seq_len % c == 0:
            return c
    raise ValueError(f"unsupported sequence length {seq_len}")


MOE_ROWS = 512
TOKEN_BLOCK = 256
assert MOE_ROWS % ZERO_FILL_SIZES[0] == 0 and SLAB + MOE_ROWS - GROUP_ALIGN < 2 * ZERO_FILL_SIZES[0]


def _layer(stream, layer, seq_len, n_lead, alpha, p, frames_only):
    t_all, d = stream.shape
    attn_tb = _pick_block(seq_len, (640, 512, 256, 128))
    wa, wg, ws = _split_in_proj(p["w_in"])
    attn, gdn_slab, gates, key_bias = _in_proj(stream, wa, wg, ws, _lanes(p["fox_forget_b"], GATE_FF),
                                               seq_len, n_lead, attn_tb)

    lv = p["diff_lambda"].astype(F32)
    lambda_init = _diff_lambda_init(layer)
    lam = (jnp.exp(jnp.sum(lv[0] * lv[1])) - jnp.exp(jnp.sum(lv[2] * lv[3])) + lambda_init).reshape(1)
    y_diff = _diff_attention(attn, lam, p["diff_subln_g"], seq_len, n_lead, lambda_init, attn_tb)

    y_fox = _fox_attention(attn, key_bias, seq_len, attn_tb)

    conv_w = jnp.zeros((SUBLANES, GDN_CONV_CH), F32).at[:CONV_K].set(p["gdn_conv_w"])
    y_gdn = _gated_deltanet(gdn_slab, gates, conv_w, _lanes(p["gdn_a_log"], GATE_DECAY),
                            _lanes(p["gdn_dt_bias"], GATE_DECAY), p["gdn_norm_g"], seq_len, attn_tb)

    rw = jnp.zeros((d, LANES), F32).at[:, :N_EXPERTS].set(p["router_w"])
    rb = _lanes(p["router_b"], 0)
    tm = TOKEN_BLOCK
    x1, lr, lrt, ge, counts = _mix_router(stream, y_diff, y_fox, y_gdn, p["w_out"].astype(BF16),
                                          p["ln1_g"], p["ln1_b"], rw, rb, alpha, tm, seq_len, n_lead)

    mb = MOE_ROWS
    nb = t_all // tm
    n_slots = -(-(t_all * TOP_K + (GROUP_ALIGN - 1) * nb * N_EXPERTS
                  + N_EXPERTS * (SLAB + mb - GROUP_ALIGN)) // mb) * mb
    n_blocks = n_slots // mb
    cnt = counts[:, 0, :N_EXPERTS].astype(I32)
    rows = (cnt + GROUP_ALIGN - 1) // GROUP_ALIGN * GROUP_ALIGN
    total = jnp.sum(rows, axis=0)
    region = (total + SLAB + mb - 1) // mb * mb
    pad_end = jnp.cumsum(region)
    pad_start = pad_end - region
    gstart = (pad_start[None, :] + jnp.cumsum(rows, axis=0) - rows).reshape(-1)
    npass = jnp.maximum((jnp.max(cnt, axis=1) + SLAB - 1) // SLAB, 1)
    block_first = jnp.arange(n_blocks, dtype=I32) * mb
    block_expert = jnp.minimum(jnp.sum((pad_end[None, :] <= block_first[:, None]).astype(I32), axis=1),
                               N_EXPERTS - 1)
    n_active = (pad_end[-1] // mb).astype(I32).reshape(1)
    group_tables = (gstart.astype(I32), cnt.reshape(-1), npass.astype(I32))

    xs = _dispatch(group_tables + ((pad_start + total).astype(I32), (region - total).astype(I32)),
                   x1, lrt, n_slots, tm)
    ys = _experts(block_expert, n_active, xs, p["expert_w1"], p["expert_b1"], p["expert_w2"], p["expert_b2"],
                  layer, mb)
    frames = (seq_len, n_lead + N_META) if frames_only else None
    return _combine(group_tables, x1, lr, ge, p["ln2_g"], p["ln2_b"], ys, alpha, tm, frames)


def kernel(x, meta_tokens, ln_in_g, ln_in_b, w_in, diff_lambda, diff_subln_g, fox_forget_b, gdn_conv_w,
           gdn_a_log, gdn_dt_bias, gdn_norm_g, w_out, ln1_g, ln1_b, router_w, router_b, expert_w1,
           expert_b1, expert_w2, expert_b2, ln2_g, ln2_b):
    bsz, s, d = x.shape
    depth = w_in.shape[0]
    n_lead = (-(s + N_META)) % Q_BLOCK
    if n_lead + N_META != Q_BLOCK:
        raise ValueError("sequence length must be a multiple of the query block")
    seq_len = n_lead + N_META + s
    alpha = (2 * depth) ** 0.25
    stream = _ln_in(x, meta_tokens.astype(x.dtype), ln_in_g, ln_in_b, n_lead,
                    _pick_block(seq_len, (640, 512, 256, 128))).reshape(bsz * seq_len, d)
    per_layer = dict(w_in=w_in, diff_lambda=diff_lambda, diff_subln_g=diff_subln_g, fox_forget_b=fox_forget_b,
                     gdn_conv_w=gdn_conv_w, gdn_a_log=gdn_a_log, gdn_dt_bias=gdn_dt_bias,
                     gdn_norm_g=gdn_norm_g, w_out=w_out, ln1_g=ln1_g, ln1_b=ln1_b, router_w=router_w,
                     router_b=router_b, expert_w1=expert_w1, expert_b1=expert_b1, expert_w2=expert_w2,
                     expert_b2=expert_b2, ln2_g=ln2_g, ln2_b=ln2_b)
    stacked = ("expert_w1", "expert_b1", "expert_w2", "expert_b2")
    for layer in range(depth):
        p = {k: (v if k in stacked else v[layer]) for k, v in per_layer.items()}
        stream = _layer(stream, layer, seq_len, n_lead, alpha, p, frames_only=layer == depth - 1)
    return stream.reshape(bsz, s, d)
```

```python
import functools
import math

import numpy as np
import jax
import jax.numpy as jnp
from jax import lax
from jax.experimental import pallas as pl
from jax.experimental.pallas import tpu as pltpu

F32 = jnp.float32
BF16 = jnp.bfloat16
I32 = jnp.int32

CHUNK = 64
CHUNK_SHIFT = 6
Q_BLOCK = 128
N_META = 16
DIFF_HEADS = 4
DIFF_QK_DIM = 64
DIFF_V_DIM = 128
FOX_HEADS = 4
FOX_HEAD_DIM = 64
GDN_HEADS = 4
GDN_HEAD_DIM = 64
CONV_K = 4
N_EXPERTS = 32
TOP_K = 4
SWIGLU_LIMIT = 7.0
SWIGLU_ALPHA = 1.702
MASK_VALUE = -1e30
LOG2_E = 1.4426950408889634
FOX_BIAS_PIECES = 3

DIFF_QK_WIDTH = DIFF_HEADS * 2 * DIFF_QK_DIM
DIFF_V_WIDTH = DIFF_HEADS * DIFF_V_DIM
FOX_WIDTH = FOX_HEADS * FOX_HEAD_DIM
GDN_WIDTH = GDN_HEADS * GDN_HEAD_DIM
GDN_CONV_CH = 3 * GDN_WIDTH
IN_PROJ_SIZES = (DIFF_QK_WIDTH, DIFF_QK_WIDTH, DIFF_V_WIDTH, FOX_WIDTH, FOX_WIDTH, FOX_WIDTH, FOX_HEADS,
                 GDN_CONV_CH, GDN_HEADS, GDN_HEADS, GDN_WIDTH)

LANES = 128
SUBLANES = 8
VMEM_LIMIT = 56 * 1024 * 1024

GATE_FF = 0
GATE_BETA = 4
GATE_DECAY = 8

ATTN_WIDTH = 2 * DIFF_QK_WIDTH + DIFF_V_WIDTH + 3 * FOX_WIDTH
GDN_SLAB = GDN_CONV_CH + GDN_WIDTH


def _cparams(*sem):
    return pltpu.CompilerParams(dimension_semantics=tuple(sem), vmem_limit_bytes=VMEM_LIMIT)


def _dot(a, b):
    return jnp.dot(a.astype(BF16), b.astype(BF16), preferred_element_type=F32)


def _split2(x):
    hi = x.astype(BF16)
    lo = (x - hi.astype(F32)).astype(BF16)
    return hi, lo


def _split3(x):
    hi = x.astype(BF16)
    r = x - hi.astype(F32)
    mid = r.astype(BF16)
    lo = (r - mid.astype(F32)).astype(BF16)
    return hi, mid, lo


def _dot3(a, b):
    ah, al = _split2(a)
    bh, bl = _split2(b)
    return (jnp.dot(ah, bh, preferred_element_type=F32) + jnp.dot(ah, bl, preferred_element_type=F32)
            + jnp.dot(al, bh, preferred_element_type=F32))


def _dot_exact_lhs(a01, b):
    a = a01.astype(BF16)
    hi, mid, lo = _split3(b)
    return (jnp.dot(a, hi, preferred_element_type=F32) + jnp.dot(a, mid, preferred_element_type=F32)
            + jnp.dot(a, lo, preferred_element_type=F32))


_BDIMS = (((2,), (1,)), ((0,), (0,)))
_BDIMS_NT = (((2,), (2,)), ((0,), (0,)))


def _bdot(a, b):
    return lax.dot_general(a.astype(BF16), b.astype(BF16), _BDIMS, preferred_element_type=F32)


def _bdot_nt(a, b):
    return lax.dot_general(a.astype(BF16), b.astype(BF16), _BDIMS_NT, preferred_element_type=F32)


def _layer_norm(x, g, b, eps=1e-5):
    mu = jnp.mean(x, axis=-1, keepdims=True)
    xc = x - mu
    var = jnp.mean(xc * xc, axis=-1, keepdims=True)
    return xc * lax.rsqrt(var + eps) * g + b


def _sigmoid(x):
    return 1.0 / (1.0 + jnp.exp(-x))


def _softplus(x):
    return jnp.maximum(x, 0.0) + jnp.log1p(jnp.exp(-jnp.abs(x)))


def _log_sigmoid(x):
    return -_softplus(-x)


def _ln_in_kernel(*refs, n_lead, n_sub):
    x_refs = refs[:n_sub]
    meta_ref, g_ref, b_ref, o_ref = refs[n_sub:]
    i = pl.program_id(1)
    g = g_ref[...]
    b = b_ref[...]
    blk = n_lead + N_META

    @pl.when(i == 0)
    def _():
        o_ref[0, :n_lead, :] = jnp.broadcast_to(b, (n_lead, b.shape[-1]))
        o_ref[0, n_lead:blk, :] = _layer_norm(meta_ref[...], g, b)

    @pl.when(i > 0)
    def _():
        o_ref[0, :blk, :] = _layer_norm(x_refs[0][0], g, b)

    for j in range(1, n_sub):
        o_ref[0, j * blk:(j + 1) * blk, :] = _layer_norm(x_refs[j][0], g, b)


def _ln_in(x, meta, g, b, n_lead, rows):
    bsz, s, d = x.shape
    l = n_lead + N_META + s
    blk = n_lead + N_META
    n_sub = rows // blk
    x_specs = [pl.BlockSpec((1, blk, d), functools.partial(
        lambda bb, i, j: (bb, jnp.maximum(n_sub * i + j - 1, 0), 0), j=j)) for j in range(n_sub)]
    return pl.pallas_call(
        functools.partial(_ln_in_kernel, n_lead=n_lead, n_sub=n_sub),
        out_shape=jax.ShapeDtypeStruct((bsz, l, d), F32),
        grid=(bsz, l // rows),
        in_specs=x_specs + [pl.BlockSpec((N_META, d), lambda bb, i: (0, 0)),
                            pl.BlockSpec((1, d), lambda bb, i: (0, 0)),
                            pl.BlockSpec((1, d), lambda bb, i: (0, 0))],
        out_specs=pl.BlockSpec((1, rows, d), lambda bb, i: (bb, i, 0)),
        compiler_params=_cparams("arbitrary", "arbitrary"),
        name="ln_in",
    )(*([x] * n_sub), meta, g.reshape(1, d), b.reshape(1, d))


PROJ_COL_CHUNK = 768


def _proj_kernel(x_ref, wa_ref, wg_ref, ws_ref, fb_ref, oa_ref, og_ref, os_ref, kb_ref, carry_ref, *, tm, n_lead):
    i = pl.program_id(1)
    pos = i * tm + lax.broadcasted_iota(I32, (tm, 1), 0)
    x = jnp.where(pos >= n_lead, x_ref[...], 0.0)
    xb = x.astype(BF16)
    for c in range(0, ATTN_WIDTH, PROJ_COL_CHUNK):
        oa_ref[:, c:c + PROJ_COL_CHUNK] = jnp.dot(
            xb, wa_ref[:, c:c + PROJ_COL_CHUNK], preferred_element_type=F32).astype(BF16)
    og_ref[...] = jnp.dot(xb, wg_ref[...], preferred_element_type=F32)
    gates = _dot3(x, ws_ref[...])
    os_ref[...] = gates

    @pl.when(i == 0)
    def _():
        carry_ref[...] = jnp.zeros_like(carry_ref)

    log_f = _log_sigmoid(gates + fb_ref[...])
    r = lax.broadcasted_iota(I32, (LANES, LANES), 0)
    c = lax.broadcasted_iota(I32, (LANES, LANES), 1)
    tri = jnp.where(c <= r, 1.0, 0.0).astype(BF16)
    carry = carry_ref[...]
    cums = []
    for r0 in range(0, tm, LANES):
        cums.append(_dot_exact_lhs(tri, log_f[r0:r0 + LANES, :]) + carry)
        carry = cums[-1][LANES - 1:LANES, :]
    carry_ref[...] = carry
    cum = jnp.concatenate(cums, axis=0)
    lane = lax.broadcasted_iota(I32, (1, LANES), 1)
    out = jnp.zeros((tm, LANES), F32)
    for hh in range(FOX_HEADS):
        bias = jnp.where(pos >= n_lead, -LOG2_E * cum[:, GATE_FF + hh:GATE_FF + hh + 1], MASK_VALUE)
        for n, piece in enumerate(_split3(bias)):
            out = jnp.where(lane == _fox_bias_lane(hh) + n, piece.astype(F32), out)
    kb_ref[...] = out.astype(BF16)


def _in_proj(stream, wa, wg, ws, forget_bias_lanes, seq_len, n_lead, tm):
    t_all, d = stream.shape
    nb = seq_len // tm
    row = lambda bb, i: (bb * nb + i, 0)
    const = lambda bb, i: (0, 0)
    return pl.pallas_call(
        functools.partial(_proj_kernel, tm=tm, n_lead=n_lead),
        out_shape=(jax.ShapeDtypeStruct((t_all, ATTN_WIDTH), BF16),
                   jax.ShapeDtypeStruct((t_all, GDN_SLAB), F32),
                   jax.ShapeDtypeStruct((t_all, LANES), F32),
                   jax.ShapeDtypeStruct((t_all, LANES), BF16)),
        grid=(t_all // seq_len, nb),
        in_specs=[pl.BlockSpec((tm, d), row),
                  pl.BlockSpec((d, ATTN_WIDTH), const),
                  pl.BlockSpec((d, GDN_SLAB), const),
                  pl.BlockSpec((d, LANES), const),
                  pl.BlockSpec((1, LANES), const)],
        out_specs=(pl.BlockSpec((tm, ATTN_WIDTH), row),
                   pl.BlockSpec((tm, GDN_SLAB), row),
                   pl.BlockSpec((tm, LANES), row),
                   pl.BlockSpec((tm, LANES), row)),
        scratch_shapes=[pltpu.VMEM((1, LANES), F32)],
        compiler_params=_cparams("arbitrary", "arbitrary"),
        name="in_proj",
    )(stream, wa, wg, ws, forget_bias_lanes)


def _online_softmax_step(s, vt, m_ref, l_ref, a_ref):
    m_old = m_ref[...]
    m_new = jnp.maximum(m_old, jnp.max(s, axis=0, keepdims=True))
    alpha = jnp.exp2(m_old - m_new)
    p = jnp.exp2(s - m_new)
    l_ref[...] = alpha * l_ref[...] + jnp.sum(p, axis=0, keepdims=True)
    a_ref[...] = alpha * a_ref[...] + jnp.dot(vt, p.astype(BF16), preferred_element_type=F32)
    m_ref[...] = m_new


def _pipelined_blocks(i, buf_a, buf_b, produce_first, produce, consume):
    produce_first(0, buf_a)

    def pair(t, carry):
        j = 2 * t
        produce(j + 1, buf_b)
        consume(j, buf_a)
        produce(j + 2, buf_a)
        consume(j + 1, buf_b)
        return carry

    lax.fori_loop(0, i // 2, pair, 0)

    @pl.when(i % 2 == 0)
    def _():
        consume(i, buf_a, diagonal=True)

    @pl.when(i % 2 == 1)
    def _():
        produce(i, buf_b)
        consume(i - 1, buf_a)
        consume(i, buf_b, diagonal=True)


def _transpose_to(dst_ref, src_ref, n_rows):
    for c in range(src_ref.shape[1] // LANES):
        def body(t, carry):
            r0 = pl.multiple_of(t * LANES, LANES)
            tile = src_ref[pl.ds(r0, LANES), c * LANES:(c + 1) * LANES].astype(F32)
            dst_ref[c * LANES:(c + 1) * LANES, pl.ds(r0, LANES)] = tile.T.astype(dst_ref.dtype)
            return carry
        lax.fori_loop(0, n_rows // LANES, body, 0)


def _init_softmax_state(m_ref, l_ref, a_ref):
    m_ref[...] = jnp.full(m_ref.shape, MASK_VALUE, F32)
    l_ref[...] = jnp.zeros(l_ref.shape, F32)
    a_ref[...] = jnp.zeros(a_ref.shape, F32)


def _diff_attn_kernel(lam_ref, q_ref, k_ref, v_ref, g_ref, o_ref, vt_ref, sa_ref, sb_ref,
                      m0, l0, a0, m1, l1, a1, *, tb, n_lead, seq_len, out_scale):
    i = pl.program_id(2)

    @pl.when(i == 0)
    def _():
        _transpose_to(vt_ref, v_ref, seq_len)

    qt = q_ref[...].astype(F32).T
    row = lax.broadcasted_iota(I32, (LANES, 1), 0)
    qt0 = jnp.where(row < DIFF_QK_DIM, qt, 0.0).astype(BF16)
    qt1 = jnp.where(row >= DIFF_QK_DIM, qt, 0.0).astype(BF16)
    _init_softmax_state(m0, l0, a0)
    _init_softmax_state(m1, l1, a1)

    def produce(j, buf, pad_bias=False):
        start = pl.multiple_of(j * tb, tb)
        k = k_ref[pl.ds(start, tb), :]
        for half, qth in enumerate((qt0, qt1)):
            s = jnp.dot(k, qth, preferred_element_type=F32)
            if pad_bias:
                kpos = lax.broadcasted_iota(I32, (tb, 1), 0)
                s = s + jnp.where(kpos >= n_lead, 0.0, MASK_VALUE)
            buf[half] = s

    def consume(j, buf, diagonal=False):
        start = pl.multiple_of(j * tb, tb)
        vt = vt_ref[:, pl.ds(start, tb)]
        for half, state in enumerate(((m0, l0, a0), (m1, l1, a1))):
            s = buf[half]
            if diagonal:
                kpos = j * tb + lax.broadcasted_iota(I32, (tb, 1), 0)
                qpos = i * tb + lax.broadcasted_iota(I32, (1, tb), 1)
                s = jnp.where(kpos >> CHUNK_SHIFT <= qpos >> CHUNK_SHIFT, s, MASK_VALUE)
            _online_softmax_step(s, vt, *state)

    _pipelined_blocks(i, sa_ref, sb_ref, functools.partial(produce, pad_bias=True), produce, consume)

    lam = lam_ref[0]
    o = a0[...] / l0[...] - lam * (a1[...] / l1[...])
    o = o * lax.rsqrt(jnp.mean(o * o, axis=0, keepdims=True) + 1e-5) * g_ref[...] * out_scale
    o_ref[...] = o.T.astype(o_ref.dtype)


def _diff_attention(attn, lam, subln_g, seq_len, n_lead, lambda_init, tb):
    t_all = attn.shape[0]
    bsz = t_all // seq_len
    nq = seq_len // tb
    kcol = DIFF_QK_WIDTH // LANES
    vcol = 2 * DIFF_QK_WIDTH // LANES
    scratch = [pltpu.VMEM((DIFF_V_DIM, seq_len), BF16),
               pltpu.VMEM((2, tb, tb), F32), pltpu.VMEM((2, tb, tb), F32)]
    for _ in range(2):
        scratch += [pltpu.VMEM((1, tb), F32), pltpu.VMEM((1, tb), F32), pltpu.VMEM((DIFF_V_DIM, tb), F32)]
    return pl.pallas_call(
        functools.partial(_diff_attn_kernel, tb=tb, n_lead=n_lead, seq_len=seq_len,
                          out_scale=1.0 - lambda_init),
        out_shape=jax.ShapeDtypeStruct((t_all, DIFF_V_WIDTH), BF16),
        grid_spec=pltpu.PrefetchScalarGridSpec(
            num_scalar_prefetch=1,
            grid=(bsz, DIFF_HEADS, nq),
            in_specs=[pl.BlockSpec((tb, LANES), lambda b, h, i, lam: (b * nq + i, h)),
                      pl.BlockSpec((seq_len, LANES), lambda b, h, i, lam: (b, kcol + h)),
                      pl.BlockSpec((seq_len, LANES), lambda b, h, i, lam: (b, vcol + h)),
                      pl.BlockSpec((DIFF_V_DIM, 1), lambda b, h, i, lam: (0, 0))],
            out_specs=pl.BlockSpec((tb, DIFF_V_DIM), lambda b, h, i, lam: (b * nq + i, h)),
            scratch_shapes=scratch),
        compiler_params=_cparams("arbitrary", "arbitrary", "arbitrary"),
        name="diff_attention",
    )(lam, attn, attn, attn, subln_g.reshape(DIFF_V_DIM, 1))


def _fox_bias_lane(hh):
    return (1 - hh % 2) * FOX_HEAD_DIM + FOX_BIAS_PIECES * (hh // 2)


def _fox_attn_kernel(q_ref, k_ref, v_ref, kb_ref, o_ref, vt_ref, sa_ref, sb_ref, *scratch, tb, seq_len):
    i = pl.program_id(1)
    pairs = FOX_HEADS // 2

    @pl.when(i == 0)
    def _():
        _transpose_to(vt_ref, v_ref, seq_len)

    row = lax.broadcasted_iota(I32, (LANES, 1), 0)
    lane = lax.broadcasted_iota(I32, (1, LANES), 1)
    qts = []
    for p in range(pairs):
        qt = q_ref[:, p * LANES:(p + 1) * LANES].astype(F32).T
        for r in range(2):
            own = jnp.logical_and(row >= r * FOX_HEAD_DIM, row < (r + 1) * FOX_HEAD_DIM)
            b0 = _fox_bias_lane(2 * p + r)
            ones = jnp.logical_and(row >= b0, row < b0 + FOX_BIAS_PIECES)
            qts.append(jnp.where(own, qt, jnp.where(ones, 1.0, 0.0)).astype(BF16))
    for hh in range(FOX_HEADS):
        _init_softmax_state(*scratch[3 * hh:3 * hh + 3])

    def produce(j, buf):
        start = pl.multiple_of(j * tb, tb)
        kb = kb_ref[pl.ds(start, tb), :]
        for p in range(pairs):
            k = k_ref[pl.ds(start, tb), p * LANES:(p + 1) * LANES]
            for r in range(2):
                own = jnp.logical_and(lane >= r * FOX_HEAD_DIM, lane < (r + 1) * FOX_HEAD_DIM)
                buf[2 * p + r] = jnp.dot(jnp.where(own, k, kb), qts[2 * p + r],
                                         preferred_element_type=F32)

    def consume(j, buf, diagonal=False):
        start = pl.multiple_of(j * tb, tb)
        for hh in range(FOX_HEADS):
            vt = vt_ref[hh * FOX_HEAD_DIM:(hh + 1) * FOX_HEAD_DIM, pl.ds(start, tb)]
            s = buf[hh]
            if diagonal:
                kpos = j * tb + lax.broadcasted_iota(I32, (tb, 1), 0)
                qpos = i * tb + lax.broadcasted_iota(I32, (1, tb), 1)
                s = jnp.where(kpos <= qpos, s, MASK_VALUE)
            _online_softmax_step(s, vt, *scratch[3 * hh:3 * hh + 3])

    _pipelined_blocks(i, sa_ref, sb_ref, produce, produce, consume)

    for p in range(pairs):
        _, l_lo, a_lo = scratch[6 * p:6 * p + 3]
        _, l_hi, a_hi = scratch[6 * p + 3:6 * p + 6]
        o = jnp.concatenate([a_lo[...] / l_lo[...], a_hi[...] / l_hi[...]], axis=0)
        o_ref[:, p * LANES:(p + 1) * LANES] = o.T.astype(o_ref.dtype)


def _fox_attention(attn, key_bias, seq_len, tb):
    t_all = attn.shape[0]
    bsz = t_all // seq_len
    nq = seq_len // tb
    base = (2 * DIFF_QK_WIDTH + DIFF_V_WIDTH) // FOX_WIDTH
    scratch = [pltpu.VMEM((FOX_WIDTH, seq_len), BF16),
               pltpu.VMEM((FOX_HEADS, tb, tb), F32), pltpu.VMEM((FOX_HEADS, tb, tb), F32)]
    for _ in range(FOX_HEADS):
        scratch += [pltpu.VMEM((1, tb), F32), pltpu.VMEM((1, tb), F32), pltpu.VMEM((FOX_HEAD_DIM, tb), F32)]
    return pl.pallas_call(
        functools.partial(_fox_attn_kernel, tb=tb, seq_len=seq_len),
        out_shape=jax.ShapeDtypeStruct((t_all, FOX_WIDTH), BF16),
        grid=(bsz, nq),
        in_specs=[pl.BlockSpec((tb, FOX_WIDTH), lambda b, i: (b * nq + i, base)),
                  pl.BlockSpec((seq_len, FOX_WIDTH), lambda b, i: (b, base + 1)),
                  pl.BlockSpec((seq_len, FOX_WIDTH), lambda b, i: (b, base + 2)),
                  pl.BlockSpec((seq_len, LANES), lambda b, i: (b, 0))],
        out_specs=pl.BlockSpec((tb, FOX_WIDTH), lambda b, i: (b * nq + i, 0)),
        scratch_shapes=scratch,
        compiler_params=_cparams("arbitrary", "arbitrary"),
        name="fox_attention",
    )(attn, attn, attn, key_bias)


def _gdn_kernel(x_ref, gate_ref, cw_ref, alog_ref, dtb_ref, ng_ref, o_ref,
                halo_ref, xs_ref, state_ref, *, rows):
    cg = pl.program_id(1)
    g_chunks = rows // CHUNK
    hd = GDN_HEAD_DIM

    @pl.when(cg == 0)
    def _():
        halo_ref[...] = jnp.zeros_like(halo_ref)
        state_ref[...] = jnp.zeros_like(state_ref)

    xin = x_ref[:, :GDN_CONV_CH]
    xs_ref[0:SUBLANES, :] = halo_ref[...]
    xs_ref[SUBLANES:SUBLANES + rows, :] = xin
    halo_ref[...] = xin[rows - SUBLANES:rows, :]
    y = jnp.zeros((rows, GDN_CONV_CH), F32)
    for tap in range(CONV_K):
        back = CONV_K - 1 - tap
        y = y + cw_ref[tap:tap + 1, :] * xs_ref[SUBLANES - back:SUBLANES - back + rows, :]
    y = y * _sigmoid(y)

    gates = gate_ref[...]
    beta_all = _sigmoid(gates)
    g_all = -jnp.exp(alog_ref[...]) * _softplus(gates + dtb_ref[...])
    rowi = lax.broadcasted_iota(I32, (rows, 1), 0) & (CHUNK - 1)
    gc_all = g_all
    shift = 1
    while shift < CHUNK:
        gc_all = gc_all + jnp.where(rowi >= shift, pltpu.roll(gc_all, shift, 0), 0.0)
        shift *= 2

    ri = lax.broadcasted_iota(I32, (CHUNK, CHUNK), 0)
    ci = lax.broadcasted_iota(I32, (CHUNK, CHUNK), 1)
    eye = jnp.where(ri == ci, 1.0, 0.0).astype(F32)
    causal = (ci <= ri)[None]
    strict = (ci < ri)[None]
    eye_b = jnp.broadcast_to(eye[None], (g_chunks, CHUNK, CHUNK))

    prep = []
    for h in range(GDN_HEADS):
        def head3(base):
            return y[:, base + h * hd:base + (h + 1) * hd]
        q = head3(0)
        k = head3(GDN_WIDTH)
        v = head3(2 * GDN_WIDTH)
        q = q * lax.rsqrt(jnp.sum(q * q, axis=-1, keepdims=True) + 1e-6) * (hd ** -0.5)
        k = k * lax.rsqrt(jnp.sum(k * k, axis=-1, keepdims=True) + 1e-6)
        beta = beta_all[:, GATE_BETA + h:GATE_BETA + h + 1]
        gcol = jnp.broadcast_to(gc_all[:, GATE_DECAY + h:GATE_DECAY + h + 1], (rows, hd))
        egc = jnp.exp(gcol)

        q3 = q.reshape(g_chunks, CHUNK, hd)
        k3 = k.reshape(g_chunks, CHUNK, hd)
        gcol3 = gcol.reshape(g_chunks, CHUNK, CHUNK)
        beta3 = jnp.broadcast_to(beta, (rows, hd)).reshape(g_chunks, CHUNK, hd)
        grow3 = jnp.swapaxes(gcol3, 1, 2)
        decay = jnp.exp(jnp.where(causal, gcol3 - grow3, MASK_VALUE))
        a = jnp.where(strict, _bdot_nt(k3, k3) * beta3 * decay, 0.0)

        x = -a
        tinv = eye_b + x
        span = 2
        while span < CHUNK:
            x = _bdot(x, x)
            tinv = tinv + _bdot(tinv, x)
            span *= 2

        vb = (v * beta).reshape(g_chunks, CHUNK, hd)
        kb = (k * beta * egc).reshape(g_chunks, CHUNK, hd)
        uw3 = _bdot(tinv, jnp.concatenate([vb, kb], axis=-1))
        qk3 = _bdot_nt(q3, k3) * decay
        qd3 = (q * egc).reshape(g_chunks, CHUNK, hd)
        glast3 = gcol3[:, CHUNK - 1:CHUNK, :]
        kdt3 = jnp.swapaxes(k3 * jnp.exp(glast3 - gcol3), 1, 2)
        cd3 = jnp.exp(glast3)
        k_uw = _bdot(kdt3, uw3)
        q_uw = _bdot(qk3, uw3)
        prep.append((k_uw[..., hd:], k_uw[..., :hd], qd3 - q_uw[..., hd:], q_uw[..., :hd], cd3))

    states = [state_ref[h] for h in range(GDN_HEADS)]
    outs = [[] for _ in range(GDN_HEADS)]
    for c in range(g_chunks):
        for h in range(GDN_HEADS):
            kw3, ku3, qp3, o03, cd3 = prep[h]
            state = states[h]
            outs[h].append(_dot(qp3[c], state) + o03[c])
            states[h] = state * cd3[c] + (ku3[c] - _dot(kw3[c], state))

    for h in range(GDN_HEADS):
        state_ref[h] = states[h]
        o = jnp.concatenate(outs[h], axis=0)
        o = o * lax.rsqrt(jnp.mean(o * o, axis=-1, keepdims=True) + 1e-6) * ng_ref[...]
        z = x_ref[:, GDN_CONV_CH + h * hd:GDN_CONV_CH + (h + 1) * hd]
        o_ref[:, h * hd:(h + 1) * hd] = (o * (z * _sigmoid(z))).astype(o_ref.dtype)


def _gated_deltanet(gdn_slab, gates, conv_w, a_log_lanes, dt_bias_lanes, norm_g, seq_len, rows):
    t_all = gdn_slab.shape[0]
    bsz = t_all // seq_len
    nb = seq_len // rows
    row = lambda b, i: (b * nb + i, 0)
    const = lambda b, i: (0, 0)
    return pl.pallas_call(
        functools.partial(_gdn_kernel, rows=rows),
        out_shape=jax.ShapeDtypeStruct((t_all, GDN_WIDTH), BF16),
        grid=(bsz, nb),
        in_specs=[pl.BlockSpec((rows, GDN_SLAB), row),
                  pl.BlockSpec((rows, LANES), row),
                  pl.BlockSpec((SUBLANES, GDN_CONV_CH), const),
                  pl.BlockSpec((1, LANES), const),
                  pl.BlockSpec((1, LANES), const),
                  pl.BlockSpec((1, GDN_HEAD_DIM), const)],
        out_specs=pl.BlockSpec((rows, GDN_WIDTH), row),
        scratch_shapes=[pltpu.VMEM((SUBLANES, GDN_CONV_CH), F32),
                        pltpu.VMEM((rows + SUBLANES, GDN_CONV_CH), F32),
                        pltpu.VMEM((GDN_HEADS, GDN_HEAD_DIM, GDN_HEAD_DIM), F32)],
        compiler_params=_cparams("arbitrary", "arbitrary"),
        name="gated_deltanet",
    )(gdn_slab, gates, conv_w, a_log_lanes, dt_bias_lanes, norm_g.reshape(1, GDN_HEAD_DIM))


def _mix_router_kernel(x_ref, yd_ref, yf_ref, yg_ref, wd_ref, wf_ref, wg_ref, g_ref, b_ref,
                       rw_ref, rb_ref, x1_ref, lr_ref, lrt_ref, ge_ref, cnt_ref, *,
                       tm, rb_rows, alpha, seq_len, n_lead, n_seq):
    mix = (jnp.dot(yd_ref[...], wd_ref[...], preferred_element_type=F32)
           + jnp.dot(yf_ref[...], wf_ref[...], preferred_element_type=F32)
           + jnp.dot(yg_ref[...], wg_ref[...], preferred_element_type=F32))
    x1 = _layer_norm(alpha * x_ref[...] + mix, g_ref[...], b_ref[...])
    x1_ref[...] = x1

    lane = lax.broadcasted_iota(I32, (tm, LANES), 1)
    lane_f = lane.astype(F32)
    logits = _dot3(x1, rw_ref[...]) + rb_ref[...]
    vals = jnp.where(lane < N_EXPERTS, logits, -jnp.inf)
    hits, tops = [], []
    for _ in range(TOP_K):
        top = jnp.max(vals, axis=-1, keepdims=True)
        idx = jnp.min(jnp.where(vals == top, lane_f, float(LANES)), axis=-1, keepdims=True)
        hit = lane_f == idx
        vals = jnp.where(hit, -jnp.inf, vals)
        hits.append(hit)
        tops.append(top)
    exps = [jnp.exp(t - tops[0]) for t in tops]
    denom = exps[0] + exps[1] + exps[2] + exps[3]
    row = pl.program_id(0) * tm + lax.broadcasted_iota(I32, (tm, 1), 0)
    is_pad = row < 0
    for sq in range(n_seq):
        is_pad = jnp.logical_or(is_pad, jnp.logical_and(row >= sq * seq_len, row < sq * seq_len + n_lead))
    onehot = jnp.zeros((tm, LANES), F32)
    gate_by_expert = jnp.zeros((tm, LANES), F32)
    for kk in range(TOP_K):
        hit = jnp.logical_and(hits[kk], jnp.logical_not(is_pad))
        onehot = jnp.where(hit, 1.0, onehot)
        gate_by_expert = jnp.where(hit, exps[kk] / denom, gate_by_expert)
    ge_ref[...] = gate_by_expert

    r = lax.broadcasted_iota(I32, (rb_rows, rb_rows), 0)
    c = lax.broadcasted_iota(I32, (rb_rows, rb_rows), 1)
    tri = jnp.where(c < r, 1.0, 0.0).astype(BF16)
    for blk in range(tm // rb_rows):
        oh = onehot[blk * rb_rows:(blk + 1) * rb_rows, :]
        rank = jnp.dot(tri, oh.astype(BF16), preferred_element_type=F32)
        local = jnp.where(oh > 0.0, rank, -1.0)
        lr_ref[blk * rb_rows:(blk + 1) * rb_rows, :] = local
        lrt_ref[blk * N_EXPERTS:(blk + 1) * N_EXPERTS, :] = local.T[:N_EXPERTS, :]
        cnt_ref[blk] = jnp.sum(oh, axis=0, keepdims=True)


def _mix_router(stream, yd, yf, yg, w_out_b, ln_g, ln_b, rw, rb, alpha, rb_rows, seq_len, n_lead):
    t_all, d = stream.shape
    n_rb = t_all // rb_rows
    per_step = max(k for k in (5, 4, 3, 2, 1) if n_rb % k == 0)
    tm = per_step * rb_rows
    row = lambda i: (i, 0)
    const = lambda i: (0, 0)
    wd = w_out_b[:DIFF_V_WIDTH]
    wf = w_out_b[DIFF_V_WIDTH:DIFF_V_WIDTH + FOX_WIDTH]
    wg = w_out_b[DIFF_V_WIDTH + FOX_WIDTH:]
    return pl.pallas_call(
        functools.partial(_mix_router_kernel, tm=tm, rb_rows=rb_rows, alpha=alpha, seq_len=seq_len,
                          n_lead=n_lead, n_seq=t_all // seq_len),
        out_shape=(jax.ShapeDtypeStruct((t_all, d), F32),
                   jax.ShapeDtypeStruct((t_all, LANES), F32),
                   jax.ShapeDtypeStruct((n_rb * N_EXPERTS, rb_rows), F32),
                   jax.ShapeDtypeStruct((t_all, LANES), F32),
                   jax.ShapeDtypeStruct((n_rb, 1, LANES), F32)),
        grid=(t_all // tm,),
        in_specs=[pl.BlockSpec((tm, d), row),
                  pl.BlockSpec((tm, DIFF_V_WIDTH), row),
                  pl.BlockSpec((tm, FOX_WIDTH), row),
                  pl.BlockSpec((tm, GDN_WIDTH), row),
                  pl.BlockSpec((DIFF_V_WIDTH, d), const),
                  pl.BlockSpec((FOX_WIDTH, d), const),
                  pl.BlockSpec((GDN_WIDTH, d), const),
                  pl.BlockSpec((1, d), const),
                  pl.BlockSpec((1, d), const),
                  pl.BlockSpec((d, LANES), const),
                  pl.BlockSpec((1, LANES), const)],
        out_specs=(pl.BlockSpec((tm, d), row),
                   pl.BlockSpec((tm, LANES), row),
                   pl.BlockSpec((per_step * N_EXPERTS, rb_rows), row),
                   pl.BlockSpec((tm, LANES), row),
                   pl.BlockSpec((per_step, 1, LANES), lambda i: (i, 0, 0))),
        compiler_params=_cparams("arbitrary"),
        name="mix_router",
    )(stream, yd, yf, yg, wd, wf, wg, ln_g.reshape(1, d), ln_b.reshape(1, d), rw, rb)


SLAB = 64
GROUP_ALIGN = 8
ZERO_FILL_SIZES = (512, 256, 128, 64, 32, 16, 8)


HIGH_HALF = -65536


def _pack_bf16_pairs(x):
    w = x.shape[1] // 2
    bits = lax.bitcast_convert_type(x.astype(BF16).astype(F32), I32)
    return (bits[:, w:] & HIGH_HALF) | ((bits[:, :w] >> 16) & 0xFFFF)


def _unpack_bf16_pairs(p):
    low = lax.bitcast_convert_type(p << 16, F32)
    high = lax.bitcast_convert_type(p & HIGH_HALF, F32)
    return jnp.concatenate([low, high], axis=1).astype(BF16)


def _slab_dma(gstart_ref, gcnt_ref, bb, c, e, action, make_copy, halves=True):
    idx = bb * N_EXPERTS + e
    start = pl.multiple_of(gstart_ref[idx] + c * SLAB, GROUP_ALIGN)
    if not halves:
        getattr(make_copy(start, SLAB), action)()
        return
    small = gcnt_ref[idx] - c * SLAB <= SLAB // 2
    for rows, cond in ((SLAB // 2, small), (SLAB, jnp.logical_not(small))):
        @pl.when(cond)
        def _():
            getattr(make_copy(start, rows), action)()


def _slab_rows(src_ref, rank_row0):
    want = (rank_row0 + lax.broadcasted_iota(I32, (SLAB, 1), 0)).astype(F32)
    return jnp.concatenate([jnp.where(src_ref[e:e + 1, :] == want, 1.0, 0.0).astype(BF16)
                            for e in range(N_EXPERTS)], axis=0)


def _dispatch_kernel(gstart_ref, gcnt_ref, npass_ref, zstart_ref, zrows_ref, x_ref, lrt_ref, xs_ref,
                     zbuf, zero_ref, sem, zsem, *, nb, n_slots):
    b = pl.program_id(0)
    slot = b % 2

    def slab_copy(bb, c, e, s, action):
        _slab_dma(gstart_ref, gcnt_ref, bb, c, e, action, lambda start, rows: pltpu.make_async_copy(
            zbuf.at[s, pl.ds(e * SLAB, rows)], xs_ref.at[pl.ds(start, rows)], sem.at[s]), halves=False)

    def zero_fill(act):
        for e in range(N_EXPERTS):
            rows = zrows_ref[e]
            for size in ZERO_FILL_SIZES:
                @pl.when((rows & size) != 0)
                def _():
                    start = pl.multiple_of(zstart_ref[e] + (rows & ~(2 * size - 1)), GROUP_ALIGN)
                    act(pltpu.make_async_copy(zero_ref.at[pl.ds(0, size)], xs_ref.at[pl.ds(start, size)], zsem))

    def zero_tail(act):
        first = zstart_ref[N_EXPERTS - 1] + zrows_ref[N_EXPERTS - 1]
        size = ZERO_FILL_SIZES[0]

        def body(t, carry):
            start = pl.multiple_of(first + t * size, GROUP_ALIGN)
            act(pltpu.make_async_copy(zero_ref, xs_ref.at[pl.ds(start, size)], zsem))
            return carry

        lax.fori_loop(0, (n_slots - first) // size, body, 0)

    @pl.when(b == 0)
    def _():
        zero_ref[...] = jnp.zeros_like(zero_ref)
        for fill in (zero_fill, zero_tail):
            fill(lambda cp: cp.start())
        for fill in (zero_fill, zero_tail):
            fill(lambda cp: cp.wait())

    xb = x_ref[...].astype(BF16)
    zbuf[slot] = _pack_bf16_pairs(jnp.dot(_slab_rows(lrt_ref, 0), xb, preferred_element_type=F32))

    @pl.when(b > 0)
    def _():
        @pl.when(npass_ref[b - 1] <= 1)
        def _():
            for e in range(N_EXPERTS):
                slab_copy(b - 1, 0, e, 1 - slot, "wait")

    for e in range(N_EXPERTS):
        slab_copy(b, 0, e, slot, "start")

    @pl.when(npass_ref[b] > 1)
    def _():
        for e in range(N_EXPERTS):
            slab_copy(b, 0, e, slot, "wait")

        def extra_pass(c, carry):
            zbuf[slot] = _pack_bf16_pairs(
                jnp.dot(_slab_rows(lrt_ref, c * SLAB), xb, preferred_element_type=F32))
            for action in ("start", "wait"):
                for e in range(N_EXPERTS):
                    @pl.when(gcnt_ref[b * N_EXPERTS + e] > c * SLAB)
                    def _():
                        slab_copy(b, c, e, slot, action)
            return carry

        lax.fori_loop(1, npass_ref[b], extra_pass, 0)

    @pl.when(b == nb - 1)
    def _():
        @pl.when(npass_ref[b] <= 1)
        def _():
            for e in range(N_EXPERTS):
                slab_copy(b, 0, e, slot, "wait")


def _dispatch(tables, x1, lrt, n_slots, tm):
    t_all, d = x1.shape
    nb = t_all // tm
    ntab = len(tables)
    return pl.pallas_call(
        functools.partial(_dispatch_kernel, nb=nb, n_slots=n_slots),
        out_shape=jax.ShapeDtypeStruct((n_slots, d // 2), I32),
        grid_spec=pltpu.PrefetchScalarGridSpec(
            num_scalar_prefetch=ntab,
            grid=(nb,),
            in_specs=[pl.BlockSpec((tm, d), lambda i, *_: (i, 0)),
                      pl.BlockSpec((N_EXPERTS, tm), lambda i, *_: (i, 0))],
            out_specs=pl.BlockSpec(memory_space=pl.ANY),
            scratch_shapes=[pltpu.VMEM((2, N_EXPERTS * SLAB, d // 2), I32),
                            pltpu.VMEM((ZERO_FILL_SIZES[0], d // 2), I32),
                            pltpu.SemaphoreType.DMA((2,)),
                            pltpu.SemaphoreType.DMA]),
        compiler_params=_cparams("arbitrary"),
        name="moe_dispatch",
    )(*tables, x1, lrt)


EXPERT_W_CHUNKS = 4


def _expert_kernel(be_ref, na_ref, x_ref, w1_ref, b1_ref, w2_ref, b2_ref, o_ref, w1b_ref, w2b_ref, *, d_expert):
    i = pl.program_id(0)
    active = i < na_ref[0]
    new_expert = jnp.logical_or(i == 0, be_ref[i] != be_ref[jnp.maximum(i - 1, 0)])

    def ffn(w1_chunk, w2_chunk):
        xb = _unpack_bf16_pairs(x_ref[...])
        hid = jnp.concatenate([jnp.dot(xb, w1_chunk(c), preferred_element_type=F32)
                               for c in range(EXPERT_W_CHUNKS)], axis=1) + b1_ref[0]
        glu = jnp.minimum(hid[:, :d_expert], SWIGLU_LIMIT)
        lin = jnp.clip(hid[:, d_expert:], -SWIGLU_LIMIT, SWIGLU_LIMIT)
        act = (glu * _sigmoid(SWIGLU_ALPHA * glu) * (lin + 1.0)).astype(BF16)
        o_ref[...] = _pack_bf16_pairs(jnp.concatenate([jnp.dot(act, w2_chunk(c), preferred_element_type=F32)
                                                       for c in range(EXPERT_W_CHUNKS)], axis=1) + b2_ref[0])

    def chunk_of(ref, c, lead=()):
        width = ref.shape[-1] // EXPERT_W_CHUNKS
        return ref[lead + (slice(None), slice(c * width, (c + 1) * width))]

    def cast_chunk(src_ref, dst_ref, c):
        wc = chunk_of(src_ref, c, (0,)).astype(BF16)
        width = dst_ref.shape[-1] // EXPERT_W_CHUNKS
        dst_ref[:, c * width:(c + 1) * width] = wc
        return wc

    @pl.when(jnp.logical_and(active, new_expert))
    def _():
        ffn(functools.partial(cast_chunk, w1_ref, w1b_ref), functools.partial(cast_chunk, w2_ref, w2b_ref))

    @pl.when(jnp.logical_and(active, jnp.logical_not(new_expert)))
    def _():
        ffn(functools.partial(chunk_of, w1b_ref), functools.partial(chunk_of, w2b_ref))

    @pl.when(jnp.logical_not(active))
    def _():
        o_ref[...] = jnp.zeros_like(o_ref)


def _experts(block_expert, n_active, xs, w1_all, b1, w2_all, b2, layer, mb):
    n_slots = xs.shape[0]
    depth, n_exp, d, f2 = w1_all.shape
    f = f2 // 2
    nb = n_slots // mb
    w1 = w1_all.reshape(depth * n_exp, d, f2)
    w2 = w2_all.reshape(depth * n_exp, f, d)
    block_expert = block_expert + layer * n_exp
    return pl.pallas_call(
        functools.partial(_expert_kernel, d_expert=f),
        out_shape=jax.ShapeDtypeStruct((n_slots, d // 2), I32),
        grid_spec=pltpu.PrefetchScalarGridSpec(
            num_scalar_prefetch=2,
            grid=(nb,),
            in_specs=[pl.BlockSpec((mb, d // 2), lambda i, be, na: (jnp.minimum(i, na[0] - 1), 0)),
                      pl.BlockSpec((1, d, f2), lambda i, be, na: (be[i], 0, 0)),
                      pl.BlockSpec((1, 1, f2), lambda i, be, na: (be[i], 0, 0)),
                      pl.BlockSpec((1, f, d), lambda i, be, na: (be[i], 0, 0)),
                      pl.BlockSpec((1, 1, d), lambda i, be, na: (be[i], 0, 0))],
            out_specs=pl.BlockSpec((mb, d // 2), lambda i, be, na: (i, 0)),
            scratch_shapes=[pltpu.VMEM((d, f2), BF16), pltpu.VMEM((f, d), BF16)]),
        compiler_params=_cparams("arbitrary"),
        name="moe_experts",
    )(block_expert, n_active, xs, w1, b1.reshape(depth * n_exp, 1, f2), w2, b2.reshape(depth * n_exp, 1, d))


def _combine_kernel(gstart_ref, gcnt_ref, npass_ref, x1_ref, lr_ref, ge_ref, g_ref, b_ref, ys_ref, o_ref,
                    ybuf, acc_ref, sem, *frame_scratch, tm, nb, alpha, frames):
    b = pl.program_id(0)
    slot = b % 2

    def slab_copy(bb, c, e, s, action):
        _slab_dma(gstart_ref, gcnt_ref, bb, c, e, action, lambda start, rows: pltpu.make_async_copy(
            ys_ref.at[pl.ds(start, rows)], ybuf.at[s, pl.ds(e * SLAB, rows)], sem.at[s]), halves=False)

    @pl.when(b == 0)
    def _():
        ybuf[...] = jnp.zeros_like(ybuf)
        for e in range(N_EXPERTS):
            slab_copy(0, 0, e, 0, "start")

    @pl.when(b + 1 < nb)
    def _():
        for e in range(N_EXPERTS):
            slab_copy(b + 1, 0, e, 1 - slot, "start")

    for e in range(N_EXPERTS):
        slab_copy(b, 0, e, slot, "wait")

    lane = lax.broadcasted_iota(I32, (1, LANES), 1)
    lr = lr_ref[...]
    ge = ge_ref[...]

    def weighted_sum(rank0):
        tiles = []
        for col0 in range(0, N_EXPERTS * SLAB, LANES):
            col = col0 + lane
            rank = jnp.full((tm, LANES), -2.0, F32)
            gate = jnp.zeros((tm, LANES), F32)
            want = jnp.zeros((1, LANES), I32)
            for e in range(col0 // SLAB, min((col0 + LANES - 1) // SLAB, N_EXPERTS - 1) + 1):
                inside = jnp.logical_and(col >= e * SLAB, col < (e + 1) * SLAB)
                rank = jnp.where(inside, lr[:, e:e + 1], rank)
                gate = jnp.where(inside, ge[:, e:e + 1], gate)
                want = jnp.where(inside, col - e * SLAB + rank0, want)
            tiles.append(jnp.where(rank == want.astype(F32), gate, 0.0).astype(BF16))
        sel = jnp.concatenate(tiles, axis=1)
        return jnp.dot(sel, _unpack_bf16_pairs(ybuf[slot]), preferred_element_type=F32)

    acc_ref[...] = alpha * x1_ref[...] + weighted_sum(0)

    @pl.when(npass_ref[b] > 1)
    def _():
        def extra_pass(c, carry):
            for action in ("start", "wait"):
                for e in range(N_EXPERTS):
                    @pl.when(gcnt_ref[b * N_EXPERTS + e] > c * SLAB)
                    def _():
                        slab_copy(b, c, e, slot, action)
            acc_ref[...] += weighted_sum(c * SLAB)
            return carry

        lax.fori_loop(1, npass_ref[b], extra_pass, 0)

    result = _layer_norm(acc_ref[...], g_ref[...], b_ref[...])
    if frames is None:
        o_ref[...] = result
        return

    seq_len, lead = frames
    obuf, osem = frame_scratch

    def frame_copies(bb, s, action):
        for piece in range(tm // lead):
            row0 = bb * tm + piece * lead
            sq = jnp.int32(0)
            for j in range(1, nb * tm // seq_len):
                sq = sq + (row0 >= j * seq_len).astype(I32)
            pos = row0 - sq * seq_len

            @pl.when(pos >= lead)
            def _():
                dst = pl.multiple_of(sq * (seq_len - lead) + pos - lead, lead)
                getattr(pltpu.make_async_copy(obuf.at[s, pl.ds(piece * lead, lead)],
                                              o_ref.at[pl.ds(dst, lead)], osem.at[s]), action)()

    @pl.when(b >= 2)
    def _():
        frame_copies(b - 2, slot, "wait")

    obuf[slot] = result
    frame_copies(b, slot, "start")

    @pl.when(b == nb - 1)
    def _():
        frame_copies(b, slot, "wait")
        if nb >= 2:
            frame_copies(b - 1, 1 - slot, "wait")


def _combine(tables, x1, lr, ge, ln_g, ln_b, ys, alpha, tm, frames=None):
    t_all, d = x1.shape
    nb = t_all // tm
    ntab = len(tables)
    row = lambda i, *_: (i, 0)
    const = lambda i, *_: (0, 0)
    scratch = [pltpu.VMEM((2, N_EXPERTS * SLAB, d // 2), I32), pltpu.VMEM((tm, d), F32),
               pltpu.SemaphoreType.DMA((2,))]
    if frames is None:
        out_rows, out_spec = t_all, pl.BlockSpec((tm, d), row)
    else:
        seq_len, lead = frames
        assert tm % lead == 0 and seq_len % lead == 0
        out_rows, out_spec = t_all // seq_len * (seq_len - lead), pl.BlockSpec(memory_space=pl.ANY)
        scratch += [pltpu.VMEM((2, tm, d), F32), pltpu.SemaphoreType.DMA((2,))]
    return pl.pallas_call(
        functools.partial(_combine_kernel, tm=tm, nb=nb, alpha=alpha, frames=frames),
        out_shape=jax.ShapeDtypeStruct((out_rows, d), F32),
        grid_spec=pltpu.PrefetchScalarGridSpec(
            num_scalar_prefetch=ntab,
            grid=(nb,),
            in_specs=[pl.BlockSpec((tm, d), row),
                      pl.BlockSpec((tm, LANES), row),
                      pl.BlockSpec((tm, LANES), row),
                      pl.BlockSpec((1, d), const),
                      pl.BlockSpec((1, d), const),
                      pl.BlockSpec(memory_space=pl.ANY)],
            out_specs=out_spec,
            scratch_shapes=scratch),
        compiler_params=_cparams("arbitrary"),
        name="moe_combine",
    )(*tables, x1, lr, ge, ln_g.reshape(1, d), ln_b.reshape(1, d), ys)


def _lanes(values, offset):
    return jnp.zeros((1, LANES), F32).at[0, offset:offset + values.shape[0]].set(values.astype(F32))


def _split_in_proj(w_in_layer):
    offs = np.concatenate([[0], np.cumsum(IN_PROJ_SIZES)])
    seg = lambda n: w_in_layer[:, offs[n]:offs[n + 1]]
    qk_scale = DIFF_QK_DIM ** -0.5 * LOG2_E
    fox_scale = FOX_HEAD_DIM ** -0.5 * LOG2_E
    wa = jnp.concatenate([seg(0) * qk_scale, seg(1), seg(2), seg(3) * fox_scale, seg(4), seg(5)], axis=1)
    wg = jnp.concatenate([seg(7), seg(10)], axis=1)
    d = w_in_layer.shape[0]
    ws = jnp.zeros((d, LANES), F32)
    ws = ws.at[:, GATE_FF:GATE_FF + FOX_HEADS].set(seg(6))
    ws = ws.at[:, GATE_BETA:GATE_BETA + GDN_HEADS].set(seg(8))
    ws = ws.at[:, GATE_DECAY:GATE_DECAY + GDN_HEADS].set(seg(9))
    return wa.astype(BF16), wg.astype(BF16), ws


def _diff_lambda_init(layer):
    return 0.8 - 0.6 * math.exp(-0.3 * layer)


def _pick_block(seq_len, candidates):
    for c in candidates:
        if seq_len % c == 0:
            return c
    raise ValueError(f"unsupported sequence length {seq_len}")


MOE_ROWS = 512
TOKEN_BLOCK = 256
assert MOE_ROWS % ZERO_FILL_SIZES[0] == 0 and SLAB + MOE_ROWS - GROUP_ALIGN < 2 * ZERO_FILL_SIZES[0]


def _layer(stream, layer, seq_len, n_lead, alpha, p, frames_only):
    t_all, d = stream.shape
    attn_tb = _pick_block(seq_len, (640, 512, 256, 128))
    wa, wg, ws = _split_in_proj(p["w_in"])
    attn, gdn_slab, gates, key_bias = _in_proj(stream, wa, wg, ws, _lanes(p["fox_forget_b"], GATE_FF),
                                               seq_len, n_lead, attn_tb)

    lv = p["diff_lambda"].astype(F32)
    lambda_init = _diff_lambda_init(layer)
    lam = (jnp.exp(jnp.sum(lv[0] * lv[1])) - jnp.exp(jnp.sum(lv[2] * lv[3])) + lambda_init).reshape(1)
    y_diff = _diff_attention(attn, lam, p["diff_subln_g"], seq_len, n_lead, lambda_init, attn_tb)

    y_fox = _fox_attention(attn, key_bias, seq_len, attn_tb)

    conv_w = jnp.zeros((SUBLANES, GDN_CONV_CH), F32).at[:CONV_K].set(p["gdn_conv_w"])
    y_gdn = _gated_deltanet(gdn_slab, gates, conv_w, _lanes(p["gdn_a_log"], GATE_DECAY),
                            _lanes(p["gdn_dt_bias"], GATE_DECAY), p["gdn_norm_g"], seq_len, attn_tb)

    rw = jnp.zeros((d, LANES), F32).at[:, :N_EXPERTS].set(p["router_w"])
    rb = _lanes(p["router_b"], 0)
    tm = TOKEN_BLOCK
    x1, lr, lrt, ge, counts = _mix_router(stream, y_diff, y_fox, y_gdn, p["w_out"].astype(BF16),
                                          p["ln1_g"], p["ln1_b"], rw, rb, alpha, tm, seq_len, n_lead)

    mb = MOE_ROWS
    nb = t_all // tm
    n_slots = -(-(t_all * TOP_K + (GROUP_ALIGN - 1) * nb * N_EXPERTS
                  + N_EXPERTS * (SLAB + mb - GROUP_ALIGN)) // mb) * mb
    n_blocks = n_slots // mb
    cnt = counts[:, 0, :N_EXPERTS].astype(I32)
    rows = (cnt + GROUP_ALIGN - 1) // GROUP_ALIGN * GROUP_ALIGN
    total = jnp.sum(rows, axis=0)
    region = (total + SLAB + mb - 1) // mb * mb
    pad_end = jnp.cumsum(region)
    pad_start = pad_end - region
    gstart = (pad_start[None, :] + jnp.cumsum(rows, axis=0) - rows).reshape(-1)
    npass = jnp.maximum((jnp.max(cnt, axis=1) + SLAB - 1) // SLAB, 1)
    block_first = jnp.arange(n_blocks, dtype=I32) * mb
    block_expert = jnp.minimum(jnp.sum((pad_end[None, :] <= block_first[:, None]).astype(I32), axis=1),
                               N_EXPERTS - 1)
    n_active = (pad_end[-1] // mb).astype(I32).reshape(1)
    group_tables = (gstart.astype(I32), cnt.reshape(-1), npass.astype(I32))

    xs = _dispatch(group_tables + ((pad_start + total).astype(I32), (region - total).astype(I32)),
                   x1, lrt, n_slots, tm)
    ys = _experts(block_expert, n_active, xs, p["expert_w1"], p["expert_b1"], p["expert_w2"], p["expert_b2"],
                  layer, mb)
    frames = (seq_len, n_lead + N_META) if frames_only else None
    return _combine(group_tables, x1, lr, ge, p["ln2_g"], p["ln2_b"], ys, alpha, tm, frames)


def kernel(x, meta_tokens, ln_in_g, ln_in_b, w_in, diff_lambda, diff_subln_g, fox_forget_b, gdn_conv_w,
           gdn_a_log, gdn_dt_bias, gdn_norm_g, w_out, ln1_g, ln1_b, router_w, router_b, expert_w1,
           expert_b1, expert_w2, expert_b2, ln2_g, ln2_b):
    bsz, s, d = x.shape
    depth = w_in.shape[0]
    n_lead = (-(s + N_META)) % Q_BLOCK
    if n_lead + N_META != Q_BLOCK:
        raise ValueError("sequence length must be a multiple of the query block")
    seq_len = n_lead + N_META + s
    alpha = (2 * depth) ** 0.25
    stream = _ln_in(x, meta_tokens.astype(x.dtype), ln_in_g, ln_in_b, n_lead,
                    _pick_block(seq_len, (640, 512, 256, 128))).reshape(bsz * seq_len, d)
    per_layer = dict(w_in=w_in, diff_lambda=diff_lambda, diff_subln_g=diff_subln_g, fox_forget_b=fox_forget_b,
                     gdn_conv_w=gdn_conv_w, gdn_a_log=gdn_a_log, gdn_dt_bias=gdn_dt_bias,
                     gdn_norm_g=gdn_norm_g, w_out=w_out, ln1_g=ln1_g, ln1_b=ln1_b, router_w=router_w,
                     router_b=router_b, expert_w1=expert_w1, expert_b1=expert_b1, expert_w2=expert_w2,
                     expert_b2=expert_b2, ln2_g=ln2_g, ln2_b=ln2_b)
    stacked = ("expert_w1", "expert_b1", "expert_w2", "expert_b2")
    for layer in range(depth):
        p = {k: (v if k in stacked else v[layer]) for k, v in per_layer.items()}
        stream = _layer(stream, layer, seq_len, n_lead, alpha, p, frames_only=layer == depth - 1)
    return stream.reshape(bsz, s, d)
```
